```python
import jax, jax.numpy as jnp
from jax import lax
import numpy as np

D_MODEL = 1024
BATCH = 16
SEQ = 4096
DEPTH = 1

ATTN_HEADS = 8
ATTN_KV_HEADS = 2
ATTN_GROUP = ATTN_HEADS // ATTN_KV_HEADS
ATTN_HEAD_DIM = 64
ATTN_WIDTH = ATTN_HEADS * ATTN_HEAD_DIM
KV_WIDTH = ATTN_KV_HEADS * ATTN_HEAD_DIM
WINDOW = 128
ATTN_BLOCK = 128

HGRN_HEADS = 4
HGRN_DK = 128
HGRN_DV = 128
HGRN_WIDTH = HGRN_HEADS * HGRN_DK
HGRN_V_WIDTH = HGRN_HEADS * HGRN_DV
HGRN_CHUNK = 32

N_BRANCHES = 2
IN_WIDTH = ATTN_WIDTH + 2 * KV_WIDTH + 2 * HGRN_WIDTH + 2 * HGRN_V_WIDTH + N_BRANCHES * D_MODEL

PEER_HEADS = 8
PEER_N_KEYS = 128
PEER_N_EXPERTS = PEER_N_KEYS * PEER_N_KEYS
PEER_KEY_DIM = 128
PEER_TOPK = 16
PEER_TOKEN_BLOCK = 128

EPS = 1e-6
MASK_VALUE = -1e30

kernel_name = "hybrid_swa_hgrn2_peer_gated_block"


def rms_norm(x, g):
    xf = x.astype(jnp.float32)
    y = xf * lax.rsqrt(jnp.mean(xf * xf, axis=-1, keepdims=True) + EPS)
    return (y * g.astype(jnp.float32)).astype(x.dtype)


def alibi_slopes():
    return jnp.asarray(2.0 ** (-8.0 * np.arange(1, ATTN_HEADS + 1) / ATTN_HEADS), dtype=jnp.float32)


def split_combined(proj):
    sizes = (ATTN_WIDTH, KV_WIDTH, KV_WIDTH, HGRN_WIDTH, HGRN_WIDTH,
             HGRN_V_WIDTH, HGRN_V_WIDTH, D_MODEL, D_MODEL)
    offsets = np.cumsum(sizes)[:-1].tolist()
    return jnp.split(proj, offsets, axis=-1)


def sliding_window_attention(q, k, v, sinks):
    B, S = q.shape[0], q.shape[1]
    W = ATTN_BLOCK
    nb = S // W
    qb = q.reshape(B, nb, W, ATTN_KV_HEADS, ATTN_GROUP, ATTN_HEAD_DIM)

    def with_prev(t):
        tb = t.reshape(B, nb, W, ATTN_KV_HEADS, ATTN_HEAD_DIM)
        prev = jnp.concatenate([jnp.zeros_like(tb[:, :1]), tb[:, :-1]], axis=1)
        return jnp.concatenate([prev, tb], axis=2)

    kc, vc = with_prev(k), with_prev(v)
    scale = ATTN_HEAD_DIM ** -0.5
    s = jnp.einsum('bnqkgd,bnskd->bnkgqs', qb, kc,
                   preferred_element_type=jnp.float32) * scale
    qpos = jnp.arange(W)[:, None] + W
    kpos = jnp.arange(2 * W)[None, :]
    dist = qpos - kpos
    blk = jnp.arange(nb)[:, None, None]
    valid = (dist >= 0) & (dist < WINDOW) & (blk * W + kpos - W >= 0)
    slopes = alibi_slopes().reshape(ATTN_KV_HEADS, ATTN_GROUP)
    s = s - slopes[:, :, None, None] * dist.astype(jnp.float32)
    s = jnp.where(valid[None, :, None, None], s, MASK_VALUE)
    sink = jnp.broadcast_to(sinks.astype(jnp.float32).reshape(ATTN_KV_HEADS, ATTN_GROUP)[:, :, None, None],
                            s.shape[:-1] + (1,))
    p = jax.nn.softmax(jnp.concatenate([s, sink], axis=-1), axis=-1)[..., :-1]
    o = jnp.einsum('bnkgqs,bnskd->bnqkgd', p.astype(v.dtype), vc)
    return o.reshape(B, S, ATTN_WIDTH)


def hgrn2_recurrence(q, f_logit, i, lower_bound):
    B, S = q.shape[0], q.shape[1]
    C = HGRN_CHUNK
    nc = S // C
    f32 = jnp.float32
    f = lower_bound + (1.0 - lower_bound) * jax.nn.sigmoid(f_logit.astype(f32))
    k = 1.0 - f
    logf = jnp.log(f)

    def chunk(t):
        return t.reshape(B, nc, C, HGRN_HEADS, t.shape[-1])

    qc, kc, lc, vc = chunk(q.astype(f32)), chunk(k), chunk(logf), chunk(i.astype(f32))
    b = jnp.cumsum(lc, axis=2)
    q_dec = qc * jnp.exp(b)
    k_inv = kc * jnp.exp(-b)
    causal = jnp.tril(jnp.ones((C, C), dtype=bool))
    a = jnp.where(causal, jnp.einsum('bnthd,bnshd->bnhts', q_dec, k_inv), 0.0)
    o_intra = jnp.einsum('bnhts,bnshv->bnthv', a, vc)
    b_last = b[:, :, -1:]
    ds = jnp.einsum('bnshd,bnshv->bnhdv', kc * jnp.exp(b_last - b), vc)
    decay = jnp.exp(b_last[:, :, 0])

    def step(state, inp):
        dec, d = inp
        return dec[..., None] * state + d, state

    s0 = jnp.zeros((B, HGRN_HEADS, HGRN_DK, HGRN_DV), f32)
    _, s_prev = lax.scan(step, s0, (jnp.moveaxis(decay, 1, 0), jnp.moveaxis(ds, 1, 0)))
    s_prev = jnp.moveaxis(s_prev, 0, 1)
    o_inter = jnp.einsum('bnthd,bnhdv->bnthv', q_dec, s_prev)
    return (o_intra + o_inter).reshape(B, S, HGRN_HEADS, HGRN_DV)


def peer_layer(h, w_q, keys, u, v):
    B, S = h.shape[0], h.shape[1]
    q = (h @ w_q).reshape(B, S, PEER_HEADS, 2, PEER_KEY_DIM)
    sc = jnp.einsum('bshpd,phnd->bshpn', q, keys).astype(jnp.float32)
    v1, i1 = lax.top_k(sc[..., 0, :], PEER_TOPK)
    v2, i2 = lax.top_k(sc[..., 1, :], PEER_TOPK)
    cand = (v1[..., :, None] + v2[..., None, :]).reshape(B, S, PEER_HEADS, PEER_TOPK * PEER_TOPK)
    cidx = (i1[..., :, None] * PEER_N_KEYS + i2[..., None, :]).reshape(B, S, PEER_HEADS, PEER_TOPK * PEER_TOPK)
    best, pos = lax.top_k(cand, PEER_TOPK)
    idx = jnp.take_along_axis(cidx, pos, axis=-1)
    gates = jax.nn.softmax(best, axis=-1)
    T = B * S
    nblk = T // PEER_TOKEN_BLOCK
    hk = PEER_HEADS * PEER_TOPK
    hb = h.reshape(nblk, PEER_TOKEN_BLOCK, D_MODEL)
    ib = idx.reshape(nblk, PEER_TOKEN_BLOCK, hk)
    gb = gates.reshape(nblk, PEER_TOKEN_BLOCK, hk).astype(h.dtype)

    def expert_block(args):
        ht, it, gt = args
        ue = jnp.take(u, it, axis=0)
        act = jax.nn.gelu(jnp.einsum('tkd,td->tk', ue, ht), approximate=False)
        ve = jnp.take(v, it, axis=0)
        return jnp.einsum('tk,tkd->td', gt * act, ve)

    out = lax.map(expert_block, (hb, ib, gb))
    return out.reshape(B, S, D_MODEL)


def hybrid_layer(x, norm_mix_g, w_in, sinks, lower_bound, hgrn_norm_g, w_attn_proj,
                 w_hgrn_proj, w_out, norm_ffn_g, w_peer_q, peer_keys, peer_u, peer_v):
    B, S = x.shape[0], x.shape[1]
    h = rms_norm(x, norm_mix_g)
    aq, ak, av, hq, hf, hi, hg, ga, gh = split_combined(h @ w_in)
    y_attn = sliding_window_attention(aq.reshape(B, S, ATTN_HEADS, ATTN_HEAD_DIM),
                                      ak.reshape(B, S, ATTN_KV_HEADS, ATTN_HEAD_DIM),
                                      av.reshape(B, S, ATTN_KV_HEADS, ATTN_HEAD_DIM), sinks)
    o = hgrn2_recurrence(jax.nn.silu(hq).reshape(B, S, HGRN_HEADS, HGRN_DK),
                         hf.reshape(B, S, HGRN_HEADS, HGRN_DK),
                         hi.reshape(B, S, HGRN_HEADS, HGRN_DV), lower_bound)
    o = rms_norm(o, hgrn_norm_g).astype(x.dtype) * jax.nn.silu(hg.reshape(B, S, HGRN_HEADS, HGRN_DV))
    y_hgrn = o.reshape(B, S, HGRN_V_WIDTH)
    merged = jax.nn.sigmoid(ga) * (y_attn @ w_attn_proj) + jax.nn.sigmoid(gh) * (y_hgrn @ w_hgrn_proj)
    x = x + merged @ w_out
    x = x + peer_layer(rms_norm(x, norm_ffn_g), w_peer_q, peer_keys, peer_u, peer_v)
    return x


def setup_inputs(seed: int = 0) -> dict:
    key = jax.random.key(seed)
    ks = jax.random.split(key, 16)
    f32 = jnp.float32
    n = jax.random.normal
    return {
        "x": n(ks[0], (BATCH, SEQ, D_MODEL), f32),
        "norm_mix_g": 1.0 + 0.02 * n(ks[1], (DEPTH, D_MODEL), f32),
        "w_in": n(ks[2], (DEPTH, D_MODEL, IN_WIDTH), f32) * D_MODEL ** -0.5,
        "attn_sinks": n(ks[3], (DEPTH, ATTN_HEADS), f32),
        "hgrn_lb_logits": 0.1 * n(ks[4], (DEPTH + 1, HGRN_WIDTH), f32),
        "hgrn_norm_g": 1.0 + 0.02 * n(ks[5], (DEPTH, HGRN_DV), f32),
        "w_attn_proj": n(ks[6], (DEPTH, ATTN_WIDTH, D_MODEL), f32) * ATTN_WIDTH ** -0.5,
        "w_hgrn_proj": n(ks[7], (DEPTH, HGRN_V_WIDTH, D_MODEL), f32) * HGRN_V_WIDTH ** -0.5,
        "w_out": n(ks[8], (DEPTH, D_MODEL, D_MODEL), f32) * D_MODEL ** -0.5,
        "norm_ffn_g": 1.0 + 0.02 * n(ks[9], (DEPTH, D_MODEL), f32),
        "w_peer_q": n(ks[10], (DEPTH, D_MODEL, PEER_HEADS * 2 * PEER_KEY_DIM), f32) * D_MODEL ** -0.5,
        "peer_keys": n(ks[11], (DEPTH, 2, PEER_HEADS, PEER_N_KEYS, PEER_KEY_DIM), f32) * PEER_KEY_DIM ** -0.5,
        "peer_u": n(ks[12], (DEPTH, PEER_N_EXPERTS, D_MODEL), f32) * D_MODEL ** -0.5,
        "peer_v": n(ks[13], (DEPTH, PEER_N_EXPERTS, D_MODEL), f32) * PEER_HEADS ** -0.5,
        "norm_final_g": 1.0 + 0.02 * n(ks[14], (D_MODEL,), f32),
    }


def reference(x, norm_mix_g, w_in, attn_sinks, hgrn_lb_logits, hgrn_norm_g, w_attn_proj,
              w_hgrn_proj, w_out, norm_ffn_g, w_peer_q, peer_keys, peer_u, peer_v, norm_final_g):
    lower_bounds = jnp.cumsum(jax.nn.softmax(hgrn_lb_logits.astype(jnp.float32), axis=0), axis=0)
    for layer in range(DEPTH):
        x = hybrid_layer(x, norm_mix_g[layer], w_in[layer], attn_sinks[layer],
                         lower_bounds[layer].reshape(HGRN_HEADS, HGRN_DK), hgrn_norm_g[layer],
                         w_attn_proj[layer], w_hgrn_proj[layer], w_out[layer], norm_ffn_g[layer],
                         w_peer_q[layer], peer_keys[layer], peer_u[layer], peer_v[layer])
    return rms_norm(x, norm_final_g)
```

```python
import functools

import jax
import jax.numpy as jnp
import numpy as np
from jax import lax
from jax.experimental import pallas as pl
from jax.experimental.pallas import tpu as pltpu

F32 = jnp.float32
BF16 = jnp.bfloat16

D_MODEL = 1024
ATTN_HEADS = 8
ATTN_KV_HEADS = 2
ATTN_GROUP = ATTN_HEADS // ATTN_KV_HEADS
ATTN_HEAD_DIM = 64
ATTN_WIDTH = ATTN_HEADS * ATTN_HEAD_DIM
KV_WIDTH = ATTN_KV_HEADS * ATTN_HEAD_DIM
WINDOW = 128
ATTN_BLOCK = 128
HGRN_HEADS = 4
HGRN_DK = 128
HGRN_DV = 128
HGRN_WIDTH = HGRN_HEADS * HGRN_DK
HGRN_CHUNK = 32
PEER_HEADS = 8
PEER_N_KEYS = 128
PEER_N_EXPERTS = PEER_N_KEYS * PEER_N_KEYS
PEER_KEY_DIM = 128
PEER_TOPK = 16
EPS = 1e-6
MASK_VALUE = -1e30

SPLIT_SIZES = (ATTN_WIDTH, KV_WIDTH, KV_WIDTH, HGRN_WIDTH, HGRN_WIDTH,
               HGRN_WIDTH, HGRN_WIDTH, D_MODEL, D_MODEL)
IN_WIDTH = sum(SPLIT_SIZES)

VMEM_LIMIT_BYTES = 56 * 1024 * 1024

INPROJ_TOKENS = 512
HGRN_TOKENS = 256
MERGE_TOKENS = 512
ROUTE_TOKENS = 256
PEER_TOKENS = 512
PEER_EXPERT_CHUNK = 512

CAND_FULL_ROWS = 8


def _params(semantics):
    return pltpu.CompilerParams(dimension_semantics=semantics,
                                vmem_limit_bytes=VMEM_LIMIT_BYTES)


def _nt_dot(a, b):
    return lax.dot_general(a, b, (((1,), (1,)), ((), ())), preferred_element_type=F32)


def _rms(x, g):
    ms = jnp.mean(x * x, axis=-1, keepdims=True)
    return x * lax.rsqrt(ms + EPS) * g


def _inproj_kernel(x_ref, g_ref, w_ref, *out_refs):
    h = _rms(x_ref[...], g_ref[...]).astype(BF16)
    off = 0
    for o_ref, width in zip(out_refs, SPLIT_SIZES):
        o_ref[...] = jnp.dot(h, w_ref[:, off:off + width],
                             preferred_element_type=F32).astype(o_ref.dtype)
        off += width


def _inproj(x2, g, w_in):
    t = x2.shape[0]
    tm = min(INPROJ_TOKENS, t)
    out_shape = tuple(jax.ShapeDtypeStruct((t, w), BF16) for w in SPLIT_SIZES)
    return pl.pallas_call(
        _inproj_kernel,
        grid=(t // tm,),
        in_specs=[pl.BlockSpec((tm, D_MODEL), lambda i: (i, 0)),
                  pl.BlockSpec((1, D_MODEL), lambda i: (0, 0)),
                  pl.BlockSpec((D_MODEL, IN_WIDTH), lambda i: (0, 0))],
        out_specs=tuple(pl.BlockSpec((tm, w), lambda i: (i, 0)) for w in SPLIT_SIZES),
        out_shape=out_shape,
        compiler_params=_params(("parallel",)),
        name="inproj",
    )(x2, g, w_in)


def _alibi_slopes():
    return [float(2.0 ** (-8.0 * h / ATTN_HEADS)) for h in range(1, ATTN_HEADS + 1)]


def _attn_kernel(sink_ref, q_ref, kp_ref, kc_ref, vp_ref, vc_ref, o_ref):
    n = pl.program_id(1)
    w = ATTN_BLOCK
    q = q_ref[0]
    k = jnp.concatenate([kp_ref[0], kc_ref[0]], axis=0)
    v = jnp.concatenate([vp_ref[0], vc_ref[0]], axis=0)
    qpos = lax.broadcasted_iota(jnp.int32, (w, 2 * w), 0) + w
    kpos = lax.broadcasted_iota(jnp.int32, (w, 2 * w), 1)
    dist = qpos - kpos
    first_ok = jnp.where(n > 0, 0, w)
    in_window = jnp.where(dist >= 0, jnp.where(dist < WINDOW, 1, 0), 0)
    valid = jnp.where(kpos >= first_ok, in_window, 0) == 1
    distf = dist.astype(F32)
    scale = ATTN_HEAD_DIM ** -0.5
    slopes = _alibi_slopes()
    outs = []
    for kh in range(ATTN_KV_HEADS):
        ksl = k[:, kh * ATTN_HEAD_DIM:(kh + 1) * ATTN_HEAD_DIM]
        vsl = v[:, kh * ATTN_HEAD_DIM:(kh + 1) * ATTN_HEAD_DIM]
        qg = jnp.concatenate(
            [q[:, (kh * ATTN_GROUP + g) * ATTN_HEAD_DIM:(kh * ATTN_GROUP + g + 1) * ATTN_HEAD_DIM]
             for g in range(ATTN_GROUP)], axis=0)
        s_all = _nt_dot(qg, ksl) * scale
        ps, denoms = [], []
        for g in range(ATTN_GROUP):
            head = kh * ATTN_GROUP + g
            s = s_all[g * w:(g + 1) * w] - slopes[head] * distf
            s = jnp.where(valid, s, MASK_VALUE)
            sink = sink_ref[head]
            m = jnp.maximum(jnp.max(s, axis=-1, keepdims=True), sink)
            p = jnp.exp(s - m)
            denoms.append(jnp.sum(p, axis=-1, keepdims=True) + jnp.exp(sink - m))
            ps.append(p.astype(BF16))
        o_all = jnp.dot(jnp.concatenate(ps, axis=0), vsl, preferred_element_type=F32)
        for g in range(ATTN_GROUP):
            outs.append(o_all[g * w:(g + 1) * w] / denoms[g])
    o_ref[0] = jnp.concatenate(outs, axis=-1).astype(o_ref.dtype)


def _attention(aq, ak, av, sinks, b, s):
    nb = s // ATTN_BLOCK
    q3 = aq.reshape(b, s, ATTN_WIDTH)
    k3 = ak.reshape(b, s, KV_WIDTH)
    v3 = av.reshape(b, s, KV_WIDTH)
    cur = lambda i, j: (i, j, 0)
    prev = lambda i, j: (i, jnp.maximum(j - 1, 0), 0)
    out = pl.pallas_call(
        _attn_kernel,
        grid=(b, nb),
        in_specs=[pl.BlockSpec(memory_space=pltpu.SMEM),
                  pl.BlockSpec((1, ATTN_BLOCK, ATTN_WIDTH), cur),
                  pl.BlockSpec((1, ATTN_BLOCK, KV_WIDTH), prev),
                  pl.BlockSpec((1, ATTN_BLOCK, KV_WIDTH), cur),
                  pl.BlockSpec((1, ATTN_BLOCK, KV_WIDTH), prev),
                  pl.BlockSpec((1, ATTN_BLOCK, KV_WIDTH), cur)],
        out_specs=pl.BlockSpec((1, ATTN_BLOCK, ATTN_WIDTH), cur),
        out_shape=jax.ShapeDtypeStruct((b, s, ATTN_WIDTH), BF16),
        compiler_params=_params(("parallel", "parallel")),
        name="attn",
    )(sinks, q3, k3, k3, v3, v3)
    return out.reshape(b * s, ATTN_WIDTH)


def _split3(x):
    hi = x.astype(BF16)
    r1 = x - hi.astype(F32)
    mid = r1.astype(BF16)
    lo = (r1 - mid.astype(F32)).astype(BF16)
    return hi, mid, lo


def _hgrn_kernel(lb_ref, gn_ref, hq_ref, hf_ref, hi_ref, hg_ref, o_ref, st_ref, o_scr):
    sb = hq_ref.shape[1]
    c = HGRN_CHUNK

    @pl.when(pl.program_id(1) == 0)
    def _():
        st_ref[...] = jnp.zeros_like(st_ref)

    logits = lb_ref[...]
    e = jnp.exp(logits - jnp.max(logits, axis=0, keepdims=True))
    lb = e[0:1] / jnp.sum(e, axis=0, keepdims=True)

    f = lb + (1.0 - lb) * jax.nn.sigmoid(hf_ref[0].astype(F32))
    kk = 1.0 - f
    logf = jnp.log(f)
    r = lax.broadcasted_iota(jnp.int32, (sb, sb), 0)
    cc = lax.broadcasted_iota(jnp.int32, (sb, sb), 1)
    same = (r // c) == (cc // c)
    tri = jnp.where(same, jnp.where(cc <= r, 1.0, 0.0), 0.0).astype(BF16)
    blk = jnp.where(same, 1.0, 0.0).astype(BF16)
    parts = _split3(logf)
    bcum = sum(jnp.dot(tri, p, preferred_element_type=F32) for p in parts)
    blast = sum(jnp.dot(blk, p, preferred_element_type=F32) for p in parts)
    hq = hq_ref[0].astype(F32)
    q_dec = (hq * jax.nn.sigmoid(hq) * jnp.exp(bcum)).astype(BF16)
    k_inv = (kk * jnp.exp(-bcum)).astype(BF16)
    k_end = (kk * jnp.exp(blast - bcum)).astype(BF16)
    decay = jnp.exp(blast)
    vv = hi_ref[0]
    v_t = jnp.transpose(vv.astype(F32)).astype(BF16)

    tr = lax.broadcasted_iota(jnp.int32, (c, c), 0)
    tc = lax.broadcasted_iota(jnp.int32, (c, c), 1)
    causal = tc <= tr
    for ci in range(sb // c):
        rows = slice(ci * c, (ci + 1) * c)
        outs = []
        for h in range(HGRN_HEADS):
            lanes = slice(h * HGRN_DK, (h + 1) * HGRN_DK)
            qd = q_dec[rows, lanes]
            a = jnp.where(causal, _nt_dot(qd, k_inv[rows, lanes]), 0.0)
            st = st_ref[h]
            o = jnp.dot(a.astype(BF16), vv[rows, lanes], preferred_element_type=F32)
            o = o + _nt_dot(qd, st.astype(BF16))
            ds = jnp.dot(v_t[lanes, rows], k_end[rows, lanes], preferred_element_type=F32)
            st_ref[h] = st * decay[ci * c:ci * c + 1, lanes] + ds
            outs.append(o)
        o_scr[rows, :] = jnp.concatenate(outs, axis=-1)

    hg = hg_ref[0].astype(F32)
    gate = hg * jax.nn.sigmoid(hg)
    ys = []
    for h in range(HGRN_HEADS):
        lanes = slice(h * HGRN_DV, (h + 1) * HGRN_DV)
        ys.append(_rms(o_scr[:, lanes], gn_ref[...]) * gate[:, lanes])
    o_ref[0] = jnp.concatenate(ys, axis=-1).astype(o_ref.dtype)


def _hgrn(hq, hf, hi, hg, lb_logits, gn, b, s):
    sb = min(HGRN_TOKENS, s)
    spec = pl.BlockSpec((1, sb, HGRN_WIDTH), lambda i, j: (i, j, 0))
    r3 = lambda a: a.reshape(b, s, HGRN_WIDTH)
    out = pl.pallas_call(
        _hgrn_kernel,
        grid=(b, s // sb),
        in_specs=[pl.BlockSpec(lb_logits.shape, lambda i, j: (0, 0)),
                  pl.BlockSpec((1, HGRN_DV), lambda i, j: (0, 0)),
                  spec, spec, spec, spec],
        out_specs=spec,
        out_shape=jax.ShapeDtypeStruct((b, s, HGRN_WIDTH), BF16),
        scratch_shapes=[pltpu.VMEM((HGRN_HEADS, HGRN_DV, HGRN_DK), F32),
                        pltpu.VMEM((sb, HGRN_WIDTH), F32)],
        compiler_params=_params(("parallel", "arbitrary")),
        name="hgrn",
    )(lb_logits, gn, r3(hq), r3(hf), r3(hi), r3(hg))
    return out.reshape(b * s, HGRN_WIDTH)


def _merge_kernel(x_ref, ya_ref, yh_ref, ga_ref, gh_ref, wa_ref, wh_ref, wo_ref, g_ref,
                  x1_ref, hnt_ref):
    pa = jnp.dot(ya_ref[...], wa_ref[...], preferred_element_type=F32)
    ph = jnp.dot(yh_ref[...], wh_ref[...], preferred_element_type=F32)
    merged = (jax.nn.sigmoid(ga_ref[...].astype(F32)) * pa
              + jax.nn.sigmoid(gh_ref[...].astype(F32)) * ph)
    x1 = x_ref[...] + jnp.dot(merged.astype(BF16), wo_ref[...], preferred_element_type=F32)
    x1_ref[...] = x1
    hnt_ref[...] = jnp.transpose(_rms(x1, g_ref[...])).astype(BF16)


def _merge(x2, ya, yh, ga, gh, wa, wh, wo, g):
    t = x2.shape[0]
    tm = min(MERGE_TOKENS, t)
    row = lambda w: pl.BlockSpec((tm, w), lambda i: (i, 0))
    full = lambda a: pl.BlockSpec(a.shape, lambda i: (0, 0))
    return pl.pallas_call(
        _merge_kernel,
        grid=(t // tm,),
        in_specs=[row(D_MODEL), row(ATTN_WIDTH), row(HGRN_WIDTH), row(D_MODEL), row(D_MODEL),
                  full(wa), full(wh), full(wo), full(g)],
        out_specs=(row(D_MODEL), pl.BlockSpec((D_MODEL, tm), lambda i: (0, i))),
        out_shape=(jax.ShapeDtypeStruct((t, D_MODEL), F32),
                   jax.ShapeDtypeStruct((D_MODEL, t), BF16)),
        compiler_params=_params(("parallel",)),
        name="merge",
    )(x2, ya, yh, ga, gh, wa, wh, wo, g)


def _extract_topk(s, n_rows, k):
    lanes = s.shape[1]
    row = lax.broadcasted_iota(jnp.int32, (n_rows, lanes), 0).astype(F32)
    kid = lax.broadcasted_iota(jnp.int32, (k, lanes), 0)

    def body(a, carry):
        s, rank, vals = carry
        m = jnp.max(s, axis=0, keepdims=True)
        first = jnp.min(jnp.where(s == m, row, float(n_rows)), axis=0, keepdims=True)
        sel = row == first
        af = a.astype(F32)
        rank = jnp.where(sel, af, rank)
        vals = jnp.where(kid == a, m, vals)
        s = jnp.where(sel, -jnp.inf, s)
        return s, rank, vals

    init = (s, jnp.full((n_rows, lanes), float(k), F32), jnp.zeros((k, lanes), F32))
    _, rank, vals = lax.fori_loop(0, k, body, init)
    return rank, vals


def _route_kernel(hnt_ref, wq_ref, k1_ref, k2_ref, n1_ref, e1_ref, r2_ref, e2_ref):
    k = PEER_TOPK
    q_t = jnp.dot(wq_ref[...], hnt_ref[...], preferred_element_type=F32).astype(BF16)
    s1 = jnp.dot(k1_ref[0], q_t[:PEER_KEY_DIM], preferred_element_type=F32)
    s2 = jnp.dot(k2_ref[0], q_t[PEER_KEY_DIM:], preferred_element_type=F32)
    lanes = s1.shape[1]
    rank1, v1 = _extract_topk(s1, PEER_N_KEYS, k)
    rank2, v2 = _extract_topk(s2, PEER_N_KEYS, k)

    full = [v1[a:a + 1] + v2 for a in range(CAND_FULL_ROWS)]
    brow = lax.broadcasted_iota(jnp.int32, (k, lanes), 0)
    full = [jnp.where(brow < k // (a + 1), full[a], -jnp.inf) for a in range(CAND_FULL_ROWS)]
    tail = v1[CAND_FULL_ROWS:] + v2[0:1]
    cand = jnp.concatenate(full + [tail], axis=0)
    n_cand = cand.shape[0]
    crow = lax.broadcasted_iota(jnp.int32, (n_cand, lanes), 0).astype(F32)

    def body(_, c):
        m = jnp.max(c, axis=0, keepdims=True)
        first = jnp.min(jnp.where(c == m, crow, float(n_cand)), axis=0, keepdims=True)
        return jnp.where(crow == first, -jnp.inf, c)

    left = lax.fori_loop(0, k, body, cand)
    chosen = jnp.where(left == -jnp.inf, jnp.where(cand > -jnp.inf, 1.0, 0.0), 0.0)
    top = v1[0:1] + v2[0:1]
    z = jnp.sum(chosen * jnp.exp(jnp.where(cand > -jnp.inf, cand, top) - top), axis=0, keepdims=True)
    counts = [jnp.sum(chosen[a * k:(a + 1) * k], axis=0, keepdims=True) for a in range(CAND_FULL_ROWS)]
    counts += [chosen[CAND_FULL_ROWS * k + a:CAND_FULL_ROWS * k + a + 1] for a in range(k - CAND_FULL_ROWS)]
    n1 = jnp.zeros_like(rank1)
    for a in range(k):
        n1 = jnp.where(rank1 == float(a), counts[a], n1)
    n1_ref[0] = n1
    e1_ref[0] = jnp.exp(s1 - v1[0:1])
    r2_ref[0] = rank2
    e2_ref[0] = jnp.exp(s2 - v2[0:1]) / z


def _route(hnt, wq_t, keys1, keys2):
    t = hnt.shape[1]
    tb = min(ROUTE_TOKENS, t)
    kd2 = 2 * PEER_KEY_DIM
    out = jax.ShapeDtypeStruct((PEER_HEADS, PEER_N_KEYS, t), F32)
    ospec = pl.BlockSpec((1, PEER_N_KEYS, tb), lambda i, h: (h, 0, i))
    kspec = pl.BlockSpec((1, PEER_N_KEYS, PEER_KEY_DIM), lambda i, h: (h, 0, 0))
    return pl.pallas_call(
        _route_kernel,
        grid=(t // tb, PEER_HEADS),
        in_specs=[pl.BlockSpec((D_MODEL, tb), lambda i, h: (0, i)),
                  pl.BlockSpec((kd2, D_MODEL), lambda i, h: (h, 0)),
                  kspec, kspec],
        out_specs=(ospec, ospec, ospec, ospec),
        out_shape=(out, out, out, out),
        compiler_params=_params(("parallel", "arbitrary")),
        name="route",
    )(hnt, wq_t, keys1, keys2)


def _peer_kernel(hnt_ref, u_ref, vt_ref, n1_ref, e1_ref, r2_ref, e2_ref, x1_ref, g_ref,
                 o_ref, acc_ref):
    c = pl.program_id(1)
    n_i = u_ref.shape[0] // PEER_N_KEYS

    @pl.when(c == 0)
    def _():
        acc_ref[...] = jnp.zeros_like(acc_ref)

    act = jnp.dot(u_ref[...], hnt_ref[...], preferred_element_type=F32)
    act = 0.5 * act * (1.0 + lax.erf(act * (2.0 ** -0.5)))
    pieces = []
    for ii in range(n_i):
        i = c * n_i + ii
        w = None
        for h in range(PEER_HEADS):
            n1 = n1_ref[h, pl.ds(i, 1), :]
            e1 = e1_ref[h, pl.ds(i, 1), :]
            term = jnp.where(r2_ref[h] < n1, e2_ref[h], 0.0) * e1
            w = term if w is None else w + term
        pieces.append((w * act[ii * PEER_N_KEYS:(ii + 1) * PEER_N_KEYS]).astype(BF16))
    d_t = jnp.concatenate(pieces, axis=0)
    acc_ref[...] += jnp.dot(vt_ref[...], d_t, preferred_element_type=F32)

    @pl.when(c == pl.num_programs(1) - 1)
    def _():
        y = x1_ref[...] + jnp.transpose(acc_ref[...])
        o_ref[...] = _rms(y, g_ref[...])


def _peer(hnt, u, v_t, n1, e1, r2, e2, x1, g):
    t = x1.shape[0]
    tb = min(PEER_TOKENS, t)
    ec = PEER_EXPERT_CHUNK
    mspec = pl.BlockSpec((PEER_HEADS, PEER_N_KEYS, tb), lambda i, c: (0, 0, i))
    return pl.pallas_call(
        _peer_kernel,
        grid=(t // tb, PEER_N_EXPERTS // ec),
        in_specs=[pl.BlockSpec((D_MODEL, tb), lambda i, c: (0, i)),
                  pl.BlockSpec((ec, D_MODEL), lambda i, c: (c, 0)),
                  pl.BlockSpec((D_MODEL, ec), lambda i, c: (0, c)),
                  mspec, mspec, mspec, mspec,
                  pl.BlockSpec((tb, D_MODEL), lambda i, c: (i, 0)),
                  pl.BlockSpec((1, D_MODEL), lambda i, c: (0, 0))],
        out_specs=pl.BlockSpec((tb, D_MODEL), lambda i, c: (i, 0)),
        out_shape=jax.ShapeDtypeStruct((t, D_MODEL), F32),
        scratch_shapes=[pltpu.VMEM((D_MODEL, tb), F32)],
        compiler_params=_params(("parallel", "arbitrary")),
        name="peer",
    )(hnt, u, v_t, n1, e1, r2, e2, x1, g)


def kernel(x, norm_mix_g, w_in, attn_sinks, hgrn_lb_logits, hgrn_norm_g, w_attn_proj, w_hgrn_proj,
           w_out, norm_ffn_g, w_peer_q, peer_keys, peer_u, peer_v, norm_final_g):
    b, s, d = x.shape
    assert d == D_MODEL and norm_mix_g.shape[0] == 1
    t = b * s
    x2 = x.reshape(t, d)
    row = lambda a: a.reshape(1, -1).astype(F32)

    aq, ak, av, hq, hf, hi, hg, ga, gh = _inproj(x2, row(norm_mix_g[0]), w_in[0].astype(BF16))
    y_attn = _attention(aq, ak, av, attn_sinks[0].astype(F32), b, s)
    y_hgrn = _hgrn(hq, hf, hi, hg, hgrn_lb_logits.astype(F32), row(hgrn_norm_g[0]), b, s)
    x1, hnt = _merge(x2, y_attn, y_hgrn, ga, gh, w_attn_proj[0].astype(BF16),
                     w_hgrn_proj[0].astype(BF16), w_out[0].astype(BF16), row(norm_ffn_g[0]))
    wq_t = jnp.transpose(w_peer_q[0]).astype(BF16)
    keys = peer_keys[0].astype(BF16)
    n1, e1, r2, e2 = _route(hnt, wq_t, keys[0], keys[1])
    out = _peer(hnt, peer_u[0].astype(BF16), jnp.transpose(peer_v[0]).astype(BF16),
                n1, e1, r2, e2, x1, row(norm_final_g))
    return out.reshape(b, s, d)
```

```python
import functools

import jax
import jax.numpy as jnp
import numpy as np
from jax import lax
from jax.experimental import pallas as pl
from jax.experimental.pallas import tpu as pltpu

F32 = jnp.float32
BF16 = jnp.bfloat16

D_MODEL = 1024
ATTN_HEADS = 8
ATTN_KV_HEADS = 2
ATTN_GROUP = ATTN_HEADS // ATTN_KV_HEADS
ATTN_HEAD_DIM = 64
ATTN_WIDTH = ATTN_HEADS * ATTN_HEAD_DIM
KV_WIDTH = ATTN_KV_HEADS * ATTN_HEAD_DIM
WINDOW = 128
ATTN_BLOCK = 128
HGRN_HEADS = 4
HGRN_DK = 128
HGRN_DV = 128
HGRN_WIDTH = HGRN_HEADS * HGRN_DK
HGRN_CHUNK = 32
PEER_HEADS = 8
PEER_N_KEYS = 128
PEER_N_EXPERTS = PEER_N_KEYS * PEER_N_KEYS
PEER_KEY_DIM = 128
PEER_TOPK = 16
EPS = 1e-6
MASK_VALUE = -1e30

SPLIT_SIZES = (ATTN_WIDTH, KV_WIDTH, KV_WIDTH, HGRN_WIDTH, HGRN_WIDTH,
               HGRN_WIDTH, HGRN_WIDTH, D_MODEL, D_MODEL)
IN_WIDTH = sum(SPLIT_SIZES)

VMEM_LIMIT_BYTES = 56 * 1024 * 1024

INPROJ_TOKENS = 512
HGRN_TOKENS = 256
MERGE_TOKENS = 512
ROUTE_TOKENS = 256
PEER_TOKENS = 512
PEER_EXPERT_CHUNK = 1024
LANE_TILE = 128
PEER_MXU_SPLIT = 4

CAND_FULL_ROWS = 8


def _params(semantics):
    return pltpu.CompilerParams(dimension_semantics=semantics,
                                vmem_limit_bytes=VMEM_LIMIT_BYTES)


def _nt_dot(a, b):
    return lax.dot_general(a, b, (((1,), (1,)), ((), ())), preferred_element_type=F32)


def _rms(x, g):
    ms = jnp.mean(x * x, axis=-1, keepdims=True)
    return x * lax.rsqrt(ms + EPS) * g


def _inproj_kernel(x_ref, g_ref, w_ref, *out_refs):
    h = _rms(x_ref[...], g_ref[...]).astype(BF16)
    off = 0
    for o_ref, width in zip(out_refs, SPLIT_SIZES):
        o_ref[...] = jnp.dot(h, w_ref[:, off:off + width],
                             preferred_element_type=F32).astype(o_ref.dtype)
        off += width


def _inproj(x2, g, w_in):
    t = x2.shape[0]
    tm = min(INPROJ_TOKENS, t)
    out_shape = tuple(jax.ShapeDtypeStruct((t, w), BF16) for w in SPLIT_SIZES)
    return pl.pallas_call(
        _inproj_kernel,
        grid=(t // tm,),
        in_specs=[pl.BlockSpec((tm, D_MODEL), lambda i: (i, 0)),
                  pl.BlockSpec((1, D_MODEL), lambda i: (0, 0)),
                  pl.BlockSpec((D_MODEL, IN_WIDTH), lambda i: (0, 0))],
        out_specs=tuple(pl.BlockSpec((tm, w), lambda i: (i, 0)) for w in SPLIT_SIZES),
        out_shape=out_shape,
        compiler_params=_params(("parallel",)),
        name="inproj",
    )(x2, g, w_in)


def _alibi_slopes():
    return [float(2.0 ** (-8.0 * h / ATTN_HEADS)) for h in range(1, ATTN_HEADS + 1)]


def _attn_kernel(sink_ref, q_ref, kp_ref, kc_ref, vp_ref, vc_ref, o_ref):
    n = pl.program_id(1)
    w = ATTN_BLOCK
    q = q_ref[0]
    k = jnp.concatenate([kp_ref[0], kc_ref[0]], axis=0)
    v = jnp.concatenate([vp_ref[0], vc_ref[0]], axis=0)
    qpos = lax.broadcasted_iota(jnp.int32, (w, 2 * w), 0) + w
    kpos = lax.broadcasted_iota(jnp.int32, (w, 2 * w), 1)
    dist = qpos - kpos
    first_ok = jnp.where(n > 0, 0, w)
    in_window = jnp.where(dist >= 0, jnp.where(dist < WINDOW, 1, 0), 0)
    valid = jnp.where(kpos >= first_ok, in_window, 0) == 1
    distf = dist.astype(F32)
    scale = ATTN_HEAD_DIM ** -0.5
    slopes = _alibi_slopes()
    outs = []
    for kh in range(ATTN_KV_HEADS):
        ksl = k[:, kh * ATTN_HEAD_DIM:(kh + 1) * ATTN_HEAD_DIM]
        vsl = v[:, kh * ATTN_HEAD_DIM:(kh + 1) * ATTN_HEAD_DIM]
        qg = jnp.concatenate(
            [q[:, (kh * ATTN_GROUP + g) * ATTN_HEAD_DIM:(kh * ATTN_GROUP + g + 1) * ATTN_HEAD_DIM]
             for g in range(ATTN_GROUP)], axis=0)
        s_all = _nt_dot(qg, ksl) * scale
        ps, denoms = [], []
        for g in range(ATTN_GROUP):
            head = kh * ATTN_GROUP + g
            s = s_all[g * w:(g + 1) * w] - slopes[head] * distf
            s = jnp.where(valid, s, MASK_VALUE)
            sink = sink_ref[head]
            m = jnp.maximum(jnp.max(s, axis=-1, keepdims=True), sink)
            p = jnp.exp(s - m)
            denoms.append(jnp.sum(p, axis=-1, keepdims=True) + jnp.exp(sink - m))
            ps.append(p.astype(BF16))
        o_all = jnp.dot(jnp.concatenate(ps, axis=0), vsl, preferred_element_type=F32)
        for g in range(ATTN_GROUP):
            outs.append(o_all[g * w:(g + 1) * w] / denoms[g])
    o_ref[0] = jnp.concatenate(outs, axis=-1).astype(o_ref.dtype)


def _attention(aq, ak, av, sinks, b, s):
    nb = s // ATTN_BLOCK
    q3 = aq.reshape(b, s, ATTN_WIDTH)
    k3 = ak.reshape(b, s, KV_WIDTH)
    v3 = av.reshape(b, s, KV_WIDTH)
    cur = lambda i, j: (i, j, 0)
    prev = lambda i, j: (i, jnp.maximum(j - 1, 0), 0)
    out = pl.pallas_call(
        _attn_kernel,
        grid=(b, nb),
        in_specs=[pl.BlockSpec(memory_space=pltpu.SMEM),
                  pl.BlockSpec((1, ATTN_BLOCK, ATTN_WIDTH), cur),
                  pl.BlockSpec((1, ATTN_BLOCK, KV_WIDTH), prev),
                  pl.BlockSpec((1, ATTN_BLOCK, KV_WIDTH), cur),
                  pl.BlockSpec((1, ATTN_BLOCK, KV_WIDTH), prev),
                  pl.BlockSpec((1, ATTN_BLOCK, KV_WIDTH), cur)],
        out_specs=pl.BlockSpec((1, ATTN_BLOCK, ATTN_WIDTH), cur),
        out_shape=jax.ShapeDtypeStruct((b, s, ATTN_WIDTH), BF16),
        compiler_params=_params(("parallel", "parallel")),
        name="attn",
    )(sinks, q3, k3, k3, v3, v3)
    return out.reshape(b * s, ATTN_WIDTH)


def _split3(x):
    hi = x.astype(BF16)
    r1 = x - hi.astype(F32)
    mid = r1.astype(BF16)
    lo = (r1 - mid.astype(F32)).astype(BF16)
    return hi, mid, lo


def _hgrn_kernel(lb_ref, gn_ref, hq_ref, hf_ref, hi_ref, hg_ref, o_ref, st_ref, o_scr):
    sb = hq_ref.shape[1]
    c = HGRN_CHUNK

    @pl.when(pl.program_id(1) == 0)
    def _():
        st_ref[...] = jnp.zeros_like(st_ref)

    logits = lb_ref[...]
    e = jnp.exp(logits - jnp.max(logits, axis=0, keepdims=True))
    lb = e[0:1] / jnp.sum(e, axis=0, keepdims=True)

    f = lb + (1.0 - lb) * jax.nn.sigmoid(hf_ref[0].astype(F32))
    kk = 1.0 - f
    logf = jnp.log(f)
    r = lax.broadcasted_iota(jnp.int32, (sb, sb), 0)
    cc = lax.broadcasted_iota(jnp.int32, (sb, sb), 1)
    same = (r // c) == (cc // c)
    tri = jnp.where(same, jnp.where(cc <= r, 1.0, 0.0), 0.0).astype(BF16)
    blk = jnp.where(same, 1.0, 0.0).astype(BF16)
    parts = _split3(logf)
    bcum = sum(jnp.dot(tri, p, preferred_element_type=F32) for p in parts)
    blast = sum(jnp.dot(blk, p, preferred_element_type=F32) for p in parts)
    hq = hq_ref[0].astype(F32)
    q_dec = (hq * jax.nn.sigmoid(hq) * jnp.exp(bcum)).astype(BF16)
    k_inv = (kk * jnp.exp(-bcum)).astype(BF16)
    k_end = (kk * jnp.exp(blast - bcum)).astype(BF16)
    decay = jnp.exp(blast)
    vv = hi_ref[0]
    v_t = jnp.transpose(vv.astype(F32)).astype(BF16)

    tr = lax.broadcasted_iota(jnp.int32, (c, c), 0)
    tc = lax.broadcasted_iota(jnp.int32, (c, c), 1)
    causal = tc <= tr
    for ci in range(sb // c):
        rows = slice(ci * c, (ci + 1) * c)
        outs = []
        for h in range(HGRN_HEADS):
            lanes = slice(h * HGRN_DK, (h + 1) * HGRN_DK)
            qd = q_dec[rows, lanes]
            a = jnp.where(causal, _nt_dot(qd, k_inv[rows, lanes]), 0.0)
            st = st_ref[h]
            o = jnp.dot(a.astype(BF16), vv[rows, lanes], preferred_element_type=F32)
            o = o + _nt_dot(qd, st.astype(BF16))
            ds = jnp.dot(v_t[lanes, rows], k_end[rows, lanes], preferred_element_type=F32)
            st_ref[h] = st * decay[ci * c:ci * c + 1, lanes] + ds
            outs.append(o)
        o_scr[rows, :] = jnp.concatenate(outs, axis=-1)

    hg = hg_ref[0].astype(F32)
    gate = hg * jax.nn.sigmoid(hg)
    ys = []
    for h in range(HGRN_HEADS):
        lanes = slice(h * HGRN_DV, (h + 1) * HGRN_DV)
        ys.append(_rms(o_scr[:, lanes], gn_ref[...]) * gate[:, lanes])
    o_ref[0] = jnp.concatenate(ys, axis=-1).astype(o_ref.dtype)


def _hgrn(hq, hf, hi, hg, lb_logits, gn, b, s):
    sb = min(HGRN_TOKENS, s)
    spec = pl.BlockSpec((1, sb, HGRN_WIDTH), lambda i, j: (i, j, 0))
    r3 = lambda a: a.reshape(b, s, HGRN_WIDTH)
    out = pl.pallas_call(
        _hgrn_kernel,
        grid=(b, s // sb),
        in_specs=[pl.BlockSpec(lb_logits.shape, lambda i, j: (0, 0)),
                  pl.BlockSpec((1, HGRN_DV), lambda i, j: (0, 0)),
                  spec, spec, spec, spec],
        out_specs=spec,
        out_shape=jax.ShapeDtypeStruct((b, s, HGRN_WIDTH), BF16),
        scratch_shapes=[pltpu.VMEM((HGRN_HEADS, HGRN_DV, HGRN_DK), F32),
                        pltpu.VMEM((sb, HGRN_WIDTH), F32)],
        compiler_params=_params(("parallel", "arbitrary")),
        name="hgrn",
    )(lb_logits, gn, r3(hq), r3(hf), r3(hi), r3(hg))
    return out.reshape(b * s, HGRN_WIDTH)


def _merge_kernel(x_ref, ya_ref, yh_ref, ga_ref, gh_ref, wa_ref, wh_ref, wo_ref, g_ref,
                  x1_ref, hnt_ref):
    pa = jnp.dot(ya_ref[...], wa_ref[...], preferred_element_type=F32)
    ph = jnp.dot(yh_ref[...], wh_ref[...], preferred_element_type=F32)
    merged = (jax.nn.sigmoid(ga_ref[...].astype(F32)) * pa
              + jax.nn.sigmoid(gh_ref[...].astype(F32)) * ph)
    x1 = x_ref[...] + jnp.dot(merged.astype(BF16), wo_ref[...], preferred_element_type=F32)
    x1_ref[...] = x1
    hnt_ref[...] = jnp.transpose(_rms(x1, g_ref[...])).astype(BF16)


def _merge(x2, ya, yh, ga, gh, wa, wh, wo, g):
    t = x2.shape[0]
    tm = min(MERGE_TOKENS, t)
    row = lambda w: pl.BlockSpec((tm, w), lambda i: (i, 0))
    full = lambda a: pl.BlockSpec(a.shape, lambda i: (0, 0))
    return pl.pallas_call(
        _merge_kernel,
        grid=(t // tm,),
        in_specs=[row(D_MODEL), row(ATTN_WIDTH), row(HGRN_WIDTH), row(D_MODEL), row(D_MODEL),
                  full(wa), full(wh), full(wo), full(g)],
        out_specs=(row(D_MODEL), pl.BlockSpec((D_MODEL, tm), lambda i: (0, i))),
        out_shape=(jax.ShapeDtypeStruct((t, D_MODEL), F32),
                   jax.ShapeDtypeStruct((D_MODEL, t), BF16)),
        compiler_params=_params(("parallel",)),
        name="merge",
    )(x2, ya, yh, ga, gh, wa, wh, wo, g)


def _extract_topk(s, n_rows, k):
    lanes = s.shape[1]
    row = lax.broadcasted_iota(jnp.int32, (n_rows, lanes), 0).astype(F32)
    kid = lax.broadcasted_iota(jnp.int32, (k, lanes), 0)

    def body(a, carry):
        s, rank, vals = carry
        m = jnp.max(s, axis=0, keepdims=True)
        first = jnp.min(jnp.where(s == m, row, float(n_rows)), axis=0, keepdims=True)
        sel = row == first
        af = a.astype(F32)
        rank = jnp.where(sel, af, rank)
        vals = jnp.where(kid == a, m, vals)
        s = jnp.where(sel, -jnp.inf, s)
        return s, rank, vals

    init = (s, jnp.full((n_rows, lanes), float(k), F32), jnp.zeros((k, lanes), F32))
    _, rank, vals = lax.fori_loop(0, k, body, init)
    return rank, vals


def _pack_rows(x):
    return pltpu.bitcast(x, jnp.uint32)


def _unpack_rows(x):
    return pltpu.bitcast(x, BF16)


def _route_kernel(hnt_ref, wq_ref, k1_ref, k2_ref, n1_ref, e1_ref, r2_ref, e2_ref):
    k = PEER_TOPK
    q_t = jnp.dot(wq_ref[...], hnt_ref[...], preferred_element_type=F32).astype(BF16)
    s1 = jnp.dot(k1_ref[0], q_t[:PEER_KEY_DIM], preferred_element_type=F32)
    s2 = jnp.dot(k2_ref[0], q_t[PEER_KEY_DIM:], preferred_element_type=F32)
    lanes = s1.shape[1]
    rank1, v1 = _extract_topk(s1, PEER_N_KEYS, k)
    rank2, v2 = _extract_topk(s2, PEER_N_KEYS, k)

    full = [v1[a:a + 1] + v2 for a in range(CAND_FULL_ROWS)]
    brow = lax.broadcasted_iota(jnp.int32, (k, lanes), 0)
    full = [jnp.where(brow < k // (a + 1), full[a], -jnp.inf) for a in range(CAND_FULL_ROWS)]
    tail = v1[CAND_FULL_ROWS:] + v2[0:1]
    cand = jnp.concatenate(full + [tail], axis=0)
    n_cand = cand.shape[0]
    crow = lax.broadcasted_iota(jnp.int32, (n_cand, lanes), 0).astype(F32)

    def body(_, c):
        m = jnp.max(c, axis=0, keepdims=True)
        first = jnp.min(jnp.where(c == m, crow, float(n_cand)), axis=0, keepdims=True)
        return jnp.where(crow == first, -jnp.inf, c)

    left = lax.fori_loop(0, k, body, cand)
    chosen = jnp.where(left == -jnp.inf, jnp.where(cand > -jnp.inf, 1.0, 0.0), 0.0)
    top = v1[0:1] + v2[0:1]
    z = jnp.sum(chosen * jnp.exp(jnp.where(cand > -jnp.inf, cand, top) - top), axis=0, keepdims=True)
    counts = [jnp.sum(chosen[a * k:(a + 1) * k], axis=0, keepdims=True) for a in range(CAND_FULL_ROWS)]
    counts += [chosen[CAND_FULL_ROWS * k + a:CAND_FULL_ROWS * k + a + 1] for a in range(k - CAND_FULL_ROWS)]
    n1 = jnp.zeros_like(rank1)
    for a in range(k):
        n1 = jnp.where(rank1 == float(a), counts[a], n1)
    n1_ref[0] = n1
    e1_ref[0] = jnp.exp(s1 - v1[0:1])
    r2_ref[0] = _pack_rows(rank2.astype(BF16))
    e2_ref[0] = _pack_rows((jnp.exp(s2 - v2[0:1]) / z).astype(BF16))


def _route(hnt, wq_t, keys1, keys2):
    t = hnt.shape[1]
    tb = min(ROUTE_TOKENS, t)
    kd2 = 2 * PEER_KEY_DIM
    out = jax.ShapeDtypeStruct((PEER_HEADS, PEER_N_KEYS, t), F32)
    out16 = jax.ShapeDtypeStruct((PEER_HEADS, PEER_N_KEYS // 2, t), jnp.uint32)
    pspec = pl.BlockSpec((1, PEER_N_KEYS // 2, tb), lambda i, h: (h, 0, i))
    ospec = pl.BlockSpec((1, PEER_N_KEYS, tb), lambda i, h: (h, 0, i))
    kspec = pl.BlockSpec((1, PEER_N_KEYS, PEER_KEY_DIM), lambda i, h: (h, 0, 0))
    return pl.pallas_call(
        _route_kernel,
        grid=(t // tb, PEER_HEADS),
        in_specs=[pl.BlockSpec((D_MODEL, tb), lambda i, h: (0, i)),
                  pl.BlockSpec((kd2, D_MODEL), lambda i, h: (h, 0)),
                  kspec, kspec],
        out_specs=(ospec, ospec, pspec, pspec),
        out_shape=(out, out, out16, out16),
        compiler_params=_params(("parallel", "arbitrary")),
        name="route",
    )(hnt, wq_t, keys1, keys2)


def _peer_step(c, hnt_ref, u_ref, vt_ref, n1_ref, e1_ref, r2_ref, e2_ref, acc_ref,
               act_w, act_r, d_w, d_r):
    nk = PEER_N_KEYS
    ec = u_ref.shape[0]
    n_i = ec // nk
    tb = hnt_ref.shape[1]

    def act_piece(p):
        rows = slice(p * ec // PEER_MXU_SPLIT, (p + 1) * ec // PEER_MXU_SPLIT)
        act_w[rows, :] = jnp.dot(u_ref[rows, :], hnt_ref[...], preferred_element_type=F32)

    def acc_piece(p):
        rows = slice(p * D_MODEL // PEER_MXU_SPLIT, (p + 1) * D_MODEL // PEER_MXU_SPLIT)
        part = jnp.dot(vt_ref[rows, :], _unpack_rows(d_r[...]), preferred_element_type=F32)
        acc_ref[rows, :] = jnp.where(c > 2, acc_ref[rows, :], 0.0) + part

    def gate_block(lt, i0):
        lanes = slice(lt * LANE_TILE, (lt + 1) * LANE_TILE)
        pair = (i0, i0 + 1)
        ws = [None, None]
        for h in range(PEER_HEADS):
            keys = slice(h * nk // 2, (h + 1) * nk // 2)
            r2 = _unpack_rows(r2_ref[keys, lanes])
            e2 = _unpack_rows(e2_ref[keys, lanes])
            for k, ii in enumerate(pair):
                n1 = n1_ref[h, ii:ii + 1, lanes].astype(BF16)
                e1 = e1_ref[h, ii:ii + 1, lanes].astype(BF16)
                term = jnp.where(r2 < n1, e2, 0.0) * e1
                ws[k] = term if ws[k] is None else ws[k] + term
        for k, ii in enumerate(pair):
            a = act_r[ii * nk:(ii + 1) * nk, lanes]
            gelu = (0.5 * a) * (1.0 + lax.erf(a * (2.0 ** -0.5)))
            d_w[ii * nk // 2:(ii + 1) * nk // 2, lanes] = _pack_rows(ws[k] * gelu.astype(BF16))

    mxu = [functools.partial(f, p) for p in range(PEER_MXU_SPLIT) for f in (act_piece, acc_piece)]
    vpu = [functools.partial(gate_block, lt, i0)
           for lt in range(tb // LANE_TILE) for i0 in range(0, n_i, 2)]
    per = -(-len(vpu) // len(mxu))
    for k, piece in enumerate(mxu):
        piece()
        for block in vpu[k * per:(k + 1) * per]:
            block()


def _peer_kernel(hnt_ref, u_ref, vt_ref, n1_ref, e1_ref, r2_ref, e2_ref, x1_ref, g_ref,
                 o_ref, acc_ref, act_a, act_b, d_a, d_b):
    c = pl.program_id(1)
    args = (c, hnt_ref, u_ref, vt_ref, n1_ref, e1_ref, r2_ref, e2_ref, acc_ref)

    @pl.when(c == 0)
    def _():
        act_b[...] = jnp.zeros_like(act_b)
        d_a[...] = jnp.zeros_like(d_a)
        acc_ref[...] = jnp.zeros_like(acc_ref)

    @pl.when(c % 2 == 0)
    def _():
        _peer_step(*args, act_w=act_a, act_r=act_b, d_w=d_b, d_r=d_a)

    @pl.when(c % 2 == 1)
    def _():
        _peer_step(*args, act_w=act_b, act_r=act_a, d_w=d_a, d_r=d_b)

    @pl.when(c == pl.num_programs(1) - 1)
    def _():
        y = x1_ref[...] + jnp.transpose(acc_ref[...])
        o_ref[...] = _rms(y, g_ref[...])


def _peer(hnt, u, v_t, n1, e1, r2, e2, x1, g):
    t = x1.shape[0]
    tb = min(PEER_TOKENS, t)
    ec = PEER_EXPERT_CHUNK
    packed_rows = PEER_HEADS * PEER_N_KEYS // 2
    dense = pl.BlockSpec((packed_rows, tb), lambda i, c: (0, i))
    r2 = r2.reshape(packed_rows, t)
    e2 = e2.reshape(packed_rows, t)
    nc = PEER_N_EXPERTS // ec
    stage = lambda c, lag: jnp.clip(c - lag, 0, nc - 1)
    rows = pl.BlockSpec((PEER_HEADS, ec // PEER_N_KEYS, tb), lambda i, c: (0, stage(c, 1), i))
    return pl.pallas_call(
        _peer_kernel,
        grid=(t // tb, nc + 2),
        in_specs=[pl.BlockSpec((D_MODEL, tb), lambda i, c: (0, i)),
                  pl.BlockSpec((ec, D_MODEL), lambda i, c: (stage(c, 0), 0)),
                  pl.BlockSpec((D_MODEL, ec), lambda i, c: (0, stage(c, 2))),
                  rows, rows, dense, dense,
                  pl.BlockSpec((tb, D_MODEL), lambda i, c: (i, 0)),
                  pl.BlockSpec((1, D_MODEL), lambda i, c: (0, 0))],
        out_specs=pl.BlockSpec((tb, D_MODEL), lambda i, c: (i, 0)),
        out_shape=jax.ShapeDtypeStruct((t, D_MODEL), F32),
        scratch_shapes=[pltpu.VMEM((D_MODEL, tb), F32),
                        pltpu.VMEM((ec, tb), F32), pltpu.VMEM((ec, tb), F32),
                        pltpu.VMEM((ec // 2, tb), jnp.uint32), pltpu.VMEM((ec // 2, tb), jnp.uint32)],
        compiler_params=_params(("parallel", "arbitrary")),
        name="peer",
    )(hnt, u, v_t, n1, e1, r2, e2, x1, g)


def kernel(x, norm_mix_g, w_in, attn_sinks, hgrn_lb_logits, hgrn_norm_g, w_attn_proj, w_hgrn_proj,
           w_out, norm_ffn_g, w_peer_q, peer_keys, peer_u, peer_v, norm_final_g):
    b, s, d = x.shape
    assert d == D_MODEL and norm_mix_g.shape[0] == 1
    t = b * s
    x2 = x.reshape(t, d)
    row = lambda a: a.reshape(1, -1).astype(F32)

    aq, ak, av, hq, hf, hi, hg, ga, gh = _inproj(x2, row(norm_mix_g[0]), w_in[0].astype(BF16))
    y_attn = _attention(aq, ak, av, attn_sinks[0].astype(F32), b, s)
    y_hgrn = _hgrn(hq, hf, hi, hg, hgrn_lb_logits.astype(F32), row(hgrn_norm_g[0]), b, s)
    x1, hnt = _merge(x2, y_attn, y_hgrn, ga, gh, w_attn_proj[0].astype(BF16),
                     w_hgrn_proj[0].astype(BF16), w_out[0].astype(BF16), row(norm_ffn_g[0]))
    wq_t = jnp.transpose(w_peer_q[0]).astype(BF16)
    keys = peer_keys[0].astype(BF16)
    n1, e1, r2, e2 = _route(hnt, wq_t, keys[0], keys[1])
    out = _peer(hnt, peer_u[0].astype(BF16), jnp.transpose(peer_v[0]).astype(BF16),
                n1, e1, r2, e2, x1, row(norm_final_g))
    return out.reshape(b, s, d)
```

```python
import functools

import jax
import jax.numpy as jnp
import numpy as np
from jax import lax
from jax.experimental import pallas as pl
from jax.experimental.pallas import tpu as pltpu

F32 = jnp.float32
BF16 = jnp.bfloat16

D_MODEL = 1024
ATTN_HEADS = 8
ATTN_KV_HEADS = 2
ATTN_GROUP = ATTN_HEADS // ATTN_KV_HEADS
ATTN_HEAD_DIM = 64
ATTN_WIDTH = ATTN_HEADS * ATTN_HEAD_DIM
KV_WIDTH = ATTN_KV_HEADS * ATTN_HEAD_DIM
WINDOW = 128
ATTN_BLOCK = 128
HGRN_HEADS = 4
HGRN_DK = 128
HGRN_DV = 128
HGRN_WIDTH = HGRN_HEADS * HGRN_DK
HGRN_CHUNK = 32
PEER_HEADS = 8
PEER_N_KEYS = 128
PEER_N_EXPERTS = PEER_N_KEYS * PEER_N_KEYS
PEER_KEY_DIM = 128
PEER_TOPK = 16
EPS = 1e-6
MASK_VALUE = -1e30

SPLIT_SIZES = (ATTN_WIDTH, KV_WIDTH, KV_WIDTH, HGRN_WIDTH, HGRN_WIDTH,
               HGRN_WIDTH, HGRN_WIDTH, D_MODEL, D_MODEL)
IN_WIDTH = sum(SPLIT_SIZES)

VMEM_LIMIT_BYTES = 56 * 1024 * 1024

INPROJ_TOKENS = 512
HGRN_TOKENS = 256
MERGE_TOKENS = 512
ROUTE_TOKENS = 256
PEER_TOKENS = 512
PEER_EXPERT_CHUNK = 1024
LANE_TILE = 128
PEER_MXU_SPLIT = 4

CAND_FULL_ROWS = 8


def _params(semantics):
    return pltpu.CompilerParams(dimension_semantics=semantics,
                                vmem_limit_bytes=VMEM_LIMIT_BYTES)


def _nt_dot(a, b):
    return lax.dot_general(a, b, (((1,), (1,)), ((), ())), preferred_element_type=F32)


def _rms(x, g):
    ms = jnp.mean(x * x, axis=-1, keepdims=True)
    return x * lax.rsqrt(ms + EPS) * g


def _inproj_kernel(x_ref, g_ref, w_ref, *out_refs):
    h = _rms(x_ref[...], g_ref[...]).astype(BF16)
    off = 0
    for o_ref, width in zip(out_refs, SPLIT_SIZES):
        o_ref[...] = jnp.dot(h, w_ref[:, off:off + width],
                             preferred_element_type=F32).astype(o_ref.dtype)
        off += width


def _inproj(x2, g, w_in):
    t = x2.shape[0]
    tm = min(INPROJ_TOKENS, t)
    out_shape = tuple(jax.ShapeDtypeStruct((t, w), BF16) for w in SPLIT_SIZES)
    return pl.pallas_call(
        _inproj_kernel,
        grid=(t // tm,),
        in_specs=[pl.BlockSpec((tm, D_MODEL), lambda i: (i, 0)),
                  pl.BlockSpec((1, D_MODEL), lambda i: (0, 0)),
                  pl.BlockSpec((D_MODEL, IN_WIDTH), lambda i: (0, 0))],
        out_specs=tuple(pl.BlockSpec((tm, w), lambda i: (i, 0)) for w in SPLIT_SIZES),
        out_shape=out_shape,
        compiler_params=_params(("parallel",)),
        name="inproj",
    )(x2, g, w_in)


def _alibi_slopes():
    return [float(2.0 ** (-8.0 * h / ATTN_HEADS)) for h in range(1, ATTN_HEADS + 1)]


def _attn_kernel(sink_ref, q_ref, kp_ref, kc_ref, vp_ref, vc_ref, o_ref):
    n = pl.program_id(1)
    w = ATTN_BLOCK
    q = q_ref[0]
    k = jnp.concatenate([kp_ref[0], kc_ref[0]], axis=0)
    v = jnp.concatenate([vp_ref[0], vc_ref[0]], axis=0)
    qpos = lax.broadcasted_iota(jnp.int32, (w, 2 * w), 0) + w
    kpos = lax.broadcasted_iota(jnp.int32, (w, 2 * w), 1)
    dist = qpos - kpos
    first_ok = jnp.where(n > 0, 0, w)
    in_window = jnp.where(dist >= 0, jnp.where(dist < WINDOW, 1, 0), 0)
    valid = jnp.where(kpos >= first_ok, in_window, 0) == 1
    distf = dist.astype(F32)
    scale = ATTN_HEAD_DIM ** -0.5
    slopes = _alibi_slopes()
    outs = []
    for kh in range(ATTN_KV_HEADS):
        ksl = k[:, kh * ATTN_HEAD_DIM:(kh + 1) * ATTN_HEAD_DIM]
        vsl = v[:, kh * ATTN_HEAD_DIM:(kh + 1) * ATTN_HEAD_DIM]
        qg = jnp.concatenate(
            [q[:, (kh * ATTN_GROUP + g) * ATTN_HEAD_DIM:(kh * ATTN_GROUP + g + 1) * ATTN_HEAD_DIM]
             for g in range(ATTN_GROUP)], axis=0)
        s_all = _nt_dot(qg, ksl) * scale
        ps, denoms = [], []
        for g in range(ATTN_GROUP):
            head = kh * ATTN_GROUP + g
            s = s_all[g * w:(g + 1) * w] - slopes[head] * distf
            s = jnp.where(valid, s, MASK_VALUE)
            sink = sink_ref[head]
            m = jnp.maximum(jnp.max(s, axis=-1, keepdims=True), sink)
            p = jnp.exp(s - m)
            denoms.append(jnp.sum(p, axis=-1, keepdims=True) + jnp.exp(sink - m))
            ps.append(p.astype(BF16))
        o_all = jnp.dot(jnp.concatenate(ps, axis=0), vsl, preferred_element_type=F32)
        for g in range(ATTN_GROUP):
            outs.append(o_all[g * w:(g + 1) * w] / denoms[g])
    o_ref[0] = jnp.concatenate(outs, axis=-1).astype(o_ref.dtype)


def _attention(aq, ak, av, sinks, b, s):
    nb = s // ATTN_BLOCK
    q3 = aq.reshape(b, s, ATTN_WIDTH)
    k3 = ak.reshape(b, s, KV_WIDTH)
    v3 = av.reshape(b, s, KV_WIDTH)
    cur = lambda i, j: (i, j, 0)
    prev = lambda i, j: (i, jnp.maximum(j - 1, 0), 0)
    out = pl.pallas_call(
        _attn_kernel,
        grid=(b, nb),
        in_specs=[pl.BlockSpec(memory_space=pltpu.SMEM),
                  pl.BlockSpec((1, ATTN_BLOCK, ATTN_WIDTH), cur),
                  pl.BlockSpec((1, ATTN_BLOCK, KV_WIDTH), prev),
                  pl.BlockSpec((1, ATTN_BLOCK, KV_WIDTH), cur),
                  pl.BlockSpec((1, ATTN_BLOCK, KV_WIDTH), prev),
                  pl.BlockSpec((1, ATTN_BLOCK, KV_WIDTH), cur)],
        out_specs=pl.BlockSpec((1, ATTN_BLOCK, ATTN_WIDTH), cur),
        out_shape=jax.ShapeDtypeStruct((b, s, ATTN_WIDTH), BF16),
        compiler_params=_params(("parallel", "parallel")),
        name="attn",
    )(sinks, q3, k3, k3, v3, v3)
    return out.reshape(b * s, ATTN_WIDTH)


def _split3(x):
    hi = x.astype(BF16)
    r1 = x - hi.astype(F32)
    mid = r1.astype(BF16)
    lo = (r1 - mid.astype(F32)).astype(BF16)
    return hi, mid, lo


def _hgrn_kernel(lb_ref, gn_ref, hq_ref, hf_ref, hi_ref, hg_ref, o_ref, st_ref, o_scr):
    sb = hq_ref.shape[1]
    c = HGRN_CHUNK

    @pl.when(pl.program_id(1) == 0)
    def _():
        st_ref[...] = jnp.zeros_like(st_ref)

    logits = lb_ref[...]
    e = jnp.exp(logits - jnp.max(logits, axis=0, keepdims=True))
    lb = e[0:1] / jnp.sum(e, axis=0, keepdims=True)

    f = lb + (1.0 - lb) * jax.nn.sigmoid(hf_ref[0].astype(F32))
    kk = 1.0 - f
    logf = jnp.log(f)
    r = lax.broadcasted_iota(jnp.int32, (sb, sb), 0)
    cc = lax.broadcasted_iota(jnp.int32, (sb, sb), 1)
    same = (r // c) == (cc // c)
    tri = jnp.where(same, jnp.where(cc <= r, 1.0, 0.0), 0.0).astype(BF16)
    blk = jnp.where(same, 1.0, 0.0).astype(BF16)
    parts = _split3(logf)
    bcum = sum(jnp.dot(tri, p, preferred_element_type=F32) for p in parts)
    blast = sum(jnp.dot(blk, p, preferred_element_type=F32) for p in parts)
    hq = hq_ref[0].astype(F32)
    q_dec = (hq * jax.nn.sigmoid(hq) * jnp.exp(bcum)).astype(BF16)
    k_inv = (kk * jnp.exp(-bcum)).astype(BF16)
    k_end = (kk * jnp.exp(blast - bcum)).astype(BF16)
    decay = jnp.exp(blast)
    vv = hi_ref[0]
    v_t = jnp.transpose(vv.astype(F32)).astype(BF16)

    tr = lax.broadcasted_iota(jnp.int32, (c, c), 0)
    tc = lax.broadcasted_iota(jnp.int32, (c, c), 1)
    causal = tc <= tr
    for ci in range(sb // c):
        rows = slice(ci * c, (ci + 1) * c)
        outs = []
        for h in range(HGRN_HEADS):
            lanes = slice(h * HGRN_DK, (h + 1) * HGRN_DK)
            qd = q_dec[rows, lanes]
            a = jnp.where(causal, _nt_dot(qd, k_inv[rows, lanes]), 0.0)
            st = st_ref[h]
            o = jnp.dot(a.astype(BF16), vv[rows, lanes], preferred_element_type=F32)
            o = o + _nt_dot(qd, st.astype(BF16))
            ds = jnp.dot(v_t[lanes, rows], k_end[rows, lanes], preferred_element_type=F32)
            st_ref[h] = st * decay[ci * c:ci * c + 1, lanes] + ds
            outs.append(o)
        o_scr[rows, :] = jnp.concatenate(outs, axis=-1)

    hg = hg_ref[0].astype(F32)
    gate = hg * jax.nn.sigmoid(hg)
    ys = []
    for h in range(HGRN_HEADS):
        lanes = slice(h * HGRN_DV, (h + 1) * HGRN_DV)
        ys.append(_rms(o_scr[:, lanes], gn_ref[...]) * gate[:, lanes])
    o_ref[0] = jnp.concatenate(ys, axis=-1).astype(o_ref.dtype)


def _hgrn(hq, hf, hi, hg, lb_logits, gn, b, s):
    sb = min(HGRN_TOKENS, s)
    spec = pl.BlockSpec((1, sb, HGRN_WIDTH), lambda i, j: (i, j, 0))
    r3 = lambda a: a.reshape(b, s, HGRN_WIDTH)
    out = pl.pallas_call(
        _hgrn_kernel,
        grid=(b, s // sb),
        in_specs=[pl.BlockSpec(lb_logits.shape, lambda i, j: (0, 0)),
                  pl.BlockSpec((1, HGRN_DV), lambda i, j: (0, 0)),
                  spec, spec, spec, spec],
        out_specs=spec,
        out_shape=jax.ShapeDtypeStruct((b, s, HGRN_WIDTH), BF16),
        scratch_shapes=[pltpu.VMEM((HGRN_HEADS, HGRN_DV, HGRN_DK), F32),
                        pltpu.VMEM((sb, HGRN_WIDTH), F32)],
        compiler_params=_params(("parallel", "arbitrary")),
        name="hgrn",
    )(lb_logits, gn, r3(hq), r3(hf), r3(hi), r3(hg))
    return out.reshape(b * s, HGRN_WIDTH)


def _merge_kernel(x_ref, ya_ref, yh_ref, ga_ref, gh_ref, wa_ref, wh_ref, wo_ref, g_ref,
                  x1_ref, hnt_ref):
    pa = jnp.dot(ya_ref[...], wa_ref[...], preferred_element_type=F32)
    ph = jnp.dot(yh_ref[...], wh_ref[...], preferred_element_type=F32)
    merged = (jax.nn.sigmoid(ga_ref[...].astype(F32)) * pa
              + jax.nn.sigmoid(gh_ref[...].astype(F32)) * ph)
    x1 = x_ref[...] + jnp.dot(merged.astype(BF16), wo_ref[...], preferred_element_type=F32)
    x1_ref[...] = x1
    hnt_ref[...] = jnp.transpose(_rms(x1, g_ref[...])).astype(BF16)


def _merge(x2, ya, yh, ga, gh, wa, wh, wo, g):
    t = x2.shape[0]
    tm = min(MERGE_TOKENS, t)
    row = lambda w: pl.BlockSpec((tm, w), lambda i: (i, 0))
    full = lambda a: pl.BlockSpec(a.shape, lambda i: (0, 0))
    return pl.pallas_call(
        _merge_kernel,
        grid=(t // tm,),
        in_specs=[row(D_MODEL), row(ATTN_WIDTH), row(HGRN_WIDTH), row(D_MODEL), row(D_MODEL),
                  full(wa), full(wh), full(wo), full(g)],
        out_specs=(row(D_MODEL), pl.BlockSpec((D_MODEL, tm), lambda i: (0, i))),
        out_shape=(jax.ShapeDtypeStruct((t, D_MODEL), F32),
                   jax.ShapeDtypeStruct((D_MODEL, t), BF16)),
        compiler_params=_params(("parallel",)),
        name="merge",
    )(x2, ya, yh, ga, gh, wa, wh, wo, g)


def _as_f32(i):
    return jnp.asarray(i, jnp.int32).astype(F32)


def _topk_exact(s, k):
    n_rows, lanes = s.shape
    row = lax.broadcasted_iota(jnp.int32, (n_rows, lanes), 0).astype(F32)
    kid = lax.broadcasted_iota(jnp.int32, (k, lanes), 0).astype(F32)

    def body(a, carry):
        s, rank, vals = carry
        af = _as_f32(a)
        m = jnp.max(s, axis=0, keepdims=True)
        first = jnp.min(jnp.where(s == m, row, float(n_rows)), axis=0, keepdims=True)
        sel = row == first
        return jnp.where(sel, -jnp.inf, s), jnp.where(sel, af, rank), jnp.where(kid == af, m, vals)

    init = (s, jnp.full((n_rows, lanes), float(k), F32), jnp.zeros((k, lanes), F32))
    _, rank, vals = lax.fori_loop(0, k, body, init)
    return rank, vals


RANK_CODE_SCALE = 2.0 ** 123
RANK_CODE_LIMIT = -float(PEER_TOPK) * RANK_CODE_SCALE


def _topk_quick_pair(s1, s2, k):
    kid = lax.broadcasted_iota(jnp.int32, (k, s1.shape[1]), 0).astype(F32)

    def body(a, carry):
        c1, c2, v1, v2 = carry
        af = _as_f32(a)
        code = -(af + float(k)) * RANK_CODE_SCALE
        m1 = jnp.max(c1, axis=0, keepdims=True)
        m2 = jnp.max(c2, axis=0, keepdims=True)
        hit = kid == af
        return (jnp.where(c1 == m1, code, c1), jnp.where(c2 == m2, code, c2),
                jnp.where(hit, m1, v1), jnp.where(hit, m2, v2))

    zeros = jnp.zeros((k, s1.shape[1]), F32)
    c1, c2, v1, v2 = lax.fori_loop(0, k, body, (s1, s2, zeros, zeros))

    def decode(c, s):
        marked = c <= RANK_CODE_LIMIT
        rank = jnp.where(marked, c * (-1.0 / RANK_CODE_SCALE) - float(k), float(k))
        n_marked = jnp.sum(jnp.where(marked, 1.0, 0.0), axis=0, keepdims=True)
        in_range = jnp.min(s, axis=0, keepdims=True) > RANK_CODE_LIMIT
        return rank, jnp.where(n_marked == float(k), jnp.where(in_range, 1.0, 0.0), 0.0)

    rank1, ok1 = decode(c1, s1)
    rank2, ok2 = decode(c2, s2)
    return rank1, v1, rank2, v2, ok1 * ok2


def _candidates(v1, v2, k):
    brow = lax.broadcasted_iota(jnp.int32, (k, v1.shape[1]), 0)
    full = [jnp.where(brow < k // (a + 1), v1[a:a + 1] + v2, -jnp.inf) for a in range(CAND_FULL_ROWS)]
    tail = v1[CAND_FULL_ROWS:] + v2[0:1]
    return jnp.concatenate(full + [tail], axis=0)


def _choose_exact(cand, k):
    n_cand, lanes = cand.shape
    crow = lax.broadcasted_iota(jnp.int32, (n_cand, lanes), 0).astype(F32)

    def body(_, c):
        m = jnp.max(c, axis=0, keepdims=True)
        first = jnp.min(jnp.where(c == m, crow, float(n_cand)), axis=0, keepdims=True)
        return jnp.where(crow == first, -jnp.inf, c)

    return lax.fori_loop(0, k, body, cand)


def _choose_quick(cand, k):
    def body(_, c):
        return jnp.where(c == jnp.max(c, axis=0, keepdims=True), -jnp.inf, c)

    return lax.fori_loop(0, k, body, cand)


def _pack_rows(x):
    return pltpu.bitcast(x, jnp.uint32)


def _unpack_rows(x):
    return pltpu.bitcast(x, BF16)


def _route_outputs(s1, s2, rank1, v1, rank2, v2, cand, left, k):
    chosen = jnp.where(left == -jnp.inf, jnp.where(cand > -jnp.inf, 1.0, 0.0), 0.0)
    top = v1[0:1] + v2[0:1]
    z = jnp.sum(chosen * jnp.exp(cand - top), axis=0, keepdims=True)
    counts = [jnp.sum(chosen[a * k:(a + 1) * k], axis=0, keepdims=True) for a in range(CAND_FULL_ROWS)]
    counts += [chosen[CAND_FULL_ROWS * k + a:CAND_FULL_ROWS * k + a + 1] for a in range(k - CAND_FULL_ROWS)]
    n1 = jnp.zeros_like(rank1)
    for a in range(k):
        n1 = jnp.where(rank1 == float(a), counts[a], n1)
    e1 = jnp.exp(s1 - v1[0:1])
    r2 = _pack_rows(rank2.astype(BF16))
    e2 = _pack_rows((jnp.exp(s2 - v2[0:1]) / z).astype(BF16))
    return (n1, e1, r2, e2), jnp.sum(chosen, axis=0, keepdims=True)


def _route_kernel(hnt_ref, wq_ref, k1_ref, k2_ref, n1_ref, e1_ref, r2_ref, e2_ref):
    k = PEER_TOPK
    out_refs = (n1_ref, e1_ref, r2_ref, e2_ref)
    q_t = jnp.dot(wq_ref[...], hnt_ref[...], preferred_element_type=F32).astype(BF16)
    s1_all = jnp.dot(k1_ref[0], q_t[:PEER_KEY_DIM], preferred_element_type=F32)
    s2_all = jnp.dot(k2_ref[0], q_t[PEER_KEY_DIM:], preferred_element_type=F32)
    for lt in range(s1_all.shape[1] // LANE_TILE):
        lanes = slice(lt * LANE_TILE, (lt + 1) * LANE_TILE)
        s1, s2 = s1_all[:, lanes], s2_all[:, lanes]

        def store(outs):
            for ref, val in zip(out_refs, outs):
                ref[0, :, lanes] = val

        rank1, v1, rank2, v2, ok = _topk_quick_pair(s1, s2, k)
        cand = _candidates(v1, v2, k)
        outs, n_chosen = _route_outputs(s1, s2, rank1, v1, rank2, v2, cand, _choose_quick(cand, k), k)
        store(outs)
        ok = ok * jnp.where(n_chosen == float(k), 1.0, 0.0)

        @pl.when(jnp.min(ok) < 1.0)
        def _():
            rank1, v1 = _topk_exact(s1, k)
            rank2, v2 = _topk_exact(s2, k)
            cand = _candidates(v1, v2, k)
            outs, _ = _route_outputs(s1, s2, rank1, v1, rank2, v2, cand, _choose_exact(cand, k), k)
            store(outs)


def _route(hnt, wq_t, keys1, keys2):
    t = hnt.shape[1]
    tb = min(ROUTE_TOKENS, t)
    kd2 = 2 * PEER_KEY_DIM
    out = jax.ShapeDtypeStruct((PEER_HEADS, PEER_N_KEYS, t), F32)
    out16 = jax.ShapeDtypeStruct((PEER_HEADS, PEER_N_KEYS // 2, t), jnp.uint32)
    pspec = pl.BlockSpec((1, PEER_N_KEYS // 2, tb), lambda i, h: (h, 0, i))
    ospec = pl.BlockSpec((1, PEER_N_KEYS, tb), lambda i, h: (h, 0, i))
    kspec = pl.BlockSpec((1, PEER_N_KEYS, PEER_KEY_DIM), lambda i, h: (h, 0, 0))
    return pl.pallas_call(
        _route_kernel,
        grid=(t // tb, PEER_HEADS),
        in_specs=[pl.BlockSpec((D_MODEL, tb), lambda i, h: (0, i)),
                  pl.BlockSpec((kd2, D_MODEL), lambda i, h: (h, 0)),
                  kspec, kspec],
        out_specs=(ospec, ospec, pspec, pspec),
        out_shape=(out, out, out16, out16),
        compiler_params=_params(("parallel", "arbitrary")),
        name="route",
    )(hnt, wq_t, keys1, keys2)


def _peer_step(c, hnt_ref, u_ref, vt_ref, n1_ref, e1_ref, r2_ref, e2_ref, acc_ref,
               act_w, act_r, d_w, d_r):
    nk = PEER_N_KEYS
    ec = u_ref.shape[0]
    n_i = ec // nk
    tb = hnt_ref.shape[1]

    def act_piece(p):
        rows = slice(p * ec // PEER_MXU_SPLIT, (p + 1) * ec // PEER_MXU_SPLIT)
        act_w[rows, :] = jnp.dot(u_ref[rows, :], hnt_ref[...], preferred_element_type=F32)

    def acc_piece(p):
        rows = slice(p * D_MODEL // PEER_MXU_SPLIT, (p + 1) * D_MODEL // PEER_MXU_SPLIT)
        part = jnp.dot(vt_ref[rows, :], _unpack_rows(d_r[...]), preferred_element_type=F32)
        acc_ref[rows, :] = jnp.where(c > 2, acc_ref[rows, :], 0.0) + part

    def gate_block(lt, i0):
        lanes = slice(lt * LANE_TILE, (lt + 1) * LANE_TILE)
        pair = (i0, i0 + 1)
        ws = [None, None]
        for h in range(PEER_HEADS):
            keys = slice(h * nk // 2, (h + 1) * nk // 2)
            r2 = _unpack_rows(r2_ref[keys, lanes])
            e2 = _unpack_rows(e2_ref[keys, lanes])
            for k, ii in enumerate(pair):
                n1 = n1_ref[h, ii:ii + 1, lanes].astype(BF16)
                e1 = e1_ref[h, ii:ii + 1, lanes].astype(BF16)
                term = jnp.where(r2 < n1, e2, 0.0) * e1
                ws[k] = term if ws[k] is None else ws[k] + term
        for k, ii in enumerate(pair):
            a = act_r[ii * nk:(ii + 1) * nk, lanes]
            gelu = (0.5 * a) * (1.0 + lax.erf(a * (2.0 ** -0.5)))
            d_w[ii * nk // 2:(ii + 1) * nk // 2, lanes] = _pack_rows(ws[k] * gelu.astype(BF16))

    mxu = [functools.partial(f, p) for p in range(PEER_MXU_SPLIT) for f in (act_piece, acc_piece)]
    vpu = [functools.partial(gate_block, lt, i0)
           for lt in range(tb // LANE_TILE) for i0 in range(0, n_i, 2)]
    per = -(-len(vpu) // len(mxu))
    for k, piece in enumerate(mxu):
        piece()
        for block in vpu[k * per:(k + 1) * per]:
            block()


def _peer_kernel(hnt_ref, u_ref, vt_ref, n1_ref, e1_ref, r2_ref, e2_ref, x1_ref, g_ref,
                 o_ref, acc_ref, act_a, act_b, d_a, d_b):
    c = pl.program_id(1)
    args = (c, hnt_ref, u_ref, vt_ref, n1_ref, e1_ref, r2_ref, e2_ref, acc_ref)

    @pl.when(c == 0)
    def _():
        act_b[...] = jnp.zeros_like(act_b)
        d_a[...] = jnp.zeros_like(d_a)
        acc_ref[...] = jnp.zeros_like(acc_ref)

    @pl.when(c % 2 == 0)
    def _():
        _peer_step(*args, act_w=act_a, act_r=act_b, d_w=d_b, d_r=d_a)

    @pl.when(c % 2 == 1)
    def _():
        _peer_step(*args, act_w=act_b, act_r=act_a, d_w=d_a, d_r=d_b)

    @pl.when(c == pl.num_programs(1) - 1)
    def _():
        y = x1_ref[...] + jnp.transpose(acc_ref[...])
        o_ref[...] = _rms(y, g_ref[...])


def _peer(hnt, u, v_t, n1, e1, r2, e2, x1, g):
    t = x1.shape[0]
    tb = min(PEER_TOKENS, t)
    ec = PEER_EXPERT_CHUNK
    packed_rows = PEER_HEADS * PEER_N_KEYS // 2
    dense = pl.BlockSpec((packed_rows, tb), lambda i, c: (0, i))
    r2 = r2.reshape(packed_rows, t)
    e2 = e2.reshape(packed_rows, t)
    nc = PEER_N_EXPERTS // ec
    stage = lambda c, lag: jnp.clip(c - lag, 0, nc - 1)
    rows = pl.BlockSpec((PEER_HEADS, ec // PEER_N_KEYS, tb), lambda i, c: (0, stage(c, 1), i))
    return pl.pallas_call(
        _peer_kernel,
        grid=(t // tb, nc + 2),
        in_specs=[pl.BlockSpec((D_MODEL, tb), lambda i, c: (0, i)),
                  pl.BlockSpec((ec, D_MODEL), lambda i, c: (stage(c, 0), 0)),
                  pl.BlockSpec((D_MODEL, ec), lambda i, c: (0, stage(c, 2))),
                  rows, rows, dense, dense,
                  pl.BlockSpec((tb, D_MODEL), lambda i, c: (i, 0)),
                  pl.BlockSpec((1, D_MODEL), lambda i, c: (0, 0))],
        out_specs=pl.BlockSpec((tb, D_MODEL), lambda i, c: (i, 0)),
        out_shape=jax.ShapeDtypeStruct((t, D_MODEL), F32),
        scratch_shapes=[pltpu.VMEM((D_MODEL, tb), F32),
                        pltpu.VMEM((ec, tb), F32), pltpu.VMEM((ec, tb), F32),
                        pltpu.VMEM((ec // 2, tb), jnp.uint32), pltpu.VMEM((ec // 2, tb), jnp.uint32)],
        compiler_params=_params(("parallel", "arbitrary")),
        name="peer",
    )(hnt, u, v_t, n1, e1, r2, e2, x1, g)


def kernel(x, norm_mix_g, w_in, attn_sinks, hgrn_lb_logits, hgrn_norm_g, w_attn_proj, w_hgrn_proj,
           w_out, norm_ffn_g, w_peer_q, peer_keys, peer_u, peer_v, norm_final_g):
    b, s, d = x.shape
    assert d == D_MODEL and norm_mix_g.shape[0] == 1
    t = b * s
    x2 = x.reshape(t, d)
    row = lambda a: a.reshape(1, -1).astype(F32)

    aq, ak, av, hq, hf, hi, hg, ga, gh = _inproj(x2, row(norm_mix_g[0]), w_in[0].astype(BF16))
    y_attn = _attention(aq, ak, av, attn_sinks[0].astype(F32), b, s)
    y_hgrn = _hgrn(hq, hf, hi, hg, hgrn_lb_logits.astype(F32), row(hgrn_norm_g[0]), b, s)
    x1, hnt = _merge(x2, y_attn, y_hgrn, ga, gh, w_attn_proj[0].astype(BF16),
                     w_hgrn_proj[0].astype(BF16), w_out[0].astype(BF16), row(norm_ffn_g[0]))
    wq_t = jnp.transpose(w_peer_q[0]).astype(BF16)
    keys = peer_keys[0].astype(BF16)
    n1, e1, r2, e2 = _route(hnt, wq_t, keys[0], keys[1])
    out = _peer(hnt, peer_u[0].astype(BF16), jnp.transpose(peer_v[0]).astype(BF16),
                n1, e1, r2, e2, x1, row(norm_final_g))
    return out.reshape(b, s, d)
```

```python
import functools

import jax
import jax.numpy as jnp
import numpy as np
from jax import lax
from jax.experimental import pallas as pl
from jax.experimental.pallas import tpu as pltpu

F32 = jnp.float32
BF16 = jnp.bfloat16

D_MODEL = 1024
ATTN_HEADS = 8
ATTN_KV_HEADS = 2
ATTN_GROUP = ATTN_HEADS // ATTN_KV_HEADS
ATTN_HEAD_DIM = 64
ATTN_WIDTH = ATTN_HEADS * ATTN_HEAD_DIM
KV_WIDTH = ATTN_KV_HEADS * ATTN_HEAD_DIM
WINDOW = 128
ATTN_BLOCK = 128
HGRN_HEADS = 4
HGRN_DK = 128
HGRN_DV = 128
HGRN_WIDTH = HGRN_HEADS * HGRN_DK
HGRN_CHUNK = 32
PEER_HEADS = 8
PEER_N_KEYS = 128
PEER_N_EXPERTS = PEER_N_KEYS * PEER_N_KEYS
PEER_KEY_DIM = 128
PEER_TOPK = 16
EPS = 1e-6
MASK_VALUE = -1e30

SPLIT_SIZES = (ATTN_WIDTH, KV_WIDTH, KV_WIDTH, HGRN_WIDTH, HGRN_WIDTH,
               HGRN_WIDTH, HGRN_WIDTH, D_MODEL, D_MODEL)
IN_WIDTH = sum(SPLIT_SIZES)

VMEM_LIMIT_BYTES = 56 * 1024 * 1024

INPROJ_TOKENS = 512
HGRN_TOKENS = 256
MERGE_TOKENS = 512
ROUTE_TOKENS = 256
PEER_TOKENS = 512
PEER_EXPERT_CHUNK = 1024
LANE_TILE = 128
PEER_MXU_SPLIT = 4

CAND_FULL_ROWS = 8


def _params(semantics):
    return pltpu.CompilerParams(dimension_semantics=semantics,
                                vmem_limit_bytes=VMEM_LIMIT_BYTES)


def _nt_dot(a, b):
    return lax.dot_general(a, b, (((1,), (1,)), ((), ())), preferred_element_type=F32)


def _rms(x, g):
    ms = jnp.mean(x * x, axis=-1, keepdims=True)
    return x * lax.rsqrt(ms + EPS) * g


def _inproj_kernel(x_ref, g_ref, w_ref, *out_refs):
    h = _rms(x_ref[...], g_ref[...]).astype(BF16)
    off = 0
    for o_ref, width in zip(out_refs, SPLIT_SIZES):
        o_ref[...] = jnp.dot(h, w_ref[:, off:off + width],
                             preferred_element_type=F32).astype(o_ref.dtype)
        off += width


def _inproj(x2, g, w_in):
    t = x2.shape[0]
    tm = min(INPROJ_TOKENS, t)
    out_shape = tuple(jax.ShapeDtypeStruct((t, w), BF16) for w in SPLIT_SIZES)
    return pl.pallas_call(
        _inproj_kernel,
        grid=(t // tm,),
        in_specs=[pl.BlockSpec((tm, D_MODEL), lambda i: (i, 0)),
                  pl.BlockSpec((1, D_MODEL), lambda i: (0, 0)),
                  pl.BlockSpec((D_MODEL, IN_WIDTH), lambda i: (0, 0))],
        out_specs=tuple(pl.BlockSpec((tm, w), lambda i: (i, 0)) for w in SPLIT_SIZES),
        out_shape=out_shape,
        compiler_params=_params(("parallel",)),
        name="inproj",
    )(x2, g, w_in)


def _alibi_slopes():
    return [float(2.0 ** (-8.0 * h / ATTN_HEADS)) for h in range(1, ATTN_HEADS + 1)]


def _attn_kernel(sink_ref, q_ref, kp_ref, kc_ref, vp_ref, vc_ref, o_ref):
    n = pl.program_id(1)
    w = ATTN_BLOCK
    q = q_ref[0]
    k = jnp.concatenate([kp_ref[0], kc_ref[0]], axis=0)
    v = jnp.concatenate([vp_ref[0], vc_ref[0]], axis=0)
    qpos = lax.broadcasted_iota(jnp.int32, (w, 2 * w), 0) + w
    kpos = lax.broadcasted_iota(jnp.int32, (w, 2 * w), 1)
    dist = qpos - kpos
    first_ok = jnp.where(n > 0, 0, w)
    in_window = jnp.where(dist >= 0, jnp.where(dist < WINDOW, 1, 0), 0)
    valid = jnp.where(kpos >= first_ok, in_window, 0) == 1
    distf = dist.astype(F32)
    scale = ATTN_HEAD_DIM ** -0.5
    slopes = _alibi_slopes()
    outs = []
    for kh in range(ATTN_KV_HEADS):
        ksl = k[:, kh * ATTN_HEAD_DIM:(kh + 1) * ATTN_HEAD_DIM]
        vsl = v[:, kh * ATTN_HEAD_DIM:(kh + 1) * ATTN_HEAD_DIM]
        qg = jnp.concatenate(
            [q[:, (kh * ATTN_GROUP + g) * ATTN_HEAD_DIM:(kh * ATTN_GROUP + g + 1) * ATTN_HEAD_DIM]
             for g in range(ATTN_GROUP)], axis=0)
        s_all = _nt_dot(qg, ksl) * scale
        ps, denoms = [], []
        for g in range(ATTN_GROUP):
            head = kh * ATTN_GROUP + g
            s = s_all[g * w:(g + 1) * w] - slopes[head] * distf
            s = jnp.where(valid, s, MASK_VALUE)
            sink = sink_ref[head]
            m = jnp.maximum(jnp.max(s, axis=-1, keepdims=True), sink)
            p = jnp.exp(s - m)
            denoms.append(jnp.sum(p, axis=-1, keepdims=True) + jnp.exp(sink - m))
            ps.append(p.astype(BF16))
        o_all = jnp.dot(jnp.concatenate(ps, axis=0), vsl, preferred_element_type=F32)
        for g in range(ATTN_GROUP):
            outs.append(o_all[g * w:(g + 1) * w] / denoms[g])
    o_ref[0] = jnp.concatenate(outs, axis=-1).astype(o_ref.dtype)


def _attention(aq, ak, av, sinks, b, s):
    nb = s // ATTN_BLOCK
    q3 = aq.reshape(b, s, ATTN_WIDTH)
    k3 = ak.reshape(b, s, KV_WIDTH)
    v3 = av.reshape(b, s, KV_WIDTH)
    cur = lambda i, j: (i, j, 0)
    prev = lambda i, j: (i, jnp.maximum(j - 1, 0), 0)
    out = pl.pallas_call(
        _attn_kernel,
        grid=(b, nb),
        in_specs=[pl.BlockSpec(memory_space=pltpu.SMEM),
                  pl.BlockSpec((1, ATTN_BLOCK, ATTN_WIDTH), cur),
                  pl.BlockSpec((1, ATTN_BLOCK, KV_WIDTH), prev),
                  pl.BlockSpec((1, ATTN_BLOCK, KV_WIDTH), cur),
                  pl.BlockSpec((1, ATTN_BLOCK, KV_WIDTH), prev),
                  pl.BlockSpec((1, ATTN_BLOCK, KV_WIDTH), cur)],
        out_specs=pl.BlockSpec((1, ATTN_BLOCK, ATTN_WIDTH), cur),
        out_shape=jax.ShapeDtypeStruct((b, s, ATTN_WIDTH), BF16),
        compiler_params=_params(("parallel", "parallel")),
        name="attn",
    )(sinks, q3, k3, k3, v3, v3)
    return out.reshape(b * s, ATTN_WIDTH)


def _split3(x):
    hi = x.astype(BF16)
    r1 = x - hi.astype(F32)
    mid = r1.astype(BF16)
    lo = (r1 - mid.astype(F32)).astype(BF16)
    return hi, mid, lo


def _hgrn_kernel(lb_ref, gn_ref, hq_ref, hf_ref, hi_ref, hg_ref, o_ref, st_ref, o_scr):
    sb = hq_ref.shape[1]
    c = HGRN_CHUNK

    @pl.when(pl.program_id(1) == 0)
    def _():
        st_ref[...] = jnp.zeros_like(st_ref)

    logits = lb_ref[...]
    e = jnp.exp(logits - jnp.max(logits, axis=0, keepdims=True))
    lb = e[0:1] / jnp.sum(e, axis=0, keepdims=True)

    f = lb + (1.0 - lb) * jax.nn.sigmoid(hf_ref[0].astype(F32))
    kk = 1.0 - f
    logf = jnp.log(f)
    r = lax.broadcasted_iota(jnp.int32, (sb, sb), 0)
    cc = lax.broadcasted_iota(jnp.int32, (sb, sb), 1)
    same = (r // c) == (cc // c)
    tri = jnp.where(same, jnp.where(cc <= r, 1.0, 0.0), 0.0).astype(BF16)
    blk = jnp.where(same, 1.0, 0.0).astype(BF16)
    parts = _split3(logf)
    bcum = sum(jnp.dot(tri, p, preferred_element_type=F32) for p in parts)
    blast = sum(jnp.dot(blk, p, preferred_element_type=F32) for p in parts)
    hq = hq_ref[0].astype(F32)
    q_dec = (hq * jax.nn.sigmoid(hq) * jnp.exp(bcum)).astype(BF16)
    k_inv = (kk * jnp.exp(-bcum)).astype(BF16)
    k_end = (kk * jnp.exp(blast - bcum)).astype(BF16)
    decay = jnp.exp(blast)
    vv = hi_ref[0]
    v_t = jnp.transpose(vv.astype(F32)).astype(BF16)

    tr = lax.broadcasted_iota(jnp.int32, (c, c), 0)
    tc = lax.broadcasted_iota(jnp.int32, (c, c), 1)
    causal = tc <= tr
    for ci in range(sb // c):
        rows = slice(ci * c, (ci + 1) * c)
        outs = []
        for h in range(HGRN_HEADS):
            lanes = slice(h * HGRN_DK, (h + 1) * HGRN_DK)
            qd = q_dec[rows, lanes]
            a = jnp.where(causal, _nt_dot(qd, k_inv[rows, lanes]), 0.0)
            st = st_ref[h]
            o = jnp.dot(a.astype(BF16), vv[rows, lanes], preferred_element_type=F32)
            o = o + _nt_dot(qd, st.astype(BF16))
            ds = jnp.dot(v_t[lanes, rows], k_end[rows, lanes], preferred_element_type=F32)
            st_ref[h] = st * decay[ci * c:ci * c + 1, lanes] + ds
            outs.append(o)
        o_scr[rows, :] = jnp.concatenate(outs, axis=-1)

    hg = hg_ref[0].astype(F32)
    gate = hg * jax.nn.sigmoid(hg)
    ys = []
    for h in range(HGRN_HEADS):
        lanes = slice(h * HGRN_DV, (h + 1) * HGRN_DV)
        ys.append(_rms(o_scr[:, lanes], gn_ref[...]) * gate[:, lanes])
    o_ref[0] = jnp.concatenate(ys, axis=-1).astype(o_ref.dtype)


def _hgrn(hq, hf, hi, hg, lb_logits, gn, b, s):
    sb = min(HGRN_TOKENS, s)
    spec = pl.BlockSpec((1, sb, HGRN_WIDTH), lambda i, j: (i, j, 0))
    r3 = lambda a: a.reshape(b, s, HGRN_WIDTH)
    out = pl.pallas_call(
        _hgrn_kernel,
        grid=(b, s // sb),
        in_specs=[pl.BlockSpec(lb_logits.shape, lambda i, j: (0, 0)),
                  pl.BlockSpec((1, HGRN_DV), lambda i, j: (0, 0)),
                  spec, spec, spec, spec],
        out_specs=spec,
        out_shape=jax.ShapeDtypeStruct((b, s, HGRN_WIDTH), BF16),
        scratch_shapes=[pltpu.VMEM((HGRN_HEADS, HGRN_DV, HGRN_DK), F32),
                        pltpu.VMEM((sb, HGRN_WIDTH), F32)],
        compiler_params=_params(("parallel", "arbitrary")),
        name="hgrn",
    )(lb_logits, gn, r3(hq), r3(hf), r3(hi), r3(hg))
    return out.reshape(b * s, HGRN_WIDTH)


def _merge_kernel(x_ref, ya_ref, yh_ref, ga_ref, gh_ref, wa_ref, wh_ref, wo_ref, g_ref,
                  x1_ref, hnt_ref):
    pa = jnp.dot(ya_ref[...], wa_ref[...], preferred_element_type=F32)
    ph = jnp.dot(yh_ref[...], wh_ref[...], preferred_element_type=F32)
    merged = (jax.nn.sigmoid(ga_ref[...].astype(F32)) * pa
              + jax.nn.sigmoid(gh_ref[...].astype(F32)) * ph)
    x1 = x_ref[...] + jnp.dot(merged.astype(BF16), wo_ref[...], preferred_element_type=F32)
    x1_ref[...] = x1
    hnt_ref[...] = jnp.transpose(_rms(x1, g_ref[...])).astype(BF16)


def _merge(x2, ya, yh, ga, gh, wa, wh, wo, g):
    t = x2.shape[0]
    tm = min(MERGE_TOKENS, t)
    row = lambda w: pl.BlockSpec((tm, w), lambda i: (i, 0))
    full = lambda a: pl.BlockSpec(a.shape, lambda i: (0, 0))
    return pl.pallas_call(
        _merge_kernel,
        grid=(t // tm,),
        in_specs=[row(D_MODEL), row(ATTN_WIDTH), row(HGRN_WIDTH), row(D_MODEL), row(D_MODEL),
                  full(wa), full(wh), full(wo), full(g)],
        out_specs=(row(D_MODEL), pl.BlockSpec((D_MODEL, tm), lambda i: (0, i))),
        out_shape=(jax.ShapeDtypeStruct((t, D_MODEL), F32),
                   jax.ShapeDtypeStruct((D_MODEL, t), BF16)),
        compiler_params=_params(("parallel",)),
        name="merge",
    )(x2, ya, yh, ga, gh, wa, wh, wo, g)


def _as_f32(i):
    return jnp.asarray(i, jnp.int32).astype(F32)


def _topk_exact(s, k):
    n_rows, lanes = s.shape
    row = lax.broadcasted_iota(jnp.int32, (n_rows, lanes), 0).astype(F32)
    kid = lax.broadcasted_iota(jnp.int32, (k, lanes), 0).astype(F32)

    def body(a, carry):
        s, rank, vals = carry
        af = _as_f32(a)
        m = jnp.max(s, axis=0, keepdims=True)
        first = jnp.min(jnp.where(s == m, row, float(n_rows)), axis=0, keepdims=True)
        sel = row == first
        return jnp.where(sel, -jnp.inf, s), jnp.where(sel, af, rank), jnp.where(kid == af, m, vals)

    init = (s, jnp.full((n_rows, lanes), float(k), F32), jnp.zeros((k, lanes), F32))
    _, rank, vals = lax.fori_loop(0, k, body, init)
    return rank, vals


RANK_CODE_SCALE = 2.0 ** 123
RANK_CODE_LIMIT = -float(PEER_TOPK) * RANK_CODE_SCALE


def _topk_quick_pair(s1, s2, k):
    c1, c2, v1, v2 = s1, s2, [], []
    for a in range(k):
        code = -float(a + k) * RANK_CODE_SCALE
        m1 = jnp.max(c1, axis=0, keepdims=True)
        m2 = jnp.max(c2, axis=0, keepdims=True)
        c1 = jnp.where(c1 == m1, code, c1)
        c2 = jnp.where(c2 == m2, code, c2)
        v1.append(m1)
        v2.append(m2)
    v1 = jnp.concatenate(v1, axis=0)
    v2 = jnp.concatenate(v2, axis=0)

    def decode(c, s):
        marked = c <= RANK_CODE_LIMIT
        rank = jnp.where(marked, c * (-1.0 / RANK_CODE_SCALE) - float(k), float(k))
        n_marked = jnp.sum(jnp.where(marked, 1.0, 0.0), axis=0, keepdims=True)
        in_range = jnp.min(s, axis=0, keepdims=True) > RANK_CODE_LIMIT
        return rank, jnp.where(n_marked == float(k), jnp.where(in_range, 1.0, 0.0), 0.0)

    rank1, ok1 = decode(c1, s1)
    rank2, ok2 = decode(c2, s2)
    return rank1, v1, rank2, v2, ok1 * ok2


def _candidates(v1, v2, k):
    brow = lax.broadcasted_iota(jnp.int32, (k, v1.shape[1]), 0)
    full = [jnp.where(brow < k // (a + 1), v1[a:a + 1] + v2, -jnp.inf) for a in range(CAND_FULL_ROWS)]
    tail = v1[CAND_FULL_ROWS:] + v2[0:1]
    return jnp.concatenate(full + [tail], axis=0)


def _choose_exact(cand, k):
    n_cand, lanes = cand.shape
    crow = lax.broadcasted_iota(jnp.int32, (n_cand, lanes), 0).astype(F32)

    def body(_, c):
        m = jnp.max(c, axis=0, keepdims=True)
        first = jnp.min(jnp.where(c == m, crow, float(n_cand)), axis=0, keepdims=True)
        return jnp.where(crow == first, -jnp.inf, c)

    return lax.fori_loop(0, k, body, cand)


def _choose_quick(cand, k):
    for _ in range(k):
        cand = jnp.where(cand == jnp.max(cand, axis=0, keepdims=True), -jnp.inf, cand)
    return cand


def _pack_rows(x):
    return pltpu.bitcast(x, jnp.uint32)


def _unpack_rows(x):
    return pltpu.bitcast(x, BF16)


def _route_outputs(s1, s2, rank1, v1, rank2, v2, cand, left, k):
    chosen = jnp.where(left == -jnp.inf, jnp.where(cand > -jnp.inf, 1.0, 0.0), 0.0)
    top = v1[0:1] + v2[0:1]
    z = jnp.sum(chosen * jnp.exp(cand - top), axis=0, keepdims=True)
    counts = [jnp.sum(chosen[a * k:(a + 1) * k], axis=0, keepdims=True) for a in range(CAND_FULL_ROWS)]
    counts += [chosen[CAND_FULL_ROWS * k + a:CAND_FULL_ROWS * k + a + 1] for a in range(k - CAND_FULL_ROWS)]
    n1 = jnp.zeros_like(rank1)
    for a in range(k):
        n1 = jnp.where(rank1 == float(a), counts[a], n1)
    e1 = jnp.exp(s1 - v1[0:1])
    r2 = _pack_rows(rank2.astype(BF16))
    e2 = _pack_rows((jnp.exp(s2 - v2[0:1]) / z).astype(BF16))
    return (n1, e1, r2, e2), jnp.sum(chosen, axis=0, keepdims=True)


def _route_kernel(hnt_ref, wq_ref, k1_ref, k2_ref, n1_ref, e1_ref, r2_ref, e2_ref):
    k = PEER_TOPK
    out_refs = (n1_ref, e1_ref, r2_ref, e2_ref)
    q_t = jnp.dot(wq_ref[...], hnt_ref[...], preferred_element_type=F32).astype(BF16)
    s1 = jnp.dot(k1_ref[0], q_t[:PEER_KEY_DIM], preferred_element_type=F32)
    s2 = jnp.dot(k2_ref[0], q_t[PEER_KEY_DIM:], preferred_element_type=F32)

    def store(outs):
        for ref, val in zip(out_refs, outs):
            ref[0] = val

    tiles = [_topk_quick_pair(s1[:, lt * LANE_TILE:(lt + 1) * LANE_TILE],
                              s2[:, lt * LANE_TILE:(lt + 1) * LANE_TILE], k)
             for lt in range(s1.shape[1] // LANE_TILE)]
    rank1, v1, rank2, v2, ok = (jnp.concatenate(parts, axis=1) for parts in zip(*tiles))
    cand = _candidates(v1, v2, k)
    outs, n_chosen = _route_outputs(s1, s2, rank1, v1, rank2, v2, cand, _choose_quick(cand, k), k)
    store(outs)
    ok = ok * jnp.where(n_chosen == float(k), 1.0, 0.0)

    @pl.when(jnp.min(ok) < 1.0)
    def _():
        rank1, v1 = _topk_exact(s1, k)
        rank2, v2 = _topk_exact(s2, k)
        cand = _candidates(v1, v2, k)
        outs, _ = _route_outputs(s1, s2, rank1, v1, rank2, v2, cand, _choose_exact(cand, k), k)
        store(outs)


def _route(hnt, wq_t, keys1, keys2):
    t = hnt.shape[1]
    tb = min(ROUTE_TOKENS, t)
    kd2 = 2 * PEER_KEY_DIM
    out = jax.ShapeDtypeStruct((PEER_HEADS, PEER_N_KEYS, t), F32)
    out16 = jax.ShapeDtypeStruct((PEER_HEADS, PEER_N_KEYS // 2, t), jnp.uint32)
    pspec = pl.BlockSpec((1, PEER_N_KEYS // 2, tb), lambda i, h: (h, 0, i))
    ospec = pl.BlockSpec((1, PEER_N_KEYS, tb), lambda i, h: (h, 0, i))
    kspec = pl.BlockSpec((1, PEER_N_KEYS, PEER_KEY_DIM), lambda i, h: (h, 0, 0))
    return pl.pallas_call(
        _route_kernel,
        grid=(t // tb, PEER_HEADS),
        in_specs=[pl.BlockSpec((D_MODEL, tb), lambda i, h: (0, i)),
                  pl.BlockSpec((kd2, D_MODEL), lambda i, h: (h, 0)),
                  kspec, kspec],
        out_specs=(ospec, ospec, pspec, pspec),
        out_shape=(out, out, out16, out16),
        compiler_params=_params(("parallel", "arbitrary")),
        name="route",
    )(hnt, wq_t, keys1, keys2)


def _peer_gates(n1_ref, e1_ref, r2_ref, e2_ref, firsts, act, d_ref):
    nk = PEER_N_KEYS
    for lt in range(d_ref.shape[1] // LANE_TILE):
        lanes = slice(lt * LANE_TILE, (lt + 1) * LANE_TILE)
        ws = [None] * len(firsts)
        for h in range(PEER_HEADS):
            keys = slice(h * nk // 2, (h + 1) * nk // 2)
            r2 = _unpack_rows(r2_ref[keys, lanes])
            e2 = _unpack_rows(e2_ref[keys, lanes])
            for k, ii in enumerate(firsts):
                n1 = n1_ref[h, ii:ii + 1, lanes].astype(BF16)
                e1 = e1_ref[h, ii:ii + 1, lanes].astype(BF16)
                term = jnp.where(r2 < n1, e2, 0.0) * e1
                ws[k] = term if ws[k] is None else ws[k] + term
        for k, ii in enumerate(firsts):
            a = act[k * nk:(k + 1) * nk, lanes]
            gelu = (0.5 * a) * (1.0 + lax.erf(a * (2.0 ** -0.5)))
            d_ref[ii * nk // 2:(ii + 1) * nk // 2, lanes] = _pack_rows(ws[k] * gelu.astype(BF16))


def _peer_kernel(hnt_ref, u_ref, vt_ref, n1_ref, e1_ref, r2_ref, e2_ref, x1_ref, g_ref,
                 o_ref, acc_ref, d_ref):
    c = pl.program_id(1)
    nk = PEER_N_KEYS
    ec = u_ref.shape[0]
    piece = ec // PEER_MXU_SPLIT

    @pl.when(c == 0)
    def _():
        acc_ref[...] = jnp.zeros_like(acc_ref)

    for p in range(PEER_MXU_SPLIT):
        rows = slice(p * piece, (p + 1) * piece)
        act = jnp.dot(u_ref[rows, :], hnt_ref[...], preferred_element_type=F32)
        firsts = range(p * piece // nk, (p + 1) * piece // nk)
        _peer_gates(n1_ref, e1_ref, r2_ref, e2_ref, firsts, act, d_ref)

    acc_ref[...] += jnp.dot(vt_ref[...], _unpack_rows(d_ref[...]), preferred_element_type=F32)

    @pl.when(c == pl.num_programs(1) - 1)
    def _():
        y = x1_ref[...] + jnp.transpose(acc_ref[...])
        o_ref[...] = _rms(y, g_ref[...])


def _peer(hnt, u, v_t, n1, e1, r2, e2, x1, g):
    t = x1.shape[0]
    tb = min(PEER_TOKENS, t)
    ec = PEER_EXPERT_CHUNK
    packed_rows = PEER_HEADS * PEER_N_KEYS // 2
    dense = pl.BlockSpec((packed_rows, tb), lambda i, c: (0, i))
    r2 = r2.reshape(packed_rows, t)
    e2 = e2.reshape(packed_rows, t)
    rows = pl.BlockSpec((PEER_HEADS, ec // PEER_N_KEYS, tb), lambda i, c: (0, c, i))
    return pl.pallas_call(
        _peer_kernel,
        grid=(t // tb, PEER_N_EXPERTS // ec),
        in_specs=[pl.BlockSpec((D_MODEL, tb), lambda i, c: (0, i)),
                  pl.BlockSpec((ec, D_MODEL), lambda i, c: (c, 0)),
                  pl.BlockSpec((D_MODEL, ec), lambda i, c: (0, c)),
                  rows, rows, dense, dense,
                  pl.BlockSpec((tb, D_MODEL), lambda i, c: (i, 0)),
                  pl.BlockSpec((1, D_MODEL), lambda i, c: (0, 0))],
        out_specs=pl.BlockSpec((tb, D_MODEL), lambda i, c: (i, 0)),
        out_shape=jax.ShapeDtypeStruct((t, D_MODEL), F32),
        scratch_shapes=[pltpu.VMEM((D_MODEL, tb), F32), pltpu.VMEM((ec // 2, tb), jnp.uint32)],
        compiler_params=_params(("parallel", "arbitrary")),
        name="peer",
    )(hnt, u, v_t, n1, e1, r2, e2, x1, g)


def kernel(x, norm_mix_g, w_in, attn_sinks, hgrn_lb_logits, hgrn_norm_g, w_attn_proj, w_hgrn_proj,
           w_out, norm_ffn_g, w_peer_q, peer_keys, peer_u, peer_v, norm_final_g):
    b, s, d = x.shape
    assert d == D_MODEL and norm_mix_g.shape[0] == 1
    t = b * s
    x2 = x.reshape(t, d)
    row = lambda a: a.reshape(1, -1).astype(F32)

    aq, ak, av, hq, hf, hi, hg, ga, gh = _inproj(x2, row(norm_mix_g[0]), w_in[0].astype(BF16))
    y_attn = _attention(aq, ak, av, attn_sinks[0].astype(F32), b, s)
    y_hgrn = _hgrn(hq, hf, hi, hg, hgrn_lb_logits.astype(F32), row(hgrn_norm_g[0]), b, s)
    x1, hnt = _merge(x2, y_attn, y_hgrn, ga, gh, w_attn_proj[0].astype(BF16),
                     w_hgrn_proj[0].astype(BF16), w_out[0].astype(BF16), row(norm_ffn_g[0]))
    wq_t = jnp.transpose(w_peer_q[0]).astype(BF16)
    keys = peer_keys[0].astype(BF16)
    n1, e1, r2, e2 = _route(hnt, wq_t, keys[0], keys[1])
    out = _peer(hnt, peer_u[0].astype(BF16), jnp.transpose(peer_v[0]).astype(BF16),
                n1, e1, r2, e2, x1, row(norm_final_g))
    return out.reshape(b, s, d)
```

```python
import functools

import jax
import jax.numpy as jnp
import numpy as np
from jax import lax
from jax.experimental import pallas as pl
from jax.experimental.pallas import tpu as pltpu

F32 = jnp.float32
BF16 = jnp.bfloat16

D_MODEL = 1024
ATTN_HEADS = 8
ATTN_KV_HEADS = 2
ATTN_GROUP = ATTN_HEADS // ATTN_KV_HEADS
ATTN_HEAD_DIM = 64
ATTN_WIDTH = ATTN_HEADS * ATTN_HEAD_DIM
KV_WIDTH = ATTN_KV_HEADS * ATTN_HEAD_DIM
WINDOW = 128
ATTN_BLOCK = 128
HGRN_HEADS = 4
HGRN_DK = 128
HGRN_DV = 128
HGRN_WIDTH = HGRN_HEADS * HGRN_DK
HGRN_CHUNK = 32
PEER_HEADS = 8
PEER_N_KEYS = 128
PEER_N_EXPERTS = PEER_N_KEYS * PEER_N_KEYS
PEER_KEY_DIM = 128
PEER_TOPK = 16
EPS = 1e-6
MASK_VALUE = -1e30

SPLIT_SIZES = (ATTN_WIDTH, KV_WIDTH, KV_WIDTH, HGRN_WIDTH, HGRN_WIDTH,
               HGRN_WIDTH, HGRN_WIDTH, D_MODEL, D_MODEL)
IN_WIDTH = sum(SPLIT_SIZES)

VMEM_LIMIT_BYTES = 56 * 1024 * 1024

INPROJ_TOKENS = 512
HGRN_TOKENS = 256
HGRN_BATCH_ROWS = 2
MERGE_TOKENS = 512
ROUTE_TOKENS = 512
PEER_TOKENS = 512
PEER_EXPERT_CHUNK = 2048
PEER_SUB_CHUNK = 1024
LANE_TILE = 128
PEER_MXU_SPLIT = 4

CAND_FULL_ROWS = 8


def _params(semantics):
    return pltpu.CompilerParams(dimension_semantics=semantics,
                                vmem_limit_bytes=VMEM_LIMIT_BYTES)


def _nt_dot(a, b):
    return lax.dot_general(a, b, (((1,), (1,)), ((), ())), preferred_element_type=F32)


def _rms(x, g):
    ms = jnp.mean(x * x, axis=-1, keepdims=True)
    return x * lax.rsqrt(ms + EPS) * g


def _inproj_kernel(x_ref, g_ref, w_ref, *out_refs):
    h = _rms(x_ref[...], g_ref[...]).astype(BF16)
    off = 0
    for o_ref, width in zip(out_refs, SPLIT_SIZES):
        o_ref[...] = jnp.dot(h, w_ref[:, off:off + width],
                             preferred_element_type=F32).astype(o_ref.dtype)
        off += width


def _inproj(x2, g, w_in):
    t = x2.shape[0]
    tm = min(INPROJ_TOKENS, t)
    out_shape = tuple(jax.ShapeDtypeStruct((t, w), BF16) for w in SPLIT_SIZES)
    return pl.pallas_call(
        _inproj_kernel,
        grid=(t // tm,),
        in_specs=[pl.BlockSpec((tm, D_MODEL), lambda i: (i, 0)),
                  pl.BlockSpec((1, D_MODEL), lambda i: (0, 0)),
                  pl.BlockSpec((D_MODEL, IN_WIDTH), lambda i: (0, 0))],
        out_specs=tuple(pl.BlockSpec((tm, w), lambda i: (i, 0)) for w in SPLIT_SIZES),
        out_shape=out_shape,
        compiler_params=_params(("parallel",)),
        name="inproj",
    )(x2, g, w_in)


def _alibi_slopes():
    return [float(2.0 ** (-8.0 * h / ATTN_HEADS)) for h in range(1, ATTN_HEADS + 1)]


def _attn_kernel(sink_ref, q_ref, kp_ref, kc_ref, vp_ref, vc_ref, o_ref):
    n = pl.program_id(1)
    w = ATTN_BLOCK
    q = q_ref[0]
    k = jnp.concatenate([kp_ref[0], kc_ref[0]], axis=0)
    v = jnp.concatenate([vp_ref[0], vc_ref[0]], axis=0)
    qpos = lax.broadcasted_iota(jnp.int32, (w, 2 * w), 0) + w
    kpos = lax.broadcasted_iota(jnp.int32, (w, 2 * w), 1)
    dist = qpos - kpos
    first_ok = jnp.where(n > 0, 0, w)
    in_window = jnp.where(dist >= 0, jnp.where(dist < WINDOW, 1, 0), 0)
    valid = jnp.where(kpos >= first_ok, in_window, 0) == 1
    distf = dist.astype(F32)
    scale = ATTN_HEAD_DIM ** -0.5
    slopes = _alibi_slopes()
    outs = []
    for kh in range(ATTN_KV_HEADS):
        ksl = k[:, kh * ATTN_HEAD_DIM:(kh + 1) * ATTN_HEAD_DIM]
        vsl = v[:, kh * ATTN_HEAD_DIM:(kh + 1) * ATTN_HEAD_DIM]
        qg = jnp.concatenate(
            [q[:, (kh * ATTN_GROUP + g) * ATTN_HEAD_DIM:(kh * ATTN_GROUP + g + 1) * ATTN_HEAD_DIM]
             for g in range(ATTN_GROUP)], axis=0)
        s_all = _nt_dot(qg, ksl) * scale
        ps, denoms = [], []
        for g in range(ATTN_GROUP):
            head = kh * ATTN_GROUP + g
            s = s_all[g * w:(g + 1) * w] - slopes[head] * distf
            s = jnp.where(valid, s, MASK_VALUE)
            sink = sink_ref[head]
            m = jnp.maximum(jnp.max(s, axis=-1, keepdims=True), sink)
            p = jnp.exp(s - m)
            denoms.append(jnp.sum(p, axis=-1, keepdims=True) + jnp.exp(sink - m))
            ps.append(p.astype(BF16))
        o_all = jnp.dot(jnp.concatenate(ps, axis=0), vsl, preferred_element_type=F32)
        for g in range(ATTN_GROUP):
            outs.append(o_all[g * w:(g + 1) * w] / denoms[g])
    o_ref[0] = jnp.concatenate(outs, axis=-1).astype(o_ref.dtype)


def _attention(aq, ak, av, sinks, b, s):
    nb = s // ATTN_BLOCK
    q3 = aq.reshape(b, s, ATTN_WIDTH)
    k3 = ak.reshape(b, s, KV_WIDTH)
    v3 = av.reshape(b, s, KV_WIDTH)
    cur = lambda i, j: (i, j, 0)
    prev = lambda i, j: (i, jnp.maximum(j - 1, 0), 0)
    out = pl.pallas_call(
        _attn_kernel,
        grid=(b, nb),
        in_specs=[pl.BlockSpec(memory_space=pltpu.SMEM),
                  pl.BlockSpec((1, ATTN_BLOCK, ATTN_WIDTH), cur),
                  pl.BlockSpec((1, ATTN_BLOCK, KV_WIDTH), prev),
                  pl.BlockSpec((1, ATTN_BLOCK, KV_WIDTH), cur),
                  pl.BlockSpec((1, ATTN_BLOCK, KV_WIDTH), prev),
                  pl.BlockSpec((1, ATTN_BLOCK, KV_WIDTH), cur)],
        out_specs=pl.BlockSpec((1, ATTN_BLOCK, ATTN_WIDTH), cur),
        out_shape=jax.ShapeDtypeStruct((b, s, ATTN_WIDTH), BF16),
        compiler_params=_params(("parallel", "parallel")),
        name="attn",
    )(sinks, q3, k3, k3, v3, v3)
    return out.reshape(b * s, ATTN_WIDTH)


def _split3(x):
    hi = x.astype(BF16)
    r1 = x - hi.astype(F32)
    mid = r1.astype(BF16)
    lo = (r1 - mid.astype(F32)).astype(BF16)
    return hi, mid, lo


def _hgrn_kernel(lb_ref, gn_ref, hq_ref, hf_ref, hi_ref, hg_ref, o_ref, st_ref, o_scr):
    n_batch, sb = hq_ref.shape[0], hq_ref.shape[1]
    c = HGRN_CHUNK

    @pl.when(pl.program_id(1) == 0)
    def _():
        st_ref[...] = jnp.zeros_like(st_ref)

    logits = lb_ref[...]
    e = jnp.exp(logits - jnp.max(logits, axis=0, keepdims=True))
    lb = e[0:1] / jnp.sum(e, axis=0, keepdims=True)

    r = lax.broadcasted_iota(jnp.int32, (sb, sb), 0)
    cc = lax.broadcasted_iota(jnp.int32, (sb, sb), 1)
    same = (r // c) == (cc // c)
    tri = jnp.where(same, jnp.where(cc <= r, 1.0, 0.0), 0.0).astype(BF16)
    blk = jnp.where(same, 1.0, 0.0).astype(BF16)
    tr = lax.broadcasted_iota(jnp.int32, (c, c), 0)
    tc = lax.broadcasted_iota(jnp.int32, (c, c), 1)
    causal = tc <= tr

    def prepare(bb):
        f = lb + (1.0 - lb) * jax.nn.sigmoid(hf_ref[bb].astype(F32))
        kk = 1.0 - f
        parts = _split3(jnp.log(f))
        bcum = sum(jnp.dot(tri, p, preferred_element_type=F32) for p in parts)
        blast = sum(jnp.dot(blk, p, preferred_element_type=F32) for p in parts)
        hq = hq_ref[bb].astype(F32)
        q_dec = (hq * jax.nn.sigmoid(hq) * jnp.exp(bcum)).astype(BF16)
        k_inv = (kk * jnp.exp(-bcum)).astype(BF16)
        k_end = (kk * jnp.exp(blast - bcum)).astype(BF16)
        vv = hi_ref[bb]
        v_t = jnp.transpose(vv.astype(F32)).astype(BF16)
        return q_dec, k_inv, k_end, jnp.exp(blast), vv, v_t

    rows_of_batch = [prepare(bb) for bb in range(n_batch)]
    for ci in range(sb // c):
        rows = slice(ci * c, (ci + 1) * c)
        for bb, (q_dec, k_inv, k_end, decay, vv, v_t) in enumerate(rows_of_batch):
            outs = []
            for h in range(HGRN_HEADS):
                lanes = slice(h * HGRN_DK, (h + 1) * HGRN_DK)
                qd = q_dec[rows, lanes]
                a = jnp.where(causal, _nt_dot(qd, k_inv[rows, lanes]), 0.0)
                st = st_ref[bb, h]
                o = jnp.dot(a.astype(BF16), vv[rows, lanes], preferred_element_type=F32)
                o = o + _nt_dot(qd, st.astype(BF16))
                ds = jnp.dot(v_t[lanes, rows], k_end[rows, lanes], preferred_element_type=F32)
                st_ref[bb, h] = st * decay[ci * c:ci * c + 1, lanes] + ds
                outs.append(o)
            o_scr[bb, rows, :] = jnp.concatenate(outs, axis=-1)

    for bb in range(n_batch):
        hg = hg_ref[bb].astype(F32)
        gate = hg * jax.nn.sigmoid(hg)
        ys = []
        for h in range(HGRN_HEADS):
            lanes = slice(h * HGRN_DV, (h + 1) * HGRN_DV)
            ys.append(_rms(o_scr[bb, :, lanes], gn_ref[...]) * gate[:, lanes])
        o_ref[bb] = jnp.concatenate(ys, axis=-1).astype(o_ref.dtype)


def _hgrn(hq, hf, hi, hg, lb_logits, gn, b, s):
    sb = min(HGRN_TOKENS, s)
    nb = HGRN_BATCH_ROWS if b % HGRN_BATCH_ROWS == 0 else 1
    spec = pl.BlockSpec((nb, sb, HGRN_WIDTH), lambda i, j: (i, j, 0))
    r3 = lambda a: a.reshape(b, s, HGRN_WIDTH)
    out = pl.pallas_call(
        _hgrn_kernel,
        grid=(b // nb, s // sb),
        in_specs=[pl.BlockSpec(lb_logits.shape, lambda i, j: (0, 0)),
                  pl.BlockSpec((1, HGRN_DV), lambda i, j: (0, 0)),
                  spec, spec, spec, spec],
        out_specs=spec,
        out_shape=jax.ShapeDtypeStruct((b, s, HGRN_WIDTH), BF16),
        scratch_shapes=[pltpu.VMEM((nb, HGRN_HEADS, HGRN_DV, HGRN_DK), F32),
                        pltpu.VMEM((nb, sb, HGRN_WIDTH), F32)],
        compiler_params=_params(("parallel", "arbitrary")),
        name="hgrn",
    )(lb_logits, gn, r3(hq), r3(hf), r3(hi), r3(hg))
    return out.reshape(b * s, HGRN_WIDTH)


def _merge_kernel(x_ref, ya_ref, yh_ref, ga_ref, gh_ref, wa_ref, wh_ref, wo_ref, g_ref,
                  x1_ref, hnt_ref):
    pa = jnp.dot(ya_ref[...], wa_ref[...], preferred_element_type=F32)
    ph = jnp.dot(yh_ref[...], wh_ref[...], preferred_element_type=F32)
    merged = (jax.nn.sigmoid(ga_ref[...].astype(F32)) * pa
              + jax.nn.sigmoid(gh_ref[...].astype(F32)) * ph)
    x1 = x_ref[...] + jnp.dot(merged.astype(BF16), wo_ref[...], preferred_element_type=F32)
    x1_ref[...] = x1
    hnt_ref[...] = jnp.transpose(_rms(x1, g_ref[...])).astype(BF16)


def _merge(x2, ya, yh, ga, gh, wa, wh, wo, g):
    t = x2.shape[0]
    tm = min(MERGE_TOKENS, t)
    row = lambda w: pl.BlockSpec((tm, w), lambda i: (i, 0))
    full = lambda a: pl.BlockSpec(a.shape, lambda i: (0, 0))
    return pl.pallas_call(
        _merge_kernel,
        grid=(t // tm,),
        in_specs=[row(D_MODEL), row(ATTN_WIDTH), row(HGRN_WIDTH), row(D_MODEL), row(D_MODEL),
                  full(wa), full(wh), full(wo), full(g)],
        out_specs=(row(D_MODEL), pl.BlockSpec((D_MODEL, tm), lambda i: (0, i))),
        out_shape=(jax.ShapeDtypeStruct((t, D_MODEL), F32),
                   jax.ShapeDtypeStruct((D_MODEL, t), BF16)),
        compiler_params=_params(("parallel",)),
        name="merge",
    )(x2, ya, yh, ga, gh, wa, wh, wo, g)


def _as_f32(i):
    return jnp.asarray(i, jnp.int32).astype(F32)


def _topk_exact(s, k):
    n_rows, lanes = s.shape
    row = lax.broadcasted_iota(jnp.int32, (n_rows, lanes), 0).astype(F32)
    kid = lax.broadcasted_iota(jnp.int32, (k, lanes), 0).astype(F32)

    def body(a, carry):
        s, rank, vals = carry
        af = _as_f32(a)
        m = jnp.max(s, axis=0, keepdims=True)
        first = jnp.min(jnp.where(s == m, row, float(n_rows)), axis=0, keepdims=True)
        sel = row == first
        return jnp.where(sel, -jnp.inf, s), jnp.where(sel, af, rank), jnp.where(kid == af, m, vals)

    init = (s, jnp.full((n_rows, lanes), float(k), F32), jnp.zeros((k, lanes), F32))
    _, rank, vals = lax.fori_loop(0, k, body, init)
    return rank, vals


RANK_CODE_SCALE = 2.0 ** 123
RANK_CODE_LIMIT = -float(PEER_TOPK) * RANK_CODE_SCALE


def _topk_quick_pair(s1, s2, k):
    c1, c2, v1, v2 = s1, s2, [], []
    for a in range(k):
        code = -float(a + k) * RANK_CODE_SCALE
        m1 = jnp.max(c1, axis=0, keepdims=True)
        m2 = jnp.max(c2, axis=0, keepdims=True)
        c1 = jnp.where(c1 == m1, code, c1)
        c2 = jnp.where(c2 == m2, code, c2)
        v1.append(m1)
        v2.append(m2)
    v1 = jnp.concatenate(v1, axis=0)
    v2 = jnp.concatenate(v2, axis=0)

    def decode(c, s):
        marked = c <= RANK_CODE_LIMIT
        rank = jnp.where(marked, c * (-1.0 / RANK_CODE_SCALE) - float(k), float(k))
        n_marked = jnp.sum(jnp.where(marked, 1.0, 0.0), axis=0, keepdims=True)
        in_range = jnp.min(s, axis=0, keepdims=True) > RANK_CODE_LIMIT
        return rank, jnp.where(n_marked == float(k), jnp.where(in_range, 1.0, 0.0), 0.0)

    rank1, ok1 = decode(c1, s1)
    rank2, ok2 = decode(c2, s2)
    return rank1, v1, rank2, v2, ok1 * ok2


def _candidates(v1, v2, k):
    brow = lax.broadcasted_iota(jnp.int32, (k, v1.shape[1]), 0)
    full = [jnp.where(brow < k // (a + 1), v1[a:a + 1] + v2, -jnp.inf) for a in range(CAND_FULL_ROWS)]
    tail = v1[CAND_FULL_ROWS:] + v2[0:1]
    return jnp.concatenate(full + [tail], axis=0)


def _choose_exact(cand, k):
    n_cand, lanes = cand.shape
    crow = lax.broadcasted_iota(jnp.int32, (n_cand, lanes), 0).astype(F32)

    def body(_, c):
        m = jnp.max(c, axis=0, keepdims=True)
        first = jnp.min(jnp.where(c == m, crow, float(n_cand)), axis=0, keepdims=True)
        return jnp.where(crow == first, -jnp.inf, c)

    return lax.fori_loop(0, k, body, cand)


def _choose_quick(cand, k):
    for _ in range(k):
        cand = jnp.where(cand == jnp.max(cand, axis=0, keepdims=True), -jnp.inf, cand)
    return cand


def _pack_rows(x):
    return pltpu.bitcast(x, jnp.uint32)


def _unpack_rows(x):
    return pltpu.bitcast(x, BF16)


def _route_outputs(s1, s2, rank1, v1, rank2, v2, cand, left, k):
    chosen = jnp.where(left == -jnp.inf, jnp.where(cand > -jnp.inf, 1.0, 0.0), 0.0)
    top = v1[0:1] + v2[0:1]
    z = jnp.sum(chosen * jnp.exp(cand - top), axis=0, keepdims=True)
    counts = [jnp.sum(chosen[a * k:(a + 1) * k], axis=0, keepdims=True) for a in range(CAND_FULL_ROWS)]
    counts += [chosen[CAND_FULL_ROWS * k + a:CAND_FULL_ROWS * k + a + 1] for a in range(k - CAND_FULL_ROWS)]
    n1 = jnp.zeros_like(rank1)
    for a in range(k):
        n1 = jnp.where(rank1 == float(a), counts[a], n1)
    e1 = jnp.exp(s1 - v1[0:1])
    r2 = _pack_rows(rank2.astype(BF16))
    e2 = _pack_rows((jnp.exp(s2 - v2[0:1]) / z).astype(BF16))
    return (n1, e1, r2, e2), jnp.sum(chosen, axis=0, keepdims=True)


def _route_kernel(hnt_ref, wq_ref, k1_ref, k2_ref, n1_ref, e1_ref, r2_ref, e2_ref):
    k = PEER_TOPK
    out_refs = (n1_ref, e1_ref, r2_ref, e2_ref)
    q_t = jnp.dot(wq_ref[...], hnt_ref[...], preferred_element_type=F32).astype(BF16)
    s1 = jnp.dot(k1_ref[0], q_t[:PEER_KEY_DIM], preferred_element_type=F32)
    s2 = jnp.dot(k2_ref[0], q_t[PEER_KEY_DIM:], preferred_element_type=F32)

    def store(outs):
        for ref, val in zip(out_refs, outs):
            ref[0] = val

    tiles = [_topk_quick_pair(s1[:, lt * LANE_TILE:(lt + 1) * LANE_TILE],
                              s2[:, lt * LANE_TILE:(lt + 1) * LANE_TILE], k)
             for lt in range(s1.shape[1] // LANE_TILE)]
    rank1, v1, rank2, v2, ok = (jnp.concatenate(parts, axis=1) for parts in zip(*tiles))
    cand = _candidates(v1, v2, k)
    outs, n_chosen = _route_outputs(s1, s2, rank1, v1, rank2, v2, cand, _choose_quick(cand, k), k)
    store(outs)
    ok = ok * jnp.where(n_chosen == float(k), 1.0, 0.0)

    @pl.when(jnp.min(ok) < 1.0)
    def _():
        rank1, v1 = _topk_exact(s1, k)
        rank2, v2 = _topk_exact(s2, k)
        cand = _candidates(v1, v2, k)
        outs, _ = _route_outputs(s1, s2, rank1, v1, rank2, v2, cand, _choose_exact(cand, k), k)
        store(outs)


def _route(hnt, wq_t, keys1, keys2):
    t = hnt.shape[1]
    tb = min(ROUTE_TOKENS, t)
    kd2 = 2 * PEER_KEY_DIM
    out = jax.ShapeDtypeStruct((PEER_HEADS, PEER_N_KEYS, t), F32)
    out16 = jax.ShapeDtypeStruct((PEER_HEADS, PEER_N_KEYS // 2, t), jnp.uint32)
    pspec = pl.BlockSpec((1, PEER_N_KEYS // 2, tb), lambda i, h: (h, 0, i))
    ospec = pl.BlockSpec((1, PEER_N_KEYS, tb), lambda i, h: (h, 0, i))
    kspec = pl.BlockSpec((1, PEER_N_KEYS, PEER_KEY_DIM), lambda i, h: (h, 0, 0))
    return pl.pallas_call(
        _route_kernel,
        grid=(t // tb, PEER_HEADS),
        in_specs=[pl.BlockSpec((D_MODEL, tb), lambda i, h: (0, i)),
                  pl.BlockSpec((kd2, D_MODEL), lambda i, h: (h, 0)),
                  kspec, kspec],
        out_specs=(ospec, ospec, pspec, pspec),
        out_shape=(out, out, out16, out16),
        compiler_params=_params(("parallel", "arbitrary")),
        name="route",
    )(hnt, wq_t, keys1, keys2)


def _peer_gates(n1_ref, e1_ref, r2_ref, e2_ref, firsts, act, d_ref):
    nk = PEER_N_KEYS
    for lt in range(d_ref.shape[1] // LANE_TILE):
        lanes = slice(lt * LANE_TILE, (lt + 1) * LANE_TILE)
        ws = [None] * len(firsts)
        for h in range(PEER_HEADS):
            keys = slice(h * nk // 2, (h + 1) * nk // 2)
            r2 = _unpack_rows(r2_ref[keys, lanes])
            e2 = _unpack_rows(e2_ref[keys, lanes])
            for k, ii in enumerate(firsts):
                n1 = n1_ref[h, ii:ii + 1, lanes].astype(BF16)
                e1 = e1_ref[h, ii:ii + 1, lanes].astype(BF16)
                term = jnp.where(r2 < n1, e2, 0.0) * e1
                ws[k] = term if ws[k] is None else ws[k] + term
        for k, ii in enumerate(firsts):
            a = act[k * nk:(k + 1) * nk, lanes]
            gelu = (0.5 * a) * (1.0 + lax.erf(a * (2.0 ** -0.5)))
            d_ref[ii * nk // 2:(ii + 1) * nk // 2, lanes] = _pack_rows(ws[k] * gelu.astype(BF16))


def _peer_kernel(hnt_ref, u_ref, vt_ref, n1_ref, e1_ref, r2_ref, e2_ref, x1_ref, g_ref,
                 o_ref, acc_ref, d_ref):
    c = pl.program_id(1)
    nk = PEER_N_KEYS
    ec = u_ref.shape[0]
    sub = PEER_SUB_CHUNK
    piece = sub // PEER_MXU_SPLIT

    @pl.when(c == 0)
    def _():
        acc_ref[...] = jnp.zeros_like(acc_ref)

    for s in range(ec // sub):
        for p in range(PEER_MXU_SPLIT):
            rows = slice(s * sub + p * piece, s * sub + (p + 1) * piece)
            act = jnp.dot(u_ref[rows, :], hnt_ref[...], preferred_element_type=F32)
            firsts = range(rows.start // nk, rows.stop // nk)
            _peer_gates(n1_ref, e1_ref, r2_ref, e2_ref, firsts, act, d_ref)
        d_t = _unpack_rows(d_ref[s * sub // 2:(s + 1) * sub // 2, :])
        acc_ref[...] += jnp.dot(vt_ref[:, s * sub:(s + 1) * sub], d_t, preferred_element_type=F32)

    @pl.when(c == pl.num_programs(1) - 1)
    def _():
        y = x1_ref[...] + jnp.transpose(acc_ref[...])
        o_ref[...] = _rms(y, g_ref[...])


def _peer(hnt, u, v_t, n1, e1, r2, e2, x1, g):
    t = x1.shape[0]
    tb = min(PEER_TOKENS, t)
    ec = PEER_EXPERT_CHUNK
    packed_rows = PEER_HEADS * PEER_N_KEYS // 2
    dense = pl.BlockSpec((packed_rows, tb), lambda i, c: (0, i))
    r2 = r2.reshape(packed_rows, t)
    e2 = e2.reshape(packed_rows, t)
    rows = pl.BlockSpec((PEER_HEADS, ec // PEER_N_KEYS, tb), lambda i, c: (0, c, i))
    return pl.pallas_call(
        _peer_kernel,
        grid=(t // tb, PEER_N_EXPERTS // ec),
        in_specs=[pl.BlockSpec((D_MODEL, tb), lambda i, c: (0, i)),
                  pl.BlockSpec((ec, D_MODEL), lambda i, c: (c, 0)),
                  pl.BlockSpec((D_MODEL, ec), lambda i, c: (0, c)),
                  rows, rows, dense, dense,
                  pl.BlockSpec((tb, D_MODEL), lambda i, c: (i, 0)),
                  pl.BlockSpec((1, D_MODEL), lambda i, c: (0, 0))],
        out_specs=pl.BlockSpec((tb, D_MODEL), lambda i, c: (i, 0)),
        out_shape=jax.ShapeDtypeStruct((t, D_MODEL), F32),
        scratch_shapes=[pltpu.VMEM((D_MODEL, tb), F32), pltpu.VMEM((ec // 2, tb), jnp.uint32)],
        compiler_params=_params(("parallel", "arbitrary")),
        name="peer",
    )(hnt, u, v_t, n1, e1, r2, e2, x1, g)


def kernel(x, norm_mix_g, w_in, attn_sinks, hgrn_lb_logits, hgrn_norm_g, w_attn_proj, w_hgrn_proj,
           w_out, norm_ffn_g, w_peer_q, peer_keys, peer_u, peer_v, norm_final_g):
    b, s, d = x.shape
    assert d == D_MODEL and norm_mix_g.shape[0] == 1
    t = b * s
    x2 = x.reshape(t, d)
    row = lambda a: a.reshape(1, -1).astype(F32)

    aq, ak, av, hq, hf, hi, hg, ga, gh = _inproj(x2, row(norm_mix_g[0]), w_in[0].astype(BF16))
    y_attn = _attention(aq, ak, av, attn_sinks[0].astype(F32), b, s)
    y_hgrn = _hgrn(hq, hf, hi, hg, hgrn_lb_logits.astype(F32), row(hgrn_norm_g[0]), b, s)
    x1, hnt = _merge(x2, y_attn, y_hgrn, ga, gh, w_attn_proj[0].astype(BF16),
                     w_hgrn_proj[0].astype(BF16), w_out[0].astype(BF16), row(norm_ffn_g[0]))
    wq_t = jnp.transpose(w_peer_q[0]).astype(BF16)
    keys = peer_keys[0].astype(BF16)
    n1, e1, r2, e2 = _route(hnt, wq_t, keys[0], keys[1])
    out = _peer(hnt, peer_u[0].astype(BF16), jnp.transpose(peer_v[0]).astype(BF16),
                n1, e1, r2, e2, x1, row(norm_final_g))
    return out.reshape(b, s, d)
```

```python
import functools

import jax
import jax.numpy as jnp
import numpy as np
from jax import lax
from jax.experimental import pallas as pl
from jax.experimental.pallas import tpu as pltpu

F32 = jnp.float32
BF16 = jnp.bfloat16

D_MODEL = 1024
ATTN_HEADS = 8
ATTN_KV_HEADS = 2
ATTN_GROUP = ATTN_HEADS // ATTN_KV_HEADS
ATTN_HEAD_DIM = 64
ATTN_WIDTH = ATTN_HEADS * ATTN_HEAD_DIM
KV_WIDTH = ATTN_KV_HEADS * ATTN_HEAD_DIM
WINDOW = 128
ATTN_BLOCK = 128
HGRN_HEADS = 4
HGRN_DK = 128
HGRN_DV = 128
HGRN_WIDTH = HGRN_HEADS * HGRN_DK
HGRN_CHUNK = 32
PEER_HEADS = 8
PEER_N_KEYS = 128
PEER_N_EXPERTS = PEER_N_KEYS * PEER_N_KEYS
PEER_KEY_DIM = 128
PEER_TOPK = 16
EPS = 1e-6
MASK_VALUE = -1e30

SPLIT_SIZES = (ATTN_WIDTH, KV_WIDTH, KV_WIDTH, HGRN_WIDTH, HGRN_WIDTH,
               HGRN_WIDTH, HGRN_WIDTH, D_MODEL, D_MODEL)
IN_WIDTH = sum(SPLIT_SIZES)

VMEM_LIMIT_BYTES = 56 * 1024 * 1024

INPROJ_TOKENS = 512
HGRN_TOKENS = 256
HGRN_BATCH_ROWS = 2
MERGE_TOKENS = 512
ROUTE_TOKENS = 512
PEER_TOKENS = 512
PEER_EXPERT_CHUNK = 2048
PEER_SUB_CHUNK = 1024
LANE_TILE = 128
PEER_MXU_SPLIT = 4

CAND_FULL_ROWS = 8


def _params(semantics):
    return pltpu.CompilerParams(dimension_semantics=semantics,
                                vmem_limit_bytes=VMEM_LIMIT_BYTES)


def _nt_dot(a, b):
    return lax.dot_general(a, b, (((1,), (1,)), ((), ())), preferred_element_type=F32)


def _rms(x, g):
    ms = jnp.mean(x * x, axis=-1, keepdims=True)
    return x * lax.rsqrt(ms + EPS) * g


def _inproj_kernel(x_ref, g_ref, w_ref, *out_refs):
    h = _rms(x_ref[...], g_ref[...]).astype(BF16)
    off = 0
    for o_ref, width in zip(out_refs, SPLIT_SIZES):
        o_ref[...] = jnp.dot(h, w_ref[:, off:off + width],
                             preferred_element_type=F32).astype(o_ref.dtype)
        off += width


def _inproj(x2, g, w_in):
    t = x2.shape[0]
    tm = min(INPROJ_TOKENS, t)
    out_shape = tuple(jax.ShapeDtypeStruct((t, w), BF16) for w in SPLIT_SIZES)
    return pl.pallas_call(
        _inproj_kernel,
        grid=(t // tm,),
        in_specs=[pl.BlockSpec((tm, D_MODEL), lambda i: (i, 0)),
                  pl.BlockSpec((1, D_MODEL), lambda i: (0, 0)),
                  pl.BlockSpec((D_MODEL, IN_WIDTH), lambda i: (0, 0))],
        out_specs=tuple(pl.BlockSpec((tm, w), lambda i: (i, 0)) for w in SPLIT_SIZES),
        out_shape=out_shape,
        compiler_params=_params(("parallel",)),
        name="inproj",
    )(x2, g, w_in)


def _alibi_slopes():
    return [float(2.0 ** (-8.0 * h / ATTN_HEADS)) for h in range(1, ATTN_HEADS + 1)]


def _attn_kernel(sink_ref, q_ref, kp_ref, kc_ref, vp_ref, vc_ref, o_ref):
    n = pl.program_id(1)
    w = ATTN_BLOCK
    q = q_ref[0]
    k = jnp.concatenate([kp_ref[0], kc_ref[0]], axis=0)
    v = jnp.concatenate([vp_ref[0], vc_ref[0]], axis=0)
    qpos = lax.broadcasted_iota(jnp.int32, (w, 2 * w), 0) + w
    kpos = lax.broadcasted_iota(jnp.int32, (w, 2 * w), 1)
    dist = qpos - kpos
    first_ok = jnp.where(n > 0, 0, w)
    in_window = jnp.where(dist >= 0, jnp.where(dist < WINDOW, 1, 0), 0)
    valid = jnp.where(kpos >= first_ok, in_window, 0) == 1
    distf = dist.astype(F32)
    scale = ATTN_HEAD_DIM ** -0.5
    slopes = _alibi_slopes()
    outs = []
    for kh in range(ATTN_KV_HEADS):
        ksl = k[:, kh * ATTN_HEAD_DIM:(kh + 1) * ATTN_HEAD_DIM]
        vsl = v[:, kh * ATTN_HEAD_DIM:(kh + 1) * ATTN_HEAD_DIM]
        qg = jnp.concatenate(
            [q[:, (kh * ATTN_GROUP + g) * ATTN_HEAD_DIM:(kh * ATTN_GROUP + g + 1) * ATTN_HEAD_DIM]
             for g in range(ATTN_GROUP)], axis=0)
        s_all = _nt_dot(qg, ksl) * scale
        ps, denoms = [], []
        for g in range(ATTN_GROUP):
            head = kh * ATTN_GROUP + g
            s = s_all[g * w:(g + 1) * w] - slopes[head] * distf
            s = jnp.where(valid, s, MASK_VALUE)
            sink = sink_ref[head]
            m = jnp.maximum(jnp.max(s, axis=-1, keepdims=True), sink)
            p = jnp.exp(s - m)
            denoms.append(jnp.sum(p, axis=-1, keepdims=True) + jnp.exp(sink - m))
            ps.append(p.astype(BF16))
        o_all = jnp.dot(jnp.concatenate(ps, axis=0), vsl, preferred_element_type=F32)
        for g in range(ATTN_GROUP):
            outs.append(o_all[g * w:(g + 1) * w] / denoms[g])
    o_ref[0] = jnp.concatenate(outs, axis=-1).astype(o_ref.dtype)


def _attention(aq, ak, av, sinks, b, s):
    nb = s // ATTN_BLOCK
    q3 = aq.reshape(b, s, ATTN_WIDTH)
    k3 = ak.reshape(b, s, KV_WIDTH)
    v3 = av.reshape(b, s, KV_WIDTH)
    cur = lambda i, j: (i, j, 0)
    prev = lambda i, j: (i, jnp.maximum(j - 1, 0), 0)
    out = pl.pallas_call(
        _attn_kernel,
        grid=(b, nb),
        in_specs=[pl.BlockSpec(memory_space=pltpu.SMEM),
                  pl.BlockSpec((1, ATTN_BLOCK, ATTN_WIDTH), cur),
                  pl.BlockSpec((1, ATTN_BLOCK, KV_WIDTH), prev),
                  pl.BlockSpec((1, ATTN_BLOCK, KV_WIDTH), cur),
                  pl.BlockSpec((1, ATTN_BLOCK, KV_WIDTH), prev),
                  pl.BlockSpec((1, ATTN_BLOCK, KV_WIDTH), cur)],
        out_specs=pl.BlockSpec((1, ATTN_BLOCK, ATTN_WIDTH), cur),
        out_shape=jax.ShapeDtypeStruct((b, s, ATTN_WIDTH), BF16),
        compiler_params=_params(("parallel", "parallel")),
        name="attn",
    )(sinks, q3, k3, k3, v3, v3)
    return out.reshape(b * s, ATTN_WIDTH)


def _split3(x):
    hi = x.astype(BF16)
    r1 = x - hi.astype(F32)
    mid = r1.astype(BF16)
    lo = (r1 - mid.astype(F32)).astype(BF16)
    return hi, mid, lo


def _hgrn_kernel(lb_ref, gn_ref, hq_ref, hf_ref, hi_ref, hg_ref, o_ref, st_ref, o_scr):
    n_batch, sb = hq_ref.shape[0], hq_ref.shape[1]
    c = HGRN_CHUNK

    @pl.when(pl.program_id(1) == 0)
    def _():
        st_ref[...] = jnp.zeros_like(st_ref)

    logits = lb_ref[...]
    e = jnp.exp(logits - jnp.max(logits, axis=0, keepdims=True))
    lb = e[0:1] / jnp.sum(e, axis=0, keepdims=True)

    r = lax.broadcasted_iota(jnp.int32, (sb, sb), 0)
    cc = lax.broadcasted_iota(jnp.int32, (sb, sb), 1)
    same = (r // c) == (cc // c)
    tri = jnp.where(same, jnp.where(cc <= r, 1.0, 0.0), 0.0).astype(BF16)
    blk = jnp.where(same, 1.0, 0.0).astype(BF16)
    tr = lax.broadcasted_iota(jnp.int32, (c, c), 0)
    tc = lax.broadcasted_iota(jnp.int32, (c, c), 1)
    causal = tc <= tr

    def prepare(bb):
        f = lb + (1.0 - lb) * jax.nn.sigmoid(hf_ref[bb].astype(F32))
        kk = 1.0 - f
        parts = _split3(jnp.log(f))
        bcum = sum(jnp.dot(tri, p, preferred_element_type=F32) for p in parts)
        blast = sum(jnp.dot(blk, p, preferred_element_type=F32) for p in parts)
        hq = hq_ref[bb].astype(F32)
        q_dec = (hq * jax.nn.sigmoid(hq) * jnp.exp(bcum)).astype(BF16)
        k_inv = (kk * jnp.exp(-bcum)).astype(BF16)
        k_end = (kk * jnp.exp(blast - bcum)).astype(BF16)
        vv = hi_ref[bb]
        v_t = jnp.transpose(vv.astype(F32)).astype(BF16)
        return q_dec, k_inv, k_end, jnp.exp(blast), vv, v_t

    rows_of_batch = [prepare(bb) for bb in range(n_batch)]
    for ci in range(sb // c):
        rows = slice(ci * c, (ci + 1) * c)
        for bb, (q_dec, k_inv, k_end, decay, vv, v_t) in enumerate(rows_of_batch):
            outs = []
            for h in range(HGRN_HEADS):
                lanes = slice(h * HGRN_DK, (h + 1) * HGRN_DK)
                qd = q_dec[rows, lanes]
                a = jnp.where(causal, _nt_dot(qd, k_inv[rows, lanes]), 0.0)
                st = st_ref[bb, h]
                o = jnp.dot(a.astype(BF16), vv[rows, lanes], preferred_element_type=F32)
                o = o + _nt_dot(qd, st.astype(BF16))
                ds = jnp.dot(v_t[lanes, rows], k_end[rows, lanes], preferred_element_type=F32)
                st_ref[bb, h] = st * decay[ci * c:ci * c + 1, lanes] + ds
                outs.append(o)
            o_scr[bb, rows, :] = jnp.concatenate(outs, axis=-1)

    for bb in range(n_batch):
        hg = hg_ref[bb].astype(F32)
        gate = hg * jax.nn.sigmoid(hg)
        ys = []
        for h in range(HGRN_HEADS):
            lanes = slice(h * HGRN_DV, (h + 1) * HGRN_DV)
            ys.append(_rms(o_scr[bb, :, lanes], gn_ref[...]) * gate[:, lanes])
        o_ref[bb] = jnp.concatenate(ys, axis=-1).astype(o_ref.dtype)


def _hgrn(hq, hf, hi, hg, lb_logits, gn, b, s):
    sb = min(HGRN_TOKENS, s)
    nb = HGRN_BATCH_ROWS if b % HGRN_BATCH_ROWS == 0 else 1
    spec = pl.BlockSpec((nb, sb, HGRN_WIDTH), lambda i, j: (i, j, 0))
    r3 = lambda a: a.reshape(b, s, HGRN_WIDTH)
    out = pl.pallas_call(
        _hgrn_kernel,
        grid=(b // nb, s // sb),
        in_specs=[pl.BlockSpec(lb_logits.shape, lambda i, j: (0, 0)),
                  pl.BlockSpec((1, HGRN_DV), lambda i, j: (0, 0)),
                  spec, spec, spec, spec],
        out_specs=spec,
        out_shape=jax.ShapeDtypeStruct((b, s, HGRN_WIDTH), BF16),
        scratch_shapes=[pltpu.VMEM((nb, HGRN_HEADS, HGRN_DV, HGRN_DK), F32),
                        pltpu.VMEM((nb, sb, HGRN_WIDTH), F32)],
        compiler_params=_params(("parallel", "arbitrary")),
        name="hgrn",
    )(lb_logits, gn, r3(hq), r3(hf), r3(hi), r3(hg))
    return out.reshape(b * s, HGRN_WIDTH)


def _merge_kernel(x_ref, ya_ref, yh_ref, ga_ref, gh_ref, wa_ref, wh_ref, wo_ref, g_ref,
                  x1_ref, hnt_ref):
    pa = jnp.dot(ya_ref[...], wa_ref[...], preferred_element_type=F32)
    ph = jnp.dot(yh_ref[...], wh_ref[...], preferred_element_type=F32)
    merged = (jax.nn.sigmoid(ga_ref[...].astype(F32)) * pa
              + jax.nn.sigmoid(gh_ref[...].astype(F32)) * ph)
    x1 = x_ref[...] + jnp.dot(merged.astype(BF16), wo_ref[...], preferred_element_type=F32)
    x1_ref[...] = x1
    hnt_ref[...] = jnp.transpose(_rms(x1, g_ref[...])).astype(BF16)


def _merge(x2, ya, yh, ga, gh, wa, wh, wo, g):
    t = x2.shape[0]
    tm = min(MERGE_TOKENS, t)
    row = lambda w: pl.BlockSpec((tm, w), lambda i: (i, 0))
    full = lambda a: pl.BlockSpec(a.shape, lambda i: (0, 0))
    return pl.pallas_call(
        _merge_kernel,
        grid=(t // tm,),
        in_specs=[row(D_MODEL), row(ATTN_WIDTH), row(HGRN_WIDTH), row(D_MODEL), row(D_MODEL),
                  full(wa), full(wh), full(wo), full(g)],
        out_specs=(row(D_MODEL), pl.BlockSpec((D_MODEL, tm), lambda i: (0, i))),
        out_shape=(jax.ShapeDtypeStruct((t, D_MODEL), F32),
                   jax.ShapeDtypeStruct((D_MODEL, t), BF16)),
        compiler_params=_params(("parallel",)),
        name="merge",
    )(x2, ya, yh, ga, gh, wa, wh, wo, g)


def _as_f32(i):
    return jnp.asarray(i, jnp.int32).astype(F32)


def _topk_exact(s, k):
    n_rows, lanes = s.shape
    row = lax.broadcasted_iota(jnp.int32, (n_rows, lanes), 0).astype(F32)
    kid = lax.broadcasted_iota(jnp.int32, (k, lanes), 0).astype(F32)

    def body(a, carry):
        s, rank, vals = carry
        af = _as_f32(a)
        m = jnp.max(s, axis=0, keepdims=True)
        first = jnp.min(jnp.where(s == m, row, float(n_rows)), axis=0, keepdims=True)
        sel = row == first
        return jnp.where(sel, -jnp.inf, s), jnp.where(sel, af, rank), jnp.where(kid == af, m, vals)

    init = (s, jnp.full((n_rows, lanes), float(k), F32), jnp.zeros((k, lanes), F32))
    _, rank, vals = lax.fori_loop(0, k, body, init)
    return rank, vals


RANK_CODE_SCALE = 2.0 ** 123
RANK_CODE_LIMIT = -float(PEER_TOPK) * RANK_CODE_SCALE


def _topk_quick_pair(s1, s2, k):
    c1, c2, v1, v2 = s1, s2, [], []
    for a in range(k):
        code = -float(a + k) * RANK_CODE_SCALE
        m1 = jnp.max(c1, axis=0, keepdims=True)
        m2 = jnp.max(c2, axis=0, keepdims=True)
        c1 = jnp.where(c1 == m1, code, c1)
        c2 = jnp.where(c2 == m2, code, c2)
        v1.append(m1)
        v2.append(m2)
    v1 = jnp.concatenate(v1, axis=0)
    v2 = jnp.concatenate(v2, axis=0)

    def decode(c, s):
        marked = c <= RANK_CODE_LIMIT
        rank = jnp.where(marked, c * (-1.0 / RANK_CODE_SCALE) - float(k), float(k))
        n_marked = jnp.sum(jnp.where(marked, 1.0, 0.0), axis=0, keepdims=True)
        in_range = jnp.min(s, axis=0, keepdims=True) > RANK_CODE_LIMIT
        return rank, jnp.where(n_marked == float(k), jnp.where(in_range, 1.0, 0.0), 0.0)

    rank1, ok1 = decode(c1, s1)
    rank2, ok2 = decode(c2, s2)
    return rank1, v1, rank2, v2, ok1 * ok2


def _candidates(v1, v2, k):
    brow = lax.broadcasted_iota(jnp.int32, (k, v1.shape[1]), 0)
    full = [jnp.where(brow < k // (a + 1), v1[a:a + 1] + v2, -jnp.inf) for a in range(CAND_FULL_ROWS)]
    tail = v1[CAND_FULL_ROWS:] + v2[0:1]
    return jnp.concatenate(full + [tail], axis=0)


def _choose_exact(cand, k):
    n_cand, lanes = cand.shape
    crow = lax.broadcasted_iota(jnp.int32, (n_cand, lanes), 0).astype(F32)

    def body(_, c):
        m = jnp.max(c, axis=0, keepdims=True)
        first = jnp.min(jnp.where(c == m, crow, float(n_cand)), axis=0, keepdims=True)
        return jnp.where(crow == first, -jnp.inf, c)

    return lax.fori_loop(0, k, body, cand)


def _choose_quick(cand, k):
    for _ in range(k):
        cand = jnp.where(cand == jnp.max(cand, axis=0, keepdims=True), -jnp.inf, cand)
    return cand


def _pack_rows(x):
    return pltpu.bitcast(x, jnp.uint32)


def _unpack_rows(x):
    return pltpu.bitcast(x, BF16)


def _route_outputs(s1, s2, rank1, v1, rank2, v2, cand, left, k):
    chosen = jnp.where(left == -jnp.inf, jnp.where(cand > -jnp.inf, 1.0, 0.0), 0.0)
    top = v1[0:1] + v2[0:1]
    z = jnp.sum(chosen * jnp.exp(cand - top), axis=0, keepdims=True)
    counts = [jnp.sum(chosen[a * k:(a + 1) * k], axis=0, keepdims=True) for a in range(CAND_FULL_ROWS)]
    counts += [chosen[CAND_FULL_ROWS * k + a:CAND_FULL_ROWS * k + a + 1] for a in range(k - CAND_FULL_ROWS)]
    n1 = jnp.zeros_like(rank1)
    for a in range(k):
        n1 = jnp.where(rank1 == float(a), counts[a], n1)
    e1 = jnp.exp(s1 - v1[0:1])
    r2 = _pack_rows(rank2.astype(BF16))
    e2 = _pack_rows((jnp.exp(s2 - v2[0:1]) / z).astype(BF16))
    return (n1, e1, r2, e2), jnp.sum(chosen, axis=0, keepdims=True)


def _route_kernel(hnt_ref, wq_ref, k1_ref, k2_ref, n1_ref, e1_ref, r2_ref, e2_ref):
    k = PEER_TOPK
    out_refs = (n1_ref, e1_ref, r2_ref, e2_ref)
    q_t = jnp.dot(wq_ref[...], hnt_ref[...], preferred_element_type=F32).astype(BF16)
    s1 = jnp.dot(k1_ref[0], q_t[:PEER_KEY_DIM], preferred_element_type=F32)
    s2 = jnp.dot(k2_ref[0], q_t[PEER_KEY_DIM:], preferred_element_type=F32)

    def store(outs):
        for ref, val in zip(out_refs, outs):
            ref[0] = val

    tiles = [_topk_quick_pair(s1[:, lt * LANE_TILE:(lt + 1) * LANE_TILE],
                              s2[:, lt * LANE_TILE:(lt + 1) * LANE_TILE], k)
             for lt in range(s1.shape[1] // LANE_TILE)]
    rank1, v1, rank2, v2, ok = (jnp.concatenate(parts, axis=1) for parts in zip(*tiles))
    cand = _candidates(v1, v2, k)
    outs, n_chosen = _route_outputs(s1, s2, rank1, v1, rank2, v2, cand, _choose_quick(cand, k), k)
    store(outs)
    ok = ok * jnp.where(n_chosen == float(k), 1.0, 0.0)

    @pl.when(jnp.min(ok) < 1.0)
    def _():
        rank1, v1 = _topk_exact(s1, k)
        rank2, v2 = _topk_exact(s2, k)
        cand = _candidates(v1, v2, k)
        outs, _ = _route_outputs(s1, s2, rank1, v1, rank2, v2, cand, _choose_exact(cand, k), k)
        store(outs)


def _route(hnt, wq_t, keys1, keys2):
    t = hnt.shape[1]
    tb = min(ROUTE_TOKENS, t)
    kd2 = 2 * PEER_KEY_DIM
    out = jax.ShapeDtypeStruct((PEER_HEADS, PEER_N_KEYS, t), F32)
    out16 = jax.ShapeDtypeStruct((PEER_HEADS, PEER_N_KEYS // 2, t), jnp.uint32)
    pspec = pl.BlockSpec((1, PEER_N_KEYS // 2, tb), lambda i, h: (h, 0, i))
    ospec = pl.BlockSpec((1, PEER_N_KEYS, tb), lambda i, h: (h, 0, i))
    kspec = pl.BlockSpec((1, PEER_N_KEYS, PEER_KEY_DIM), lambda i, h: (h, 0, 0))
    return pl.pallas_call(
        _route_kernel,
        grid=(t // tb, PEER_HEADS),
        in_specs=[pl.BlockSpec((D_MODEL, tb), lambda i, h: (0, i)),
                  pl.BlockSpec((kd2, D_MODEL), lambda i, h: (h, 0)),
                  kspec, kspec],
        out_specs=(ospec, ospec, pspec, pspec),
        out_shape=(out, out, out16, out16),
        compiler_params=_params(("parallel", "arbitrary")),
        name="route",
    )(hnt, wq_t, keys1, keys2)


def _peer_gates(n1_ref, e1_ref, r2_ref, e2_ref, firsts, act, d_ref, first0):
    nk = PEER_N_KEYS
    for lt in range(d_ref.shape[1] // LANE_TILE):
        lanes = slice(lt * LANE_TILE, (lt + 1) * LANE_TILE)
        ws = [None] * len(firsts)
        for h in range(PEER_HEADS):
            keys = slice(h * nk // 2, (h + 1) * nk // 2)
            r2 = _unpack_rows(r2_ref[keys, lanes])
            e2 = _unpack_rows(e2_ref[keys, lanes])
            for k, ii in enumerate(firsts):
                n1 = n1_ref[h, ii:ii + 1, lanes].astype(BF16)
                e1 = e1_ref[h, ii:ii + 1, lanes].astype(BF16)
                term = jnp.where(r2 < n1, e2, 0.0) * e1
                ws[k] = term if ws[k] is None else ws[k] + term
        for k, ii in enumerate(firsts):
            a = act[k * nk:(k + 1) * nk, lanes].astype(BF16)
            gelu = (0.5 * a) * (1.0 + lax.erf(a * (2.0 ** -0.5)))
            d_ref[(ii - first0) * nk // 2:(ii - first0 + 1) * nk // 2, lanes] = _pack_rows(ws[k] * gelu)


def _peer_kernel(hnt_ref, u_ref, vt_ref, n1_ref, e1_ref, r2_ref, e2_ref, x1_ref, g_ref,
                 o_ref, acc_ref, *d_refs):
    c = pl.program_id(1)
    nk = PEER_N_KEYS
    sub = PEER_SUB_CHUNK
    piece = sub // PEER_MXU_SPLIT
    n_sub = u_ref.shape[0] // sub

    @pl.when(c == 0)
    def _():
        acc_ref[...] = jnp.zeros_like(acc_ref)

    acts = {}

    def first(s):
        for p in range(PEER_MXU_SPLIT):
            rows = slice(s * sub + p * piece, s * sub + (p + 1) * piece)
            acts[s, p] = rows, jnp.dot(u_ref[rows, :], hnt_ref[...], preferred_element_type=F32)

    def gate_and_second(s):
        for p in range(PEER_MXU_SPLIT):
            rows, act = acts.pop((s, p))
            firsts = range(rows.start // nk, rows.stop // nk)
            _peer_gates(n1_ref, e1_ref, r2_ref, e2_ref, firsts, act, d_refs[s], s * sub // nk)
        d_t = _unpack_rows(d_refs[s][...])
        acc_ref[...] += jnp.dot(vt_ref[:, s * sub:(s + 1) * sub], d_t, preferred_element_type=F32)

    first(0)
    for s in range(n_sub):
        if s + 1 < n_sub:
            first(s + 1)
        gate_and_second(s)

    @pl.when(c == pl.num_programs(1) - 1)
    def _():
        y = x1_ref[...] + jnp.transpose(acc_ref[...])
        o_ref[...] = _rms(y, g_ref[...])


def _peer(hnt, u, v_t, n1, e1, r2, e2, x1, g):
    t = x1.shape[0]
    tb = min(PEER_TOKENS, t)
    ec = PEER_EXPERT_CHUNK
    packed_rows = PEER_HEADS * PEER_N_KEYS // 2
    dense = pl.BlockSpec((packed_rows, tb), lambda i, c: (0, i))
    r2 = r2.reshape(packed_rows, t)
    e2 = e2.reshape(packed_rows, t)
    rows = pl.BlockSpec((PEER_HEADS, ec // PEER_N_KEYS, tb), lambda i, c: (0, c, i))
    return pl.pallas_call(
        _peer_kernel,
        grid=(t // tb, PEER_N_EXPERTS // ec),
        in_specs=[pl.BlockSpec((D_MODEL, tb), lambda i, c: (0, i)),
                  pl.BlockSpec((ec, D_MODEL), lambda i, c: (c, 0)),
                  pl.BlockSpec((D_MODEL, ec), lambda i, c: (0, c)),
                  rows, rows, dense, dense,
                  pl.BlockSpec((tb, D_MODEL), lambda i, c: (i, 0)),
                  pl.BlockSpec((1, D_MODEL), lambda i, c: (0, 0))],
        out_specs=pl.BlockSpec((tb, D_MODEL), lambda i, c: (i, 0)),
        out_shape=jax.ShapeDtypeStruct((t, D_MODEL), F32),
        scratch_shapes=([pltpu.VMEM((D_MODEL, tb), F32)]
                        + [pltpu.VMEM((PEER_SUB_CHUNK // 2, tb), jnp.uint32)] * (ec // PEER_SUB_CHUNK)),
        compiler_params=_params(("parallel", "arbitrary")),
        name="peer",
    )(hnt, u, v_t, n1, e1, r2, e2, x1, g)


def kernel(x, norm_mix_g, w_in, attn_sinks, hgrn_lb_logits, hgrn_norm_g, w_attn_proj, w_hgrn_proj,
           w_out, norm_ffn_g, w_peer_q, peer_keys, peer_u, peer_v, norm_final_g):
    b, s, d = x.shape
    assert d == D_MODEL and norm_mix_g.shape[0] == 1
    t = b * s
    x2 = x.reshape(t, d)
    row = lambda a: a.reshape(1, -1).astype(F32)

    aq, ak, av, hq, hf, hi, hg, ga, gh = _inproj(x2, row(norm_mix_g[0]), w_in[0].astype(BF16))
    y_attn = _attention(aq, ak, av, attn_sinks[0].astype(F32), b, s)
    y_hgrn = _hgrn(hq, hf, hi, hg, hgrn_lb_logits.astype(F32), row(hgrn_norm_g[0]), b, s)
    x1, hnt = _merge(x2, y_attn, y_hgrn, ga, gh, w_attn_proj[0].astype(BF16),
                     w_hgrn_proj[0].astype(BF16), w_out[0].astype(BF16), row(norm_ffn_g[0]))
    wq_t = jnp.transpose(w_peer_q[0]).astype(BF16)
    keys = peer_keys[0].astype(BF16)
    n1, e1, r2, e2 = _route(hnt, wq_t, keys[0], keys[1])
    out = _peer(hnt, peer_u[0].astype(BF16), jnp.transpose(peer_v[0]).astype(BF16),
                n1, e1, r2, e2, x1, row(norm_final_g))
    return out.reshape(b, s, d)
```

```python
import functools

import jax
import jax.numpy as jnp
import numpy as np
from jax import lax
from jax.experimental import pallas as pl
from jax.experimental.pallas import tpu as pltpu

F32 = jnp.float32
BF16 = jnp.bfloat16

D_MODEL = 1024
ATTN_HEADS = 8
ATTN_KV_HEADS = 2
ATTN_GROUP = ATTN_HEADS // ATTN_KV_HEADS
ATTN_HEAD_DIM = 64
ATTN_WIDTH = ATTN_HEADS * ATTN_HEAD_DIM
KV_WIDTH = ATTN_KV_HEADS * ATTN_HEAD_DIM
WINDOW = 128
ATTN_BLOCK = 128
HGRN_HEADS = 4
HGRN_DK = 128
HGRN_DV = 128
HGRN_WIDTH = HGRN_HEADS * HGRN_DK
HGRN_CHUNK = 32
PEER_HEADS = 8
PEER_N_KEYS = 128
PEER_N_EXPERTS = PEER_N_KEYS * PEER_N_KEYS
PEER_KEY_DIM = 128
PEER_TOPK = 16
EPS = 1e-6
MASK_VALUE = -1e30

SPLIT_SIZES = (ATTN_WIDTH, KV_WIDTH, KV_WIDTH, HGRN_WIDTH, HGRN_WIDTH,
               HGRN_WIDTH, HGRN_WIDTH, D_MODEL, D_MODEL)
IN_WIDTH = sum(SPLIT_SIZES)

VMEM_LIMIT_BYTES = 56 * 1024 * 1024

INPROJ_TOKENS = 512
HGRN_TOKENS = 256
HGRN_BATCH_ROWS = 2
MERGE_TOKENS = 512
ROUTE_TOKENS = 1024
PEER_TOKENS = 512
PEER_EXPERT_CHUNK = 2048
PEER_SUB_CHUNK = 1024
LANE_TILE = 128
PEER_MXU_SPLIT = 4

CAND_FULL_ROWS = 8


def _params(semantics):
    return pltpu.CompilerParams(dimension_semantics=semantics,
                                vmem_limit_bytes=VMEM_LIMIT_BYTES)


def _nt_dot(a, b):
    return lax.dot_general(a, b, (((1,), (1,)), ((), ())), preferred_element_type=F32)


def _rms(x, g):
    ms = jnp.mean(x * x, axis=-1, keepdims=True)
    return x * lax.rsqrt(ms + EPS) * g


def _inproj_kernel(x_ref, g_ref, w_ref, *out_refs):
    h = _rms(x_ref[...], g_ref[...]).astype(BF16)
    off = 0
    for o_ref, width in zip(out_refs, SPLIT_SIZES):
        o_ref[...] = jnp.dot(h, w_ref[:, off:off + width],
                             preferred_element_type=F32).astype(o_ref.dtype)
        off += width


def _inproj(x2, g, w_in):
    t = x2.shape[0]
    tm = min(INPROJ_TOKENS, t)
    out_shape = tuple(jax.ShapeDtypeStruct((t, w), BF16) for w in SPLIT_SIZES)
    return pl.pallas_call(
        _inproj_kernel,
        grid=(t // tm,),
        in_specs=[pl.BlockSpec((tm, D_MODEL), lambda i: (i, 0)),
                  pl.BlockSpec((1, D_MODEL), lambda i: (0, 0)),
                  pl.BlockSpec((D_MODEL, IN_WIDTH), lambda i: (0, 0))],
        out_specs=tuple(pl.BlockSpec((tm, w), lambda i: (i, 0)) for w in SPLIT_SIZES),
        out_shape=out_shape,
        compiler_params=_params(("parallel",)),
        name="inproj",
    )(x2, g, w_in)


def _alibi_slopes():
    return [float(2.0 ** (-8.0 * h / ATTN_HEADS)) for h in range(1, ATTN_HEADS + 1)]


def _attn_kernel(sink_ref, q_ref, kp_ref, kc_ref, vp_ref, vc_ref, o_ref):
    n = pl.program_id(1)
    w = ATTN_BLOCK
    q = q_ref[0]
    k = jnp.concatenate([kp_ref[0], kc_ref[0]], axis=0)
    v = jnp.concatenate([vp_ref[0], vc_ref[0]], axis=0)
    qpos = lax.broadcasted_iota(jnp.int32, (w, 2 * w), 0) + w
    kpos = lax.broadcasted_iota(jnp.int32, (w, 2 * w), 1)
    dist = qpos - kpos
    first_ok = jnp.where(n > 0, 0, w)
    in_window = jnp.where(dist >= 0, jnp.where(dist < WINDOW, 1, 0), 0)
    valid = jnp.where(kpos >= first_ok, in_window, 0) == 1
    distf = dist.astype(F32)
    scale = ATTN_HEAD_DIM ** -0.5
    slopes = _alibi_slopes()
    outs = []
    for kh in range(ATTN_KV_HEADS):
        ksl = k[:, kh * ATTN_HEAD_DIM:(kh + 1) * ATTN_HEAD_DIM]
        vsl = v[:, kh * ATTN_HEAD_DIM:(kh + 1) * ATTN_HEAD_DIM]
        qg = jnp.concatenate(
            [q[:, (kh * ATTN_GROUP + g) * ATTN_HEAD_DIM:(kh * ATTN_GROUP + g + 1) * ATTN_HEAD_DIM]
             for g in range(ATTN_GROUP)], axis=0)
        s_all = _nt_dot(qg, ksl) * scale
        ps, denoms = [], []
        for g in range(ATTN_GROUP):
            head = kh * ATTN_GROUP + g
            s = s_all[g * w:(g + 1) * w] - slopes[head] * distf
            s = jnp.where(valid, s, MASK_VALUE)
            sink = sink_ref[head]
            m = jnp.maximum(jnp.max(s, axis=-1, keepdims=True), sink)
            p = jnp.exp(s - m)
            denoms.append(jnp.sum(p, axis=-1, keepdims=True) + jnp.exp(sink - m))
            ps.append(p.astype(BF16))
        o_all = jnp.dot(jnp.concatenate(ps, axis=0), vsl, preferred_element_type=F32)
        for g in range(ATTN_GROUP):
            outs.append(o_all[g * w:(g + 1) * w] / denoms[g])
    o_ref[0] = jnp.concatenate(outs, axis=-1).astype(o_ref.dtype)


def _attention(aq, ak, av, sinks, b, s):
    nb = s // ATTN_BLOCK
    q3 = aq.reshape(b, s, ATTN_WIDTH)
    k3 = ak.reshape(b, s, KV_WIDTH)
    v3 = av.reshape(b, s, KV_WIDTH)
    cur = lambda i, j: (i, j, 0)
    prev = lambda i, j: (i, jnp.maximum(j - 1, 0), 0)
    out = pl.pallas_call(
        _attn_kernel,
        grid=(b, nb),
        in_specs=[pl.BlockSpec(memory_space=pltpu.SMEM),
                  pl.BlockSpec((1, ATTN_BLOCK, ATTN_WIDTH), cur),
                  pl.BlockSpec((1, ATTN_BLOCK, KV_WIDTH), prev),
                  pl.BlockSpec((1, ATTN_BLOCK, KV_WIDTH), cur),
                  pl.BlockSpec((1, ATTN_BLOCK, KV_WIDTH), prev),
                  pl.BlockSpec((1, ATTN_BLOCK, KV_WIDTH), cur)],
        out_specs=pl.BlockSpec((1, ATTN_BLOCK, ATTN_WIDTH), cur),
        out_shape=jax.ShapeDtypeStruct((b, s, ATTN_WIDTH), BF16),
        compiler_params=_params(("parallel", "parallel")),
        name="attn",
    )(sinks, q3, k3, k3, v3, v3)
    return out.reshape(b * s, ATTN_WIDTH)


def _split3(x):
    hi = x.astype(BF16)
    r1 = x - hi.astype(F32)
    mid = r1.astype(BF16)
    lo = (r1 - mid.astype(F32)).astype(BF16)
    return hi, mid, lo


def _hgrn_kernel(lb_ref, gn_ref, hq_ref, hf_ref, hi_ref, hg_ref, o_ref, st_ref, o_scr):
    n_batch, sb = hq_ref.shape[0], hq_ref.shape[1]
    c = HGRN_CHUNK

    @pl.when(pl.program_id(1) == 0)
    def _():
        st_ref[...] = jnp.zeros_like(st_ref)

    logits = lb_ref[...]
    e = jnp.exp(logits - jnp.max(logits, axis=0, keepdims=True))
    lb = e[0:1] / jnp.sum(e, axis=0, keepdims=True)

    r = lax.broadcasted_iota(jnp.int32, (sb, sb), 0)
    cc = lax.broadcasted_iota(jnp.int32, (sb, sb), 1)
    same = (r // c) == (cc // c)
    tri = jnp.where(same, jnp.where(cc <= r, 1.0, 0.0), 0.0).astype(BF16)
    blk = jnp.where(same, 1.0, 0.0).astype(BF16)
    tr = lax.broadcasted_iota(jnp.int32, (c, c), 0)
    tc = lax.broadcasted_iota(jnp.int32, (c, c), 1)
    causal = tc <= tr

    def prepare(bb):
        f = lb + (1.0 - lb) * jax.nn.sigmoid(hf_ref[bb].astype(F32))
        kk = 1.0 - f
        parts = _split3(jnp.log(f))
        bcum = sum(jnp.dot(tri, p, preferred_element_type=F32) for p in parts)
        blast = sum(jnp.dot(blk, p, preferred_element_type=F32) for p in parts)
        hq = hq_ref[bb].astype(F32)
        q_dec = (hq * jax.nn.sigmoid(hq) * jnp.exp(bcum)).astype(BF16)
        k_inv = (kk * jnp.exp(-bcum)).astype(BF16)
        k_end = (kk * jnp.exp(blast - bcum)).astype(BF16)
        vv = hi_ref[bb]
        v_t = jnp.transpose(vv.astype(F32)).astype(BF16)
        return q_dec, k_inv, k_end, jnp.exp(blast), vv, v_t

    rows_of_batch = [prepare(bb) for bb in range(n_batch)]
    for ci in range(sb // c):
        rows = slice(ci * c, (ci + 1) * c)
        for bb, (q_dec, k_inv, k_end, decay, vv, v_t) in enumerate(rows_of_batch):
            outs = []
            for h in range(HGRN_HEADS):
                lanes = slice(h * HGRN_DK, (h + 1) * HGRN_DK)
                qd = q_dec[rows, lanes]
                a = jnp.where(causal, _nt_dot(qd, k_inv[rows, lanes]), 0.0)
                st = st_ref[bb, h]
                o = jnp.dot(a.astype(BF16), vv[rows, lanes], preferred_element_type=F32)
                o = o + _nt_dot(qd, st.astype(BF16))
                ds = jnp.dot(v_t[lanes, rows], k_end[rows, lanes], preferred_element_type=F32)
                st_ref[bb, h] = st * decay[ci * c:ci * c + 1, lanes] + ds
                outs.append(o)
            o_scr[bb, rows, :] = jnp.concatenate(outs, axis=-1)

    for bb in range(n_batch):
        hg = hg_ref[bb].astype(F32)
        gate = hg * jax.nn.sigmoid(hg)
        ys = []
        for h in range(HGRN_HEADS):
            lanes = slice(h * HGRN_DV, (h + 1) * HGRN_DV)
            ys.append(_rms(o_scr[bb, :, lanes], gn_ref[...]) * gate[:, lanes])
        o_ref[bb] = jnp.concatenate(ys, axis=-1).astype(o_ref.dtype)


def _hgrn(hq, hf, hi, hg, lb_logits, gn, b, s):
    sb = min(HGRN_TOKENS, s)
    nb = HGRN_BATCH_ROWS if b % HGRN_BATCH_ROWS == 0 else 1
    spec = pl.BlockSpec((nb, sb, HGRN_WIDTH), lambda i, j: (i, j, 0))
    r3 = lambda a: a.reshape(b, s, HGRN_WIDTH)
    out = pl.pallas_call(
        _hgrn_kernel,
        grid=(b // nb, s // sb),
        in_specs=[pl.BlockSpec(lb_logits.shape, lambda i, j: (0, 0)),
                  pl.BlockSpec((1, HGRN_DV), lambda i, j: (0, 0)),
                  spec, spec, spec, spec],
        out_specs=spec,
        out_shape=jax.ShapeDtypeStruct((b, s, HGRN_WIDTH), BF16),
        scratch_shapes=[pltpu.VMEM((nb, HGRN_HEADS, HGRN_DV, HGRN_DK), F32),
                        pltpu.VMEM((nb, sb, HGRN_WIDTH), F32)],
        compiler_params=_params(("parallel", "arbitrary")),
        name="hgrn",
    )(lb_logits, gn, r3(hq), r3(hf), r3(hi), r3(hg))
    return out.reshape(b * s, HGRN_WIDTH)


def _merge_kernel(x_ref, ya_ref, yh_ref, ga_ref, gh_ref, wa_ref, wh_ref, wo_ref, g_ref,
                  x1_ref, hnt_ref):
    pa = jnp.dot(ya_ref[...], wa_ref[...], preferred_element_type=F32)
    ph = jnp.dot(yh_ref[...], wh_ref[...], preferred_element_type=F32)
    merged = (jax.nn.sigmoid(ga_ref[...].astype(F32)) * pa
              + jax.nn.sigmoid(gh_ref[...].astype(F32)) * ph)
    x1 = x_ref[...] + jnp.dot(merged.astype(BF16), wo_ref[...], preferred_element_type=F32)
    x1_ref[...] = x1
    hnt_ref[...] = jnp.transpose(_rms(x1, g_ref[...])).astype(BF16)


def _merge(x2, ya, yh, ga, gh, wa, wh, wo, g):
    t = x2.shape[0]
    tm = min(MERGE_TOKENS, t)
    row = lambda w: pl.BlockSpec((tm, w), lambda i: (i, 0))
    full = lambda a: pl.BlockSpec(a.shape, lambda i: (0, 0))
    return pl.pallas_call(
        _merge_kernel,
        grid=(t // tm,),
        in_specs=[row(D_MODEL), row(ATTN_WIDTH), row(HGRN_WIDTH), row(D_MODEL), row(D_MODEL),
                  full(wa), full(wh), full(wo), full(g)],
        out_specs=(row(D_MODEL), pl.BlockSpec((D_MODEL, tm), lambda i: (0, i))),
        out_shape=(jax.ShapeDtypeStruct((t, D_MODEL), F32),
                   jax.ShapeDtypeStruct((D_MODEL, t), BF16)),
        compiler_params=_params(("parallel",)),
        name="merge",
    )(x2, ya, yh, ga, gh, wa, wh, wo, g)


def _as_f32(i):
    return jnp.asarray(i, jnp.int32).astype(F32)


def _topk_exact(s, k):
    n_rows, lanes = s.shape
    row = lax.broadcasted_iota(jnp.int32, (n_rows, lanes), 0).astype(F32)
    kid = lax.broadcasted_iota(jnp.int32, (k, lanes), 0).astype(F32)

    def body(a, carry):
        s, rank, vals = carry
        af = _as_f32(a)
        m = jnp.max(s, axis=0, keepdims=True)
        first = jnp.min(jnp.where(s == m, row, float(n_rows)), axis=0, keepdims=True)
        sel = row == first
        return jnp.where(sel, -jnp.inf, s), jnp.where(sel, af, rank), jnp.where(kid == af, m, vals)

    init = (s, jnp.full((n_rows, lanes), float(k), F32), jnp.zeros((k, lanes), F32))
    _, rank, vals = lax.fori_loop(0, k, body, init)
    return rank, vals


RANK_CODE_SCALE = 2.0 ** 123
RANK_CODE_LIMIT = -float(PEER_TOPK) * RANK_CODE_SCALE


def _topk_quick_pair(s1, s2, k):
    c1, c2, v1, v2 = s1, s2, [], []
    for a in range(k):
        code = -float(a + k) * RANK_CODE_SCALE
        m1 = jnp.max(c1, axis=0, keepdims=True)
        m2 = jnp.max(c2, axis=0, keepdims=True)
        c1 = jnp.where(c1 == m1, code, c1)
        c2 = jnp.where(c2 == m2, code, c2)
        v1.append(m1)
        v2.append(m2)
    v1 = jnp.concatenate(v1, axis=0)
    v2 = jnp.concatenate(v2, axis=0)

    def decode(c, s):
        marked = c <= RANK_CODE_LIMIT
        rank = jnp.where(marked, c * (-1.0 / RANK_CODE_SCALE) - float(k), float(k))
        n_marked = jnp.sum(jnp.where(marked, 1.0, 0.0), axis=0, keepdims=True)
        in_range = jnp.min(s, axis=0, keepdims=True) > RANK_CODE_LIMIT
        return rank, jnp.where(n_marked == float(k), jnp.where(in_range, 1.0, 0.0), 0.0)

    rank1, ok1 = decode(c1, s1)
    rank2, ok2 = decode(c2, s2)
    return rank1, v1, rank2, v2, ok1 * ok2


def _candidates(v1, v2, k):
    brow = lax.broadcasted_iota(jnp.int32, (k, v1.shape[1]), 0)
    full = [jnp.where(brow < k // (a + 1), v1[a:a + 1] + v2, -jnp.inf) for a in range(CAND_FULL_ROWS)]
    tail = v1[CAND_FULL_ROWS:] + v2[0:1]
    return jnp.concatenate(full + [tail], axis=0)


def _choose_exact(cand, k):
    n_cand, lanes = cand.shape
    crow = lax.broadcasted_iota(jnp.int32, (n_cand, lanes), 0).astype(F32)

    def body(_, c):
        m = jnp.max(c, axis=0, keepdims=True)
        first = jnp.min(jnp.where(c == m, crow, float(n_cand)), axis=0, keepdims=True)
        return jnp.where(crow == first, -jnp.inf, c)

    return lax.fori_loop(0, k, body, cand)


def _choose_quick(cand, k):
    for _ in range(k):
        cand = jnp.where(cand == jnp.max(cand, axis=0, keepdims=True), -jnp.inf, cand)
    return cand


def _pack_rows(x):
    return pltpu.bitcast(x, jnp.uint32)


def _unpack_rows(x):
    return pltpu.bitcast(x, BF16)


def _route_outputs(s1, s2, rank1, v1, rank2, v2, cand, left, k):
    chosen = jnp.where(left == -jnp.inf, jnp.where(cand > -jnp.inf, 1.0, 0.0), 0.0)
    top = v1[0:1] + v2[0:1]
    z = jnp.sum(chosen * jnp.exp(cand - top), axis=0, keepdims=True)
    counts = [jnp.sum(chosen[a * k:(a + 1) * k], axis=0, keepdims=True) for a in range(CAND_FULL_ROWS)]
    counts += [chosen[CAND_FULL_ROWS * k + a:CAND_FULL_ROWS * k + a + 1] for a in range(k - CAND_FULL_ROWS)]
    n1 = jnp.zeros_like(rank1)
    for a in range(k):
        n1 = jnp.where(rank1 == float(a), counts[a], n1)
    e1 = jnp.exp(s1 - v1[0:1])
    r2 = _pack_rows(rank2.astype(BF16))
    e2 = _pack_rows((jnp.exp(s2 - v2[0:1]) / z).astype(BF16))
    return (n1, e1, r2, e2), jnp.sum(chosen, axis=0, keepdims=True)


def _route_kernel(hnt_ref, wq_ref, k1_ref, k2_ref, n1_ref, e1_ref, r2_ref, e2_ref):
    k = PEER_TOPK
    out_refs = (n1_ref, e1_ref, r2_ref, e2_ref)
    q_t = jnp.dot(wq_ref[...], hnt_ref[...], preferred_element_type=F32).astype(BF16)
    s1 = jnp.dot(k1_ref[0], q_t[:PEER_KEY_DIM], preferred_element_type=F32)
    s2 = jnp.dot(k2_ref[0], q_t[PEER_KEY_DIM:], preferred_element_type=F32)

    def store(outs):
        for ref, val in zip(out_refs, outs):
            ref[0] = val

    tiles = [_topk_quick_pair(s1[:, lt * LANE_TILE:(lt + 1) * LANE_TILE],
                              s2[:, lt * LANE_TILE:(lt + 1) * LANE_TILE], k)
             for lt in range(s1.shape[1] // LANE_TILE)]
    rank1, v1, rank2, v2, ok = (jnp.concatenate(parts, axis=1) for parts in zip(*tiles))
    cand = _candidates(v1, v2, k)
    outs, n_chosen = _route_outputs(s1, s2, rank1, v1, rank2, v2, cand, _choose_quick(cand, k), k)
    store(outs)
    ok = ok * jnp.where(n_chosen == float(k), 1.0, 0.0)

    @pl.when(jnp.min(ok) < 1.0)
    def _():
        rank1, v1 = _topk_exact(s1, k)
        rank2, v2 = _topk_exact(s2, k)
        cand = _candidates(v1, v2, k)
        outs, _ = _route_outputs(s1, s2, rank1, v1, rank2, v2, cand, _choose_exact(cand, k), k)
        store(outs)


def _route(hnt, wq_t, keys1, keys2):
    t = hnt.shape[1]
    tb = min(ROUTE_TOKENS, t)
    kd2 = 2 * PEER_KEY_DIM
    out = jax.ShapeDtypeStruct((PEER_HEADS, PEER_N_KEYS, t), F32)
    out16 = jax.ShapeDtypeStruct((PEER_HEADS, PEER_N_KEYS // 2, t), jnp.uint32)
    pspec = pl.BlockSpec((1, PEER_N_KEYS // 2, tb), lambda i, h: (h, 0, i))
    ospec = pl.BlockSpec((1, PEER_N_KEYS, tb), lambda i, h: (h, 0, i))
    kspec = pl.BlockSpec((1, PEER_N_KEYS, PEER_KEY_DIM), lambda i, h: (h, 0, 0))
    return pl.pallas_call(
        _route_kernel,
        grid=(t // tb, PEER_HEADS),
        in_specs=[pl.BlockSpec((D_MODEL, tb), lambda i, h: (0, i)),
                  pl.BlockSpec((kd2, D_MODEL), lambda i, h: (h, 0)),
                  kspec, kspec],
        out_specs=(ospec, ospec, pspec, pspec),
        out_shape=(out, out, out16, out16),
        compiler_params=_params(("parallel", "arbitrary")),
        name="route",
    )(hnt, wq_t, keys1, keys2)


def _peer_gates(n1_ref, e1_ref, r2_ref, e2_ref, firsts, act, d_ref, first0):
    nk = PEER_N_KEYS
    for lt in range(d_ref.shape[1] // LANE_TILE):
        lanes = slice(lt * LANE_TILE, (lt + 1) * LANE_TILE)
        ws = [None] * len(firsts)
        for h in range(PEER_HEADS):
            keys = slice(h * nk // 2, (h + 1) * nk // 2)
            r2 = _unpack_rows(r2_ref[keys, lanes])
            e2 = _unpack_rows(e2_ref[keys, lanes])
            for k, ii in enumerate(firsts):
                n1 = n1_ref[h, ii:ii + 1, lanes].astype(BF16)
                e1 = e1_ref[h, ii:ii + 1, lanes].astype(BF16)
                term = jnp.where(r2 < n1, e2, 0.0) * e1
                ws[k] = term if ws[k] is None else ws[k] + term
        for k, ii in enumerate(firsts):
            a = act[k * nk:(k + 1) * nk, lanes].astype(BF16)
            gelu = (0.5 * a) * (1.0 + lax.erf(a * (2.0 ** -0.5)))
            d_ref[(ii - first0) * nk // 2:(ii - first0 + 1) * nk // 2, lanes] = _pack_rows(ws[k] * gelu)


def _peer_kernel(hnt_ref, u_ref, vt_ref, n1_ref, e1_ref, r2_ref, e2_ref, x1_ref, g_ref,
                 o_ref, acc_ref, *d_refs):
    c = pl.program_id(1)
    nk = PEER_N_KEYS
    sub = PEER_SUB_CHUNK
    piece = sub // PEER_MXU_SPLIT
    n_sub = u_ref.shape[0] // sub

    @pl.when(c == 0)
    def _():
        acc_ref[...] = jnp.zeros_like(acc_ref)

    acts = {}

    def first(s):
        for p in range(PEER_MXU_SPLIT):
            rows = slice(s * sub + p * piece, s * sub + (p + 1) * piece)
            acts[s, p] = rows, jnp.dot(u_ref[rows, :], hnt_ref[...], preferred_element_type=F32)

    def gate_and_second(s):
        for p in range(PEER_MXU_SPLIT):
            rows, act = acts.pop((s, p))
            firsts = range(rows.start // nk, rows.stop // nk)
            _peer_gates(n1_ref, e1_ref, r2_ref, e2_ref, firsts, act, d_refs[s], s * sub // nk)
        d_t = _unpack_rows(d_refs[s][...])
        acc_ref[...] += jnp.dot(vt_ref[:, s * sub:(s + 1) * sub], d_t, preferred_element_type=F32)

    first(0)
    for s in range(n_sub):
        if s + 1 < n_sub:
            first(s + 1)
        gate_and_second(s)

    @pl.when(c == pl.num_programs(1) - 1)
    def _():
        y = x1_ref[...] + jnp.transpose(acc_ref[...])
        o_ref[...] = _rms(y, g_ref[...])


def _peer(hnt, u, v_t, n1, e1, r2, e2, x1, g):
    t = x1.shape[0]
    tb = min(PEER_TOKENS, t)
    ec = PEER_EXPERT_CHUNK
    packed_rows = PEER_HEADS * PEER_N_KEYS // 2
    dense = pl.BlockSpec((packed_rows, tb), lambda i, c: (0, i))
    r2 = r2.reshape(packed_rows, t)
    e2 = e2.reshape(packed_rows, t)
    rows = pl.BlockSpec((PEER_HEADS, ec // PEER_N_KEYS, tb), lambda i, c: (0, c, i))
    return pl.pallas_call(
        _peer_kernel,
        grid=(t // tb, PEER_N_EXPERTS // ec),
        in_specs=[pl.BlockSpec((D_MODEL, tb), lambda i, c: (0, i)),
                  pl.BlockSpec((ec, D_MODEL), lambda i, c: (c, 0)),
                  pl.BlockSpec((D_MODEL, ec), lambda i, c: (0, c)),
                  rows, rows, dense, dense,
                  pl.BlockSpec((tb, D_MODEL), lambda i, c: (i, 0)),
                  pl.BlockSpec((1, D_MODEL), lambda i, c: (0, 0))],
        out_specs=pl.BlockSpec((tb, D_MODEL), lambda i, c: (i, 0)),
        out_shape=jax.ShapeDtypeStruct((t, D_MODEL), F32),
        scratch_shapes=([pltpu.VMEM((D_MODEL, tb), F32)]
                        + [pltpu.VMEM((PEER_SUB_CHUNK // 2, tb), jnp.uint32)] * (ec // PEER_SUB_CHUNK)),
        compiler_params=_params(("parallel", "arbitrary")),
        name="peer",
    )(hnt, u, v_t, n1, e1, r2, e2, x1, g)


def kernel(x, norm_mix_g, w_in, attn_sinks, hgrn_lb_logits, hgrn_norm_g, w_attn_proj, w_hgrn_proj,
           w_out, norm_ffn_g, w_peer_q, peer_keys, peer_u, peer_v, norm_final_g):
    b, s, d = x.shape
    assert d == D_MODEL and norm_mix_g.shape[0] == 1
    t = b * s
    x2 = x.reshape(t, d)
    row = lambda a: a.reshape(1, -1).astype(F32)

    aq, ak, av, hq, hf, hi, hg, ga, gh = _inproj(x2, row(norm_mix_g[0]), w_in[0].astype(BF16))
    y_attn = _attention(aq, ak, av, attn_sinks[0].astype(F32), b, s)
    y_hgrn = _hgrn(hq, hf, hi, hg, hgrn_lb_logits.astype(F32), row(hgrn_norm_g[0]), b, s)
    x1, hnt = _merge(x2, y_attn, y_hgrn, ga, gh, w_attn_proj[0].astype(BF16),
                     w_hgrn_proj[0].astype(BF16), w_out[0].astype(BF16), row(norm_ffn_g[0]))
    wq_t = jnp.transpose(w_peer_q[0]).astype(BF16)
    keys = peer_keys[0].astype(BF16)
    n1, e1, r2, e2 = _route(hnt, wq_t, keys[0], keys[1])
    out = _peer(hnt, peer_u[0].astype(BF16), jnp.transpose(peer_v[0]).astype(BF16),
                n1, e1, r2, e2, x1, row(norm_final_g))
    return out.reshape(b, s, d)
```

```python
import functools

import jax
import jax.numpy as jnp
import numpy as np
from jax import lax
from jax.experimental import pallas as pl
from jax.experimental.pallas import tpu as pltpu

F32 = jnp.float32
BF16 = jnp.bfloat16

D_MODEL = 1024
ATTN_HEADS = 8
ATTN_KV_HEADS = 2
ATTN_GROUP = ATTN_HEADS // ATTN_KV_HEADS
ATTN_HEAD_DIM = 64
ATTN_WIDTH = ATTN_HEADS * ATTN_HEAD_DIM
KV_WIDTH = ATTN_KV_HEADS * ATTN_HEAD_DIM
WINDOW = 128
ATTN_BLOCK = 128
HGRN_HEADS = 4
HGRN_DK = 128
HGRN_DV = 128
HGRN_WIDTH = HGRN_HEADS * HGRN_DK
HGRN_CHUNK = 32
PEER_HEADS = 8
PEER_N_KEYS = 128
PEER_N_EXPERTS = PEER_N_KEYS * PEER_N_KEYS
PEER_KEY_DIM = 128
PEER_TOPK = 16
EPS = 1e-6
MASK_VALUE = -1e30

SPLIT_SIZES = (ATTN_WIDTH, KV_WIDTH, KV_WIDTH, HGRN_WIDTH, HGRN_WIDTH,
               HGRN_WIDTH, HGRN_WIDTH, D_MODEL, D_MODEL)
IN_WIDTH = sum(SPLIT_SIZES)

VMEM_LIMIT_BYTES = 56 * 1024 * 1024

INPROJ_TOKENS = 512
HGRN_TOKENS = 256
HGRN_BATCH_ROWS = 2
MERGE_TOKENS = 512
ROUTE_TOKENS = 512
PEER_TOKENS = 512
PEER_EXPERT_CHUNK = 2048
PEER_SUB_CHUNK = 1024
LANE_TILE = 128
SUBLANES = 8
PEER_MXU_SPLIT = 4


def _params(semantics):
    return pltpu.CompilerParams(dimension_semantics=semantics,
                                vmem_limit_bytes=VMEM_LIMIT_BYTES)


def _nt_dot(a, b):
    return lax.dot_general(a, b, (((1,), (1,)), ((), ())), preferred_element_type=F32)


def _rms(x, g):
    ms = jnp.mean(x * x, axis=-1, keepdims=True)
    return x * lax.rsqrt(ms + EPS) * g


def _inproj_kernel(x_ref, g_ref, w_ref, *out_refs):
    h = _rms(x_ref[...], g_ref[...]).astype(BF16)
    off = 0
    for o_ref, width in zip(out_refs, SPLIT_SIZES):
        o_ref[...] = jnp.dot(h, w_ref[:, off:off + width],
                             preferred_element_type=F32).astype(o_ref.dtype)
        off += width


def _inproj(x2, g, w_in):
    t = x2.shape[0]
    tm = min(INPROJ_TOKENS, t)
    out_shape = tuple(jax.ShapeDtypeStruct((t, w), BF16) for w in SPLIT_SIZES)
    return pl.pallas_call(
        _inproj_kernel,
        grid=(t // tm,),
        in_specs=[pl.BlockSpec((tm, D_MODEL), lambda i: (i, 0)),
                  pl.BlockSpec((1, D_MODEL), lambda i: (0, 0)),
                  pl.BlockSpec((D_MODEL, IN_WIDTH), lambda i: (0, 0))],
        out_specs=tuple(pl.BlockSpec((tm, w), lambda i: (i, 0)) for w in SPLIT_SIZES),
        out_shape=out_shape,
        compiler_params=_params(("parallel",)),
        name="inproj",
    )(x2, g, w_in)


def _alibi_slopes():
    return [float(2.0 ** (-8.0 * h / ATTN_HEADS)) for h in range(1, ATTN_HEADS + 1)]


def _attn_kernel(sink_ref, q_ref, kp_ref, kc_ref, vp_ref, vc_ref, o_ref):
    n = pl.program_id(1)
    w = ATTN_BLOCK
    q = q_ref[0]
    k = jnp.concatenate([kp_ref[0], kc_ref[0]], axis=0)
    v = jnp.concatenate([vp_ref[0], vc_ref[0]], axis=0)
    qpos = lax.broadcasted_iota(jnp.int32, (w, 2 * w), 0) + w
    kpos = lax.broadcasted_iota(jnp.int32, (w, 2 * w), 1)
    dist = qpos - kpos
    first_ok = jnp.where(n > 0, 0, w)
    in_window = jnp.where(dist >= 0, jnp.where(dist < WINDOW, 1, 0), 0)
    valid = jnp.where(kpos >= first_ok, in_window, 0) == 1
    distf = dist.astype(F32)
    scale = ATTN_HEAD_DIM ** -0.5
    slopes = _alibi_slopes()
    outs = []
    for kh in range(ATTN_KV_HEADS):
        ksl = k[:, kh * ATTN_HEAD_DIM:(kh + 1) * ATTN_HEAD_DIM]
        vsl = v[:, kh * ATTN_HEAD_DIM:(kh + 1) * ATTN_HEAD_DIM]
        qg = jnp.concatenate(
            [q[:, (kh * ATTN_GROUP + g) * ATTN_HEAD_DIM:(kh * ATTN_GROUP + g + 1) * ATTN_HEAD_DIM]
             for g in range(ATTN_GROUP)], axis=0)
        s_all = _nt_dot(qg, ksl) * scale
        ps, denoms = [], []
        for g in range(ATTN_GROUP):
            head = kh * ATTN_GROUP + g
            s = s_all[g * w:(g + 1) * w] - slopes[head] * distf
            s = jnp.where(valid, s, MASK_VALUE)
            sink = sink_ref[head]
            m = jnp.maximum(jnp.max(s, axis=-1, keepdims=True), sink)
            p = jnp.exp(s - m)
            denoms.append(jnp.sum(p, axis=-1, keepdims=True) + jnp.exp(sink - m))
            ps.append(p.astype(BF16))
        o_all = jnp.dot(jnp.concatenate(ps, axis=0), vsl, preferred_element_type=F32)
        for g in range(ATTN_GROUP):
            outs.append(o_all[g * w:(g + 1) * w] / denoms[g])
    o_ref[0] = jnp.concatenate(outs, axis=-1).astype(o_ref.dtype)


def _attention(aq, ak, av, sinks, b, s):
    nb = s // ATTN_BLOCK
    q3 = aq.reshape(b, s, ATTN_WIDTH)
    k3 = ak.reshape(b, s, KV_WIDTH)
    v3 = av.reshape(b, s, KV_WIDTH)
    cur = lambda i, j: (i, j, 0)
    prev = lambda i, j: (i, jnp.maximum(j - 1, 0), 0)
    out = pl.pallas_call(
        _attn_kernel,
        grid=(b, nb),
        in_specs=[pl.BlockSpec(memory_space=pltpu.SMEM),
                  pl.BlockSpec((1, ATTN_BLOCK, ATTN_WIDTH), cur),
                  pl.BlockSpec((1, ATTN_BLOCK, KV_WIDTH), prev),
                  pl.BlockSpec((1, ATTN_BLOCK, KV_WIDTH), cur),
                  pl.BlockSpec((1, ATTN_BLOCK, KV_WIDTH), prev),
                  pl.BlockSpec((1, ATTN_BLOCK, KV_WIDTH), cur)],
        out_specs=pl.BlockSpec((1, ATTN_BLOCK, ATTN_WIDTH), cur),
        out_shape=jax.ShapeDtypeStruct((b, s, ATTN_WIDTH), BF16),
        compiler_params=_params(("parallel", "parallel")),
        name="attn",
    )(sinks, q3, k3, k3, v3, v3)
    return out.reshape(b * s, ATTN_WIDTH)


def _split3(x):
    hi = x.astype(BF16)
    r1 = x - hi.astype(F32)
    mid = r1.astype(BF16)
    lo = (r1 - mid.astype(F32)).astype(BF16)
    return hi, mid, lo


def _hgrn_kernel(lb_ref, gn_ref, hq_ref, hf_ref, hi_ref, hg_ref, o_ref, st_ref, o_scr):
    n_batch, sb = hq_ref.shape[0], hq_ref.shape[1]
    c = HGRN_CHUNK

    @pl.when(pl.program_id(1) == 0)
    def _():
        st_ref[...] = jnp.zeros_like(st_ref)

    logits = lb_ref[...]
    e = jnp.exp(logits - jnp.max(logits, axis=0, keepdims=True))
    lb = e[0:1] / jnp.sum(e, axis=0, keepdims=True)

    r = lax.broadcasted_iota(jnp.int32, (sb, sb), 0)
    cc = lax.broadcasted_iota(jnp.int32, (sb, sb), 1)
    same = (r // c) == (cc // c)
    tri = jnp.where(same, jnp.where(cc <= r, 1.0, 0.0), 0.0).astype(BF16)
    blk = jnp.where(same, 1.0, 0.0).astype(BF16)
    tr = lax.broadcasted_iota(jnp.int32, (c, c), 0)
    tc = lax.broadcasted_iota(jnp.int32, (c, c), 1)
    causal = tc <= tr

    def prepare(bb):
        f = lb + (1.0 - lb) * jax.nn.sigmoid(hf_ref[bb].astype(F32))
        kk = 1.0 - f
        parts = _split3(jnp.log(f))
        bcum = sum(jnp.dot(tri, p, preferred_element_type=F32) for p in parts)
        blast = sum(jnp.dot(blk, p, preferred_element_type=F32) for p in parts)
        hq = hq_ref[bb].astype(F32)
        q_dec = (hq * jax.nn.sigmoid(hq) * jnp.exp(bcum)).astype(BF16)
        k_inv = (kk * jnp.exp(-bcum)).astype(BF16)
        k_end = (kk * jnp.exp(blast - bcum)).astype(BF16)
        vv = hi_ref[bb]
        v_t = jnp.transpose(vv.astype(F32)).astype(BF16)
        return q_dec, k_inv, k_end, jnp.exp(blast), vv, v_t

    rows_of_batch = [prepare(bb) for bb in range(n_batch)]
    for ci in range(sb // c):
        rows = slice(ci * c, (ci + 1) * c)
        for bb, (q_dec, k_inv, k_end, decay, vv, v_t) in enumerate(rows_of_batch):
            outs = []
            for h in range(HGRN_HEADS):
                lanes = slice(h * HGRN_DK, (h + 1) * HGRN_DK)
                qd = q_dec[rows, lanes]
                a = jnp.where(causal, _nt_dot(qd, k_inv[rows, lanes]), 0.0)
                st = st_ref[bb, h]
                o = jnp.dot(a.astype(BF16), vv[rows, lanes], preferred_element_type=F32)
                o = o + _nt_dot(qd, st.astype(BF16))
                ds = jnp.dot(v_t[lanes, rows], k_end[rows, lanes], preferred_element_type=F32)
                st_ref[bb, h] = st * decay[ci * c:ci * c + 1, lanes] + ds
                outs.append(o)
            o_scr[bb, rows, :] = jnp.concatenate(outs, axis=-1)

    for bb in range(n_batch):
        hg = hg_ref[bb].astype(F32)
        gate = hg * jax.nn.sigmoid(hg)
        ys = []
        for h in range(HGRN_HEADS):
            lanes = slice(h * HGRN_DV, (h + 1) * HGRN_DV)
            ys.append(_rms(o_scr[bb, :, lanes], gn_ref[...]) * gate[:, lanes])
        o_ref[bb] = jnp.concatenate(ys, axis=-1).astype(o_ref.dtype)


def _hgrn(hq, hf, hi, hg, lb_logits, gn, b, s):
    sb = min(HGRN_TOKENS, s)
    nb = HGRN_BATCH_ROWS if b % HGRN_BATCH_ROWS == 0 else 1
    spec = pl.BlockSpec((nb, sb, HGRN_WIDTH), lambda i, j: (i, j, 0))
    r3 = lambda a: a.reshape(b, s, HGRN_WIDTH)
    out = pl.pallas_call(
        _hgrn_kernel,
        grid=(b // nb, s // sb),
        in_specs=[pl.BlockSpec(lb_logits.shape, lambda i, j: (0, 0)),
                  pl.BlockSpec((1, HGRN_DV), lambda i, j: (0, 0)),
                  spec, spec, spec, spec],
        out_specs=spec,
        out_shape=jax.ShapeDtypeStruct((b, s, HGRN_WIDTH), BF16),
        scratch_shapes=[pltpu.VMEM((nb, HGRN_HEADS, HGRN_DV, HGRN_DK), F32),
                        pltpu.VMEM((nb, sb, HGRN_WIDTH), F32)],
        compiler_params=_params(("parallel", "arbitrary")),
        name="hgrn",
    )(lb_logits, gn, r3(hq), r3(hf), r3(hi), r3(hg))
    return out.reshape(b * s, HGRN_WIDTH)


def _merge_kernel(x_ref, ya_ref, yh_ref, ga_ref, gh_ref, wa_ref, wh_ref, wo_ref, g_ref,
                  x1_ref, hnt_ref):
    pa = jnp.dot(ya_ref[...], wa_ref[...], preferred_element_type=F32)
    ph = jnp.dot(yh_ref[...], wh_ref[...], preferred_element_type=F32)
    merged = (jax.nn.sigmoid(ga_ref[...].astype(F32)) * pa
              + jax.nn.sigmoid(gh_ref[...].astype(F32)) * ph)
    x1 = x_ref[...] + jnp.dot(merged.astype(BF16), wo_ref[...], preferred_element_type=F32)
    x1_ref[...] = x1
    hnt_ref[...] = jnp.transpose(_rms(x1, g_ref[...])).astype(BF16)


def _merge(x2, ya, yh, ga, gh, wa, wh, wo, g):
    t = x2.shape[0]
    tm = min(MERGE_TOKENS, t)
    row = lambda w: pl.BlockSpec((tm, w), lambda i: (i, 0))
    full = lambda a: pl.BlockSpec(a.shape, lambda i: (0, 0))
    return pl.pallas_call(
        _merge_kernel,
        grid=(t // tm,),
        in_specs=[row(D_MODEL), row(ATTN_WIDTH), row(HGRN_WIDTH), row(D_MODEL), row(D_MODEL),
                  full(wa), full(wh), full(wo), full(g)],
        out_specs=(row(D_MODEL), pl.BlockSpec((D_MODEL, tm), lambda i: (0, i))),
        out_shape=(jax.ShapeDtypeStruct((t, D_MODEL), F32),
                   jax.ShapeDtypeStruct((D_MODEL, t), BF16)),
        compiler_params=_params(("parallel",)),
        name="merge",
    )(x2, ya, yh, ga, gh, wa, wh, wo, g)


def _as_f32(i):
    return jnp.asarray(i, jnp.int32).astype(F32)


def _topk_exact(s, k):
    n_rows, lanes = s.shape
    row = lax.broadcasted_iota(jnp.int32, (n_rows, lanes), 0).astype(F32)
    kid = lax.broadcasted_iota(jnp.int32, (k, lanes), 0).astype(F32)

    def body(a, carry):
        s, rank, vals = carry
        af = _as_f32(a)
        m = jnp.max(s, axis=0, keepdims=True)
        first = jnp.min(jnp.where(s == m, row, float(n_rows)), axis=0, keepdims=True)
        sel = row == first
        return jnp.where(sel, -jnp.inf, s), jnp.where(sel, af, rank), jnp.where(kid == af, m, vals)

    init = (s, jnp.full((n_rows, lanes), float(k), F32), jnp.zeros((k, lanes), F32))
    _, rank, vals = lax.fori_loop(0, k, body, init)
    return rank, vals


RANK_CODE_SCALE = 2.0 ** 123
RANK_CODE_LIMIT = -float(PEER_TOPK) * RANK_CODE_SCALE


def _topk_quick_pair(s1, s2, k):
    c1, c2, v1, v2 = s1, s2, [], []
    for a in range(k):
        code = -float(a + k) * RANK_CODE_SCALE
        m1 = jnp.max(c1, axis=0, keepdims=True)
        m2 = jnp.max(c2, axis=0, keepdims=True)
        c1 = jnp.where(c1 == m1, code, c1)
        c2 = jnp.where(c2 == m2, code, c2)
        v1.append(m1)
        v2.append(m2)
    v1 = jnp.concatenate(v1, axis=0)
    v2 = jnp.concatenate(v2, axis=0)

    def decode(c, s):
        marked = c <= RANK_CODE_LIMIT
        rank = jnp.where(marked, c * (-1.0 / RANK_CODE_SCALE) - float(k), float(k))
        n_marked = jnp.sum(jnp.where(marked, 1.0, 0.0), axis=0, keepdims=True)
        in_range = jnp.min(s, axis=0, keepdims=True) > RANK_CODE_LIMIT
        return rank, jnp.where(n_marked == float(k), jnp.where(in_range, 1.0, 0.0), 0.0)

    rank1, ok1 = decode(c1, s1)
    rank2, ok2 = decode(c2, s2)
    return rank1, v1, rank2, v2, ok1 * ok2


def _cand_row_lengths(k):
    return [k // (a + 1) for a in range(k)]


def _candidates(v1, v2, k):
    lengths = _cand_row_lengths(k)
    rows = [v1[a:a + 1] + v2[:n] for a, n in enumerate(lengths)]
    pad = -sum(lengths) % SUBLANES
    if pad:
        rows.append(jnp.full((pad, v1.shape[1]), -jnp.inf, F32))
    return jnp.concatenate(rows, axis=0)


def _choose_exact(cand, k):
    n_cand, lanes = cand.shape
    crow = lax.broadcasted_iota(jnp.int32, (n_cand, lanes), 0).astype(F32)

    def body(_, c):
        m = jnp.max(c, axis=0, keepdims=True)
        first = jnp.min(jnp.where(c == m, crow, float(n_cand)), axis=0, keepdims=True)
        return jnp.where(crow == first, -jnp.inf, c)

    return lax.fori_loop(0, k, body, cand)


def _choose_quick(cand, k):
    for _ in range(k):
        cand = jnp.where(cand == jnp.max(cand, axis=0, keepdims=True), -jnp.inf, cand)
    return cand


def _pack_rows(x):
    return pltpu.bitcast(x, jnp.uint32)


def _unpack_rows(x):
    return pltpu.bitcast(x, BF16)


def _route_outputs(s1, s2, rank1, v1, rank2, v2, cand, left, k):
    chosen = jnp.where(left == -jnp.inf, jnp.where(cand > -jnp.inf, 1.0, 0.0), 0.0)
    top = v1[0:1] + v2[0:1]
    z = jnp.sum(chosen * jnp.exp(cand - top), axis=0, keepdims=True)
    lengths = _cand_row_lengths(k)
    offsets = [sum(lengths[:a]) for a in range(k)]
    counts = [jnp.sum(chosen[offsets[a]:offsets[a] + lengths[a]], axis=0, keepdims=True) for a in range(k)]
    nonempty = sum(counts[a] if lengths[a] == 1 else jnp.minimum(counts[a], 1.0) for a in range(k))
    n1 = jnp.where(rank1 < nonempty, 1.0, 0.0)
    for a in range(k):
        if lengths[a] > 1:
            n1 = jnp.where(rank1 == float(a), counts[a], n1)
    e1 = jnp.exp(s1 - v1[0:1])
    r2 = _pack_rows(rank2.astype(BF16))
    e2 = _pack_rows((jnp.exp(s2 - v2[0:1]) / z).astype(BF16))
    return (n1, e1, r2, e2), jnp.sum(chosen, axis=0, keepdims=True)


def _route_kernel(hnt_ref, wq_ref, k1_ref, k2_ref, n1_ref, e1_ref, r2_ref, e2_ref):
    k = PEER_TOPK
    out_refs = (n1_ref, e1_ref, r2_ref, e2_ref)
    q_t = jnp.dot(wq_ref[...], hnt_ref[...], preferred_element_type=F32).astype(BF16)
    s1 = jnp.dot(k1_ref[0], q_t[:PEER_KEY_DIM], preferred_element_type=F32)
    s2 = jnp.dot(k2_ref[0], q_t[PEER_KEY_DIM:], preferred_element_type=F32)

    def store(outs):
        for ref, val in zip(out_refs, outs):
            ref[0] = val

    tiles = [_topk_quick_pair(s1[:, lt * LANE_TILE:(lt + 1) * LANE_TILE],
                              s2[:, lt * LANE_TILE:(lt + 1) * LANE_TILE], k)
             for lt in range(s1.shape[1] // LANE_TILE)]
    rank1, v1, rank2, v2, ok = (jnp.concatenate(parts, axis=1) for parts in zip(*tiles))
    cand = _candidates(v1, v2, k)
    outs, n_chosen = _route_outputs(s1, s2, rank1, v1, rank2, v2, cand, _choose_quick(cand, k), k)
    store(outs)
    ok = ok * jnp.where(n_chosen == float(k), 1.0, 0.0)

    @pl.when(jnp.min(ok) < 1.0)
    def _():
        rank1, v1 = _topk_exact(s1, k)
        rank2, v2 = _topk_exact(s2, k)
        cand = _candidates(v1, v2, k)
        outs, _ = _route_outputs(s1, s2, rank1, v1, rank2, v2, cand, _choose_exact(cand, k), k)
        store(outs)


def _route(hnt, wq_t, keys1, keys2):
    t = hnt.shape[1]
    tb = min(ROUTE_TOKENS, t)
    kd2 = 2 * PEER_KEY_DIM
    out = jax.ShapeDtypeStruct((PEER_HEADS, PEER_N_KEYS, t), F32)
    out16 = jax.ShapeDtypeStruct((PEER_HEADS, PEER_N_KEYS // 2, t), jnp.uint32)
    pspec = pl.BlockSpec((1, PEER_N_KEYS // 2, tb), lambda i, h: (h, 0, i))
    ospec = pl.BlockSpec((1, PEER_N_KEYS, tb), lambda i, h: (h, 0, i))
    kspec = pl.BlockSpec((1, PEER_N_KEYS, PEER_KEY_DIM), lambda i, h: (h, 0, 0))
    return pl.pallas_call(
        _route_kernel,
        grid=(t // tb, PEER_HEADS),
        in_specs=[pl.BlockSpec((D_MODEL, tb), lambda i, h: (0, i)),
                  pl.BlockSpec((kd2, D_MODEL), lambda i, h: (h, 0)),
                  kspec, kspec],
        out_specs=(ospec, ospec, pspec, pspec),
        out_shape=(out, out, out16, out16),
        compiler_params=_params(("parallel", "arbitrary")),
        name="route",
    )(hnt, wq_t, keys1, keys2)


def _peer_gates(n1_ref, e1_ref, r2_ref, e2_ref, firsts, act, d_ref, first0):
    nk = PEER_N_KEYS
    for lt in range(d_ref.shape[1] // LANE_TILE):
        lanes = slice(lt * LANE_TILE, (lt + 1) * LANE_TILE)
        ws = [None] * len(firsts)
        for h in range(PEER_HEADS):
            keys = slice(h * nk // 2, (h + 1) * nk // 2)
            r2 = _unpack_rows(r2_ref[keys, lanes])
            e2 = _unpack_rows(e2_ref[keys, lanes])
            for k, ii in enumerate(firsts):
                n1 = n1_ref[h, ii:ii + 1, lanes].astype(BF16)
                e1 = e1_ref[h, ii:ii + 1, lanes].astype(BF16)
                term = jnp.where(r2 < n1, e2, 0.0) * e1
                ws[k] = term if ws[k] is None else ws[k] + term
        for k, ii in enumerate(firsts):
            a = act[k * nk:(k + 1) * nk, lanes].astype(BF16)
            gelu = (0.5 * a) * (1.0 + lax.erf(a * (2.0 ** -0.5)))
            d_ref[(ii - first0) * nk // 2:(ii - first0 + 1) * nk // 2, lanes] = _pack_rows(ws[k] * gelu)


def _peer_kernel(hnt_ref, u_ref, vt_ref, n1_ref, e1_ref, r2_ref, e2_ref, x1_ref, g_ref,
                 o_ref, acc_ref, *d_refs):
    c = pl.program_id(1)
    nk = PEER_N_KEYS
    sub = PEER_SUB_CHUNK
    piece = sub // PEER_MXU_SPLIT
    n_sub = u_ref.shape[0] // sub

    @pl.when(c == 0)
    def _():
        acc_ref[...] = jnp.zeros_like(acc_ref)

    acts = {}

    def first(s):
        for p in range(PEER_MXU_SPLIT):
            rows = slice(s * sub + p * piece, s * sub + (p + 1) * piece)
            acts[s, p] = rows, jnp.dot(u_ref[rows, :], hnt_ref[...], preferred_element_type=F32)

    def gate_and_second(s):
        for p in range(PEER_MXU_SPLIT):
            rows, act = acts.pop((s, p))
            firsts = range(rows.start // nk, rows.stop // nk)
            _peer_gates(n1_ref, e1_ref, r2_ref, e2_ref, firsts, act, d_refs[s], s * sub // nk)
        d_t = _unpack_rows(d_refs[s][...])
        acc_ref[...] += jnp.dot(vt_ref[:, s * sub:(s + 1) * sub], d_t, preferred_element_type=F32)

    first(0)
    for s in range(n_sub):
        if s + 1 < n_sub:
            first(s + 1)
        gate_and_second(s)

    @pl.when(c == pl.num_programs(1) - 1)
    def _():
        y = x1_ref[...] + jnp.transpose(acc_ref[...])
        o_ref[...] = _rms(y, g_ref[...])


def _peer(hnt, u, v_t, n1, e1, r2, e2, x1, g):
    t = x1.shape[0]
    tb = min(PEER_TOKENS, t)
    ec = PEER_EXPERT_CHUNK
    packed_rows = PEER_HEADS * PEER_N_KEYS // 2
    dense = pl.BlockSpec((packed_rows, tb), lambda i, c: (0, i))
    r2 = r2.reshape(packed_rows, t)
    e2 = e2.reshape(packed_rows, t)
    rows = pl.BlockSpec((PEER_HEADS, ec // PEER_N_KEYS, tb), lambda i, c: (0, c, i))
    return pl.pallas_call(
        _peer_kernel,
        grid=(t // tb, PEER_N_EXPERTS // ec),
        in_specs=[pl.BlockSpec((D_MODEL, tb), lambda i, c: (0, i)),
                  pl.BlockSpec((ec, D_MODEL), lambda i, c: (c, 0)),
                  pl.BlockSpec((D_MODEL, ec), lambda i, c: (0, c)),
                  rows, rows, dense, dense,
                  pl.BlockSpec((tb, D_MODEL), lambda i, c: (i, 0)),
                  pl.BlockSpec((1, D_MODEL), lambda i, c: (0, 0))],
        out_specs=pl.BlockSpec((tb, D_MODEL), lambda i, c: (i, 0)),
        out_shape=jax.ShapeDtypeStruct((t, D_MODEL), F32),
        scratch_shapes=([pltpu.VMEM((D_MODEL, tb), F32)]
                        + [pltpu.VMEM((PEER_SUB_CHUNK // 2, tb), jnp.uint32)] * (ec // PEER_SUB_CHUNK)),
        compiler_params=_params(("parallel", "arbitrary")),
        name="peer",
    )(hnt, u, v_t, n1, e1, r2, e2, x1, g)


def kernel(x, norm_mix_g, w_in, attn_sinks, hgrn_lb_logits, hgrn_norm_g, w_attn_proj, w_hgrn_proj,
           w_out, norm_ffn_g, w_peer_q, peer_keys, peer_u, peer_v, norm_final_g):
    b, s, d = x.shape
    assert d == D_MODEL and norm_mix_g.shape[0] == 1
    t = b * s
    x2 = x.reshape(t, d)
    row = lambda a: a.reshape(1, -1).astype(F32)

    aq, ak, av, hq, hf, hi, hg, ga, gh = _inproj(x2, row(norm_mix_g[0]), w_in[0].astype(BF16))
    y_attn = _attention(aq, ak, av, attn_sinks[0].astype(F32), b, s)
    y_hgrn = _hgrn(hq, hf, hi, hg, hgrn_lb_logits.astype(F32), row(hgrn_norm_g[0]), b, s)
    x1, hnt = _merge(x2, y_attn, y_hgrn, ga, gh, w_attn_proj[0].astype(BF16),
                     w_hgrn_proj[0].astype(BF16), w_out[0].astype(BF16), row(norm_ffn_g[0]))
    wq_t = jnp.transpose(w_peer_q[0]).astype(BF16)
    keys = peer_keys[0].astype(BF16)
    n1, e1, r2, e2 = _route(hnt, wq_t, keys[0], keys[1])
    out = _peer(hnt, peer_u[0].astype(BF16), jnp.transpose(peer_v[0]).astype(BF16),
                n1, e1, r2, e2, x1, row(norm_final_g))
    return out.reshape(b, s, d)
```

```python
import jax
import jax.numpy as jnp
from jax import lax
from jax.experimental import pallas as pl
from jax.experimental.pallas import tpu as pltpu

F32 = jnp.float32
BF16 = jnp.bfloat16

D_MODEL = 1024
ATTN_HEADS = 8
ATTN_KV_HEADS = 2
ATTN_GROUP = ATTN_HEADS // ATTN_KV_HEADS
ATTN_HEAD_DIM = 64
ATTN_WIDTH = ATTN_HEADS * ATTN_HEAD_DIM
KV_WIDTH = ATTN_KV_HEADS * ATTN_HEAD_DIM
WINDOW = 128
ATTN_BLOCK = 128
HGRN_HEADS = 4
HGRN_DK = 128
HGRN_DV = 128
HGRN_WIDTH = HGRN_HEADS * HGRN_DK
HGRN_CHUNK = 32
PEER_HEADS = 8
PEER_N_KEYS = 128
PEER_N_EXPERTS = PEER_N_KEYS * PEER_N_KEYS
PEER_KEY_DIM = 128
PEER_TOPK = 16
EPS = 1e-6
MASK_VALUE = -1e30

SPLIT_SIZES = (ATTN_WIDTH, KV_WIDTH, KV_WIDTH, HGRN_WIDTH, HGRN_WIDTH,
               HGRN_WIDTH, HGRN_WIDTH, D_MODEL, D_MODEL)
IN_WIDTH = sum(SPLIT_SIZES)

VMEM_LIMIT_BYTES = 56 * 1024 * 1024

INPROJ_TOKENS = 512
HGRN_TOKENS = 256
HGRN_BATCH_ROWS = 2
MERGE_TOKENS = 512
MERGE_ROW_PARTS = 2
ROUTE_TOKENS = 512
PEER_TOKENS = 512
PEER_EXPERT_CHUNK = 2048
PEER_SUB_CHUNK = 1024
LANE_TILE = 128
SUBLANES = 8
PEER_MXU_SPLIT = 4


def _params(semantics):
    return pltpu.CompilerParams(dimension_semantics=semantics,
                                vmem_limit_bytes=VMEM_LIMIT_BYTES)


def _nt_dot(a, b):
    return lax.dot_general(a, b, (((1,), (1,)), ((), ())), preferred_element_type=F32)


def _rms(x, g):
    ms = jnp.mean(x * x, axis=-1, keepdims=True)
    return x * lax.rsqrt(ms + EPS) * g


def _inproj_kernel(x_ref, g_ref, w_ref, *out_refs):
    h = _rms(x_ref[...], g_ref[...]).astype(BF16)
    off = 0
    for o_ref, width in zip(out_refs, SPLIT_SIZES):
        o_ref[...] = jnp.dot(h, w_ref[:, off:off + width],
                             preferred_element_type=F32).astype(o_ref.dtype)
        off += width


def _inproj(x2, g, w_in):
    t = x2.shape[0]
    tm = min(INPROJ_TOKENS, t)
    out_shape = tuple(jax.ShapeDtypeStruct((t, w), BF16) for w in SPLIT_SIZES)
    return pl.pallas_call(
        _inproj_kernel,
        grid=(t // tm,),
        in_specs=[pl.BlockSpec((tm, D_MODEL), lambda i: (i, 0)),
                  pl.BlockSpec((1, D_MODEL), lambda i: (0, 0)),
                  pl.BlockSpec((D_MODEL, IN_WIDTH), lambda i: (0, 0))],
        out_specs=tuple(pl.BlockSpec((tm, w), lambda i: (i, 0)) for w in SPLIT_SIZES),
        out_shape=out_shape,
        compiler_params=_params(("parallel",)),
        name="inproj",
    )(x2, g, w_in)


def _alibi_slopes():
    return [float(2.0 ** (-8.0 * h / ATTN_HEADS)) for h in range(1, ATTN_HEADS + 1)]


def _attn_kernel(sink_ref, q_ref, kp_ref, kc_ref, vp_ref, vc_ref, o_ref):
    n = pl.program_id(1)
    w = ATTN_BLOCK
    q = q_ref[0]
    k = jnp.concatenate([kp_ref[0], kc_ref[0]], axis=0)
    v = jnp.concatenate([vp_ref[0], vc_ref[0]], axis=0)
    qpos = lax.broadcasted_iota(jnp.int32, (w, 2 * w), 0) + w
    kpos = lax.broadcasted_iota(jnp.int32, (w, 2 * w), 1)
    dist = qpos - kpos
    first_ok = jnp.where(n > 0, 0, w)
    in_window = jnp.where(dist >= 0, jnp.where(dist < WINDOW, 1, 0), 0)
    valid = jnp.where(kpos >= first_ok, in_window, 0) == 1
    distf = dist.astype(F32)
    scale = ATTN_HEAD_DIM ** -0.5
    slopes = _alibi_slopes()
    outs = []
    scores = []
    for kh in range(ATTN_KV_HEADS):
        ksl = k[:, kh * ATTN_HEAD_DIM:(kh + 1) * ATTN_HEAD_DIM]
        qg = jnp.concatenate(
            [q[:, (kh * ATTN_GROUP + g) * ATTN_HEAD_DIM:(kh * ATTN_GROUP + g + 1) * ATTN_HEAD_DIM]
             for g in range(ATTN_GROUP)], axis=0)
        scores.append(_nt_dot(qg, ksl) * scale)
    for kh in range(ATTN_KV_HEADS):
        vsl = v[:, kh * ATTN_HEAD_DIM:(kh + 1) * ATTN_HEAD_DIM]
        s_all = scores[kh]
        ps, denoms = [], []
        for g in range(ATTN_GROUP):
            head = kh * ATTN_GROUP + g
            s = s_all[g * w:(g + 1) * w] - slopes[head] * distf
            s = jnp.where(valid, s, MASK_VALUE)
            sink = sink_ref[head]
            m = jnp.maximum(jnp.max(s, axis=-1, keepdims=True), sink)
            p = jnp.exp(s - m)
            denoms.append(jnp.sum(p, axis=-1, keepdims=True) + jnp.exp(sink - m))
            ps.append(p.astype(BF16))
        o_all = jnp.dot(jnp.concatenate(ps, axis=0), vsl, preferred_element_type=F32)
        for g in range(ATTN_GROUP):
            outs.append(o_all[g * w:(g + 1) * w] / denoms[g])
    o_ref[0] = jnp.concatenate(outs, axis=-1).astype(o_ref.dtype)


def _attention(aq, ak, av, sinks, b, s):
    nb = s // ATTN_BLOCK
    q3 = aq.reshape(b, s, ATTN_WIDTH)
    k3 = ak.reshape(b, s, KV_WIDTH)
    v3 = av.reshape(b, s, KV_WIDTH)
    cur = lambda i, j: (i, j, 0)
    prev = lambda i, j: (i, jnp.maximum(j - 1, 0), 0)
    out = pl.pallas_call(
        _attn_kernel,
        grid=(b, nb),
        in_specs=[pl.BlockSpec(memory_space=pltpu.SMEM),
                  pl.BlockSpec((1, ATTN_BLOCK, ATTN_WIDTH), cur),
                  pl.BlockSpec((1, ATTN_BLOCK, KV_WIDTH), prev),
                  pl.BlockSpec((1, ATTN_BLOCK, KV_WIDTH), cur),
                  pl.BlockSpec((1, ATTN_BLOCK, KV_WIDTH), prev),
                  pl.BlockSpec((1, ATTN_BLOCK, KV_WIDTH), cur)],
        out_specs=pl.BlockSpec((1, ATTN_BLOCK, ATTN_WIDTH), cur),
        out_shape=jax.ShapeDtypeStruct((b, s, ATTN_WIDTH), BF16),
        compiler_params=_params(("parallel", "parallel")),
        name="attn",
    )(sinks, q3, k3, k3, v3, v3)
    return out.reshape(b * s, ATTN_WIDTH)


def _split3(x):
    hi = x.astype(BF16)
    r1 = x - hi.astype(F32)
    mid = r1.astype(BF16)
    lo = (r1 - mid.astype(F32)).astype(BF16)
    return hi, mid, lo


def _hgrn_kernel(lb_ref, gn_ref, hq_ref, hf_ref, hi_ref, hg_ref, o_ref, st_ref, o_scr, ds_scr):
    n_batch, sb = hq_ref.shape[0], hq_ref.shape[1]
    c = HGRN_CHUNK

    @pl.when(pl.program_id(1) == 0)
    def _():
        st_ref[...] = jnp.zeros_like(st_ref)

    logits = lb_ref[...]
    e = jnp.exp(logits - jnp.max(logits, axis=0, keepdims=True))
    lb = e[0:1] / jnp.sum(e, axis=0, keepdims=True)

    r = lax.broadcasted_iota(jnp.int32, (sb, sb), 0)
    cc = lax.broadcasted_iota(jnp.int32, (sb, sb), 1)
    same = (r // c) == (cc // c)
    tri = jnp.where(same, jnp.where(cc <= r, 1.0, 0.0), 0.0).astype(BF16)
    blk = jnp.where(same, 1.0, 0.0).astype(BF16)
    tr = lax.broadcasted_iota(jnp.int32, (c, c), 0)
    tc = lax.broadcasted_iota(jnp.int32, (c, c), 1)
    causal = tc <= tr

    def prepare(bb):
        f = lb + (1.0 - lb) * jax.nn.sigmoid(hf_ref[bb].astype(F32))
        kk = 1.0 - f
        parts = _split3(jnp.log(f))
        bcum = sum(jnp.dot(tri, p, preferred_element_type=F32) for p in parts)
        blast = sum(jnp.dot(blk, p, preferred_element_type=F32) for p in parts)
        hq = hq_ref[bb].astype(F32)
        q_dec = (hq * jax.nn.sigmoid(hq) * jnp.exp(bcum)).astype(BF16)
        k_inv = (kk * jnp.exp(-bcum)).astype(BF16)
        k_end = (kk * jnp.exp(blast - bcum)).astype(BF16)
        vv = hi_ref[bb]
        v_t = jnp.transpose(vv.astype(F32)).astype(BF16)
        return q_dec, k_inv, k_end, jnp.exp(blast), vv, v_t

    rows_of_batch = [prepare(bb) for bb in range(n_batch)]
    n_chunks = sb // c
    units = [(ci, bb, h) for ci in range(n_chunks) for bb in range(n_batch) for h in range(HGRN_HEADS)]
    rows = lambda ci: slice(ci * c, (ci + 1) * c)
    lanes = lambda h: slice(h * HGRN_DK, (h + 1) * HGRN_DK)

    intra = {}
    for ci, bb, h in units:
        q_dec, k_inv = rows_of_batch[bb][0], rows_of_batch[bb][1]
        a = _nt_dot(q_dec[rows(ci), lanes(h)], k_inv[rows(ci), lanes(h)])
        intra[ci, bb, h] = jnp.where(causal, a, 0.0).astype(BF16)
    for ci, bb, h in units:
        k_end, v_t = rows_of_batch[bb][2], rows_of_batch[bb][5]
        ds_scr[ci, bb, h] = jnp.dot(v_t[lanes(h), rows(ci)], k_end[rows(ci), lanes(h)],
                                    preferred_element_type=F32)
    for ci, bb, h in units:
        vv = rows_of_batch[bb][4]
        o_scr[bb, rows(ci), lanes(h)] = jnp.dot(intra.pop((ci, bb, h)), vv[rows(ci), lanes(h)],
                                                preferred_element_type=F32)

    for ci in range(n_chunks):
        states = {}
        for bb in range(n_batch):
            q_dec = rows_of_batch[bb][0]
            for h in range(HGRN_HEADS):
                states[bb, h] = st = st_ref[bb, h]
                o_scr[bb, rows(ci), lanes(h)] += _nt_dot(q_dec[rows(ci), lanes(h)], st.astype(BF16))
        for bb in range(n_batch):
            decay = rows_of_batch[bb][3]
            for h in range(HGRN_HEADS):
                st_ref[bb, h] = states[bb, h] * decay[ci * c:ci * c + 1, lanes(h)] + ds_scr[ci, bb, h]

    for bb in range(n_batch):
        hg = hg_ref[bb].astype(F32)
        gate = hg * jax.nn.sigmoid(hg)
        ys = []
        for h in range(HGRN_HEADS):
            lanes = slice(h * HGRN_DV, (h + 1) * HGRN_DV)
            ys.append(_rms(o_scr[bb, :, lanes], gn_ref[...]) * gate[:, lanes])
        o_ref[bb] = jnp.concatenate(ys, axis=-1).astype(o_ref.dtype)


def _hgrn(hq, hf, hi, hg, lb_logits, gn, b, s):
    sb = min(HGRN_TOKENS, s)
    nb = HGRN_BATCH_ROWS if b % HGRN_BATCH_ROWS == 0 else 1
    spec = pl.BlockSpec((nb, sb, HGRN_WIDTH), lambda i, j: (i, j, 0))
    r3 = lambda a: a.reshape(b, s, HGRN_WIDTH)
    out = pl.pallas_call(
        _hgrn_kernel,
        grid=(b // nb, s // sb),
        in_specs=[pl.BlockSpec(lb_logits.shape, lambda i, j: (0, 0)),
                  pl.BlockSpec((1, HGRN_DV), lambda i, j: (0, 0)),
                  spec, spec, spec, spec],
        out_specs=spec,
        out_shape=jax.ShapeDtypeStruct((b, s, HGRN_WIDTH), BF16),
        scratch_shapes=[pltpu.VMEM((nb, HGRN_HEADS, HGRN_DV, HGRN_DK), F32),
                        pltpu.VMEM((nb, sb, HGRN_WIDTH), F32),
                        pltpu.VMEM((sb // HGRN_CHUNK, nb, HGRN_HEADS, HGRN_DV, HGRN_DK), F32)],
        compiler_params=_params(("parallel", "arbitrary")),
        name="hgrn",
    )(lb_logits, gn, r3(hq), r3(hf), r3(hi), r3(hg))
    return out.reshape(b * s, HGRN_WIDTH)


def _merge_kernel(x_ref, ya_ref, yh_ref, ga_ref, gh_ref, wa_ref, wh_ref, wo_ref, g_ref,
                  x1_ref, hnt_ref):
    tm = x_ref.shape[0]
    parts = MERGE_ROW_PARTS
    rows = [slice(p * tm // parts, (p + 1) * tm // parts) for p in range(parts)]
    proj = [(jnp.dot(ya_ref[r, :], wa_ref[...], preferred_element_type=F32),
             jnp.dot(yh_ref[r, :], wh_ref[...], preferred_element_type=F32)) for r in rows]
    for r, (pa, ph) in zip(rows, proj):
        merged = (jax.nn.sigmoid(ga_ref[r, :].astype(F32)) * pa
                  + jax.nn.sigmoid(gh_ref[r, :].astype(F32)) * ph)
        x1 = x_ref[r, :] + jnp.dot(merged.astype(BF16), wo_ref[...], preferred_element_type=F32)
        x1_ref[r, :] = x1
        hnt_ref[:, r] = jnp.transpose(_rms(x1, g_ref[...])).astype(BF16)


def _merge(x2, ya, yh, ga, gh, wa, wh, wo, g):
    t = x2.shape[0]
    tm = min(MERGE_TOKENS, t)
    row = lambda w: pl.BlockSpec((tm, w), lambda i: (i, 0))
    full = lambda a: pl.BlockSpec(a.shape, lambda i: (0, 0))
    return pl.pallas_call(
        _merge_kernel,
        grid=(t // tm,),
        in_specs=[row(D_MODEL), row(ATTN_WIDTH), row(HGRN_WIDTH), row(D_MODEL), row(D_MODEL),
                  full(wa), full(wh), full(wo), full(g)],
        out_specs=(row(D_MODEL), pl.BlockSpec((D_MODEL, tm), lambda i: (0, i))),
        out_shape=(jax.ShapeDtypeStruct((t, D_MODEL), F32),
                   jax.ShapeDtypeStruct((D_MODEL, t), BF16)),
        compiler_params=_params(("parallel",)),
        name="merge",
    )(x2, ya, yh, ga, gh, wa, wh, wo, g)


def _as_f32(i):
    return jnp.asarray(i, jnp.int32).astype(F32)


def _topk_exact(s, k):
    n_rows, lanes = s.shape
    row = lax.broadcasted_iota(jnp.int32, (n_rows, lanes), 0).astype(F32)
    kid = lax.broadcasted_iota(jnp.int32, (k, lanes), 0).astype(F32)

    def body(a, carry):
        s, rank, vals = carry
        af = _as_f32(a)
        m = jnp.max(s, axis=0, keepdims=True)
        first = jnp.min(jnp.where(s == m, row, float(n_rows)), axis=0, keepdims=True)
        sel = row == first
        return jnp.where(sel, -jnp.inf, s), jnp.where(sel, af, rank), jnp.where(kid == af, m, vals)

    init = (s, jnp.full((n_rows, lanes), float(k), F32), jnp.zeros((k, lanes), F32))
    _, rank, vals = lax.fori_loop(0, k, body, init)
    return rank, vals


RANK_CODE_SCALE = 2.0 ** 123
RANK_CODE_LIMIT = -float(PEER_TOPK) * RANK_CODE_SCALE


def _topk_quick_pair(s1, s2, k):
    c1, c2, v1, v2 = s1, s2, [], []
    for a in range(k):
        code = -float(a + k) * RANK_CODE_SCALE
        m1 = jnp.max(c1, axis=0, keepdims=True)
        m2 = jnp.max(c2, axis=0, keepdims=True)
        c1 = jnp.where(c1 == m1, code, c1)
        c2 = jnp.where(c2 == m2, code, c2)
        v1.append(m1)
        v2.append(m2)
    v1 = jnp.concatenate(v1, axis=0)
    v2 = jnp.concatenate(v2, axis=0)

    def decode(c, s):
        marked = c <= RANK_CODE_LIMIT
        rank = jnp.where(marked, c * (-1.0 / RANK_CODE_SCALE) - float(k), float(k))
        n_marked = jnp.sum(jnp.where(marked, 1.0, 0.0), axis=0, keepdims=True)
        in_range = jnp.min(s, axis=0, keepdims=True) > RANK_CODE_LIMIT
        return rank, jnp.where(n_marked == float(k), jnp.where(in_range, 1.0, 0.0), 0.0)

    rank1, ok1 = decode(c1, s1)
    rank2, ok2 = decode(c2, s2)
    return rank1, v1, rank2, v2, ok1 * ok2


def _cand_row_lengths(k):
    return [k // (a + 1) for a in range(k)]


def _candidates(v1, v2, k):
    lengths = _cand_row_lengths(k)
    rows = [v1[a:a + 1] + v2[:n] for a, n in enumerate(lengths)]
    pad = -sum(lengths) % SUBLANES
    if pad:
        rows.append(jnp.full((pad, v1.shape[1]), -jnp.inf, F32))
    return jnp.concatenate(rows, axis=0)


def _choose_exact(cand, k):
    n_cand, lanes = cand.shape
    crow = lax.broadcasted_iota(jnp.int32, (n_cand, lanes), 0).astype(F32)

    def body(_, c):
        m = jnp.max(c, axis=0, keepdims=True)
        first = jnp.min(jnp.where(c == m, crow, float(n_cand)), axis=0, keepdims=True)
        return jnp.where(crow == first, -jnp.inf, c)

    return lax.fori_loop(0, k, body, cand)


def _choose_quick(cand, k):
    for _ in range(k):
        cand = jnp.where(cand == jnp.max(cand, axis=0, keepdims=True), -jnp.inf, cand)
    return cand


def _pack_rows(x):
    return pltpu.bitcast(x, jnp.uint32)


def _unpack_rows(x):
    return pltpu.bitcast(x, BF16)


def _route_outputs(s1, s2, rank1, v1, rank2, v2, cand, left, k):
    chosen = jnp.where(left == -jnp.inf, jnp.where(cand > -jnp.inf, 1.0, 0.0), 0.0)
    top = v1[0:1] + v2[0:1]
    z = jnp.sum(chosen * jnp.exp(cand - top), axis=0, keepdims=True)
    lengths = _cand_row_lengths(k)
    offsets = [sum(lengths[:a]) for a in range(k)]
    counts = [jnp.sum(chosen[offsets[a]:offsets[a] + lengths[a]], axis=0, keepdims=True) for a in range(k)]
    nonempty = sum(counts[a] if lengths[a] == 1 else jnp.minimum(counts[a], 1.0) for a in range(k))
    n1 = jnp.where(rank1 < nonempty, 1.0, 0.0)
    for a in range(k):
        if lengths[a] > 1:
            n1 = jnp.where(rank1 == float(a), counts[a], n1)
    e1 = jnp.exp(s1 - v1[0:1])
    r2 = _pack_rows(rank2.astype(BF16))
    e2 = _pack_rows((jnp.exp(s2 - v2[0:1]) / z).astype(BF16))
    return (n1, e1, r2, e2), jnp.sum(chosen, axis=0, keepdims=True)


def _route_kernel(hnt_ref, wq_ref, k1_ref, k2_ref, n1_ref, e1_ref, r2_ref, e2_ref):
    k = PEER_TOPK
    out_refs = (n1_ref, e1_ref, r2_ref, e2_ref)
    q_t = jnp.dot(wq_ref[...], hnt_ref[...], preferred_element_type=F32).astype(BF16)
    s1 = jnp.dot(k1_ref[0], q_t[:PEER_KEY_DIM], preferred_element_type=F32)
    s2 = jnp.dot(k2_ref[0], q_t[PEER_KEY_DIM:], preferred_element_type=F32)

    def store(outs):
        for ref, val in zip(out_refs, outs):
            ref[0] = val

    tiles = [_topk_quick_pair(s1[:, lt * LANE_TILE:(lt + 1) * LANE_TILE],
                              s2[:, lt * LANE_TILE:(lt + 1) * LANE_TILE], k)
             for lt in range(s1.shape[1] // LANE_TILE)]
    rank1, v1, rank2, v2, ok = (jnp.concatenate(parts, axis=1) for parts in zip(*tiles))
    cand = _candidates(v1, v2, k)
    outs, n_chosen = _route_outputs(s1, s2, rank1, v1, rank2, v2, cand, _choose_quick(cand, k), k)
    store(outs)
    ok = ok * jnp.where(n_chosen == float(k), 1.0, 0.0)

    @pl.when(jnp.min(ok) < 1.0)
    def _():
        rank1, v1 = _topk_exact(s1, k)
        rank2, v2 = _topk_exact(s2, k)
        cand = _candidates(v1, v2, k)
        outs, _ = _route_outputs(s1, s2, rank1, v1, rank2, v2, cand, _choose_exact(cand, k), k)
        store(outs)


def _route(hnt, wq_t, keys1, keys2):
    t = hnt.shape[1]
    tb = min(ROUTE_TOKENS, t)
    kd2 = 2 * PEER_KEY_DIM
    out = jax.ShapeDtypeStruct((PEER_HEADS, PEER_N_KEYS, t), F32)
    out16 = jax.ShapeDtypeStruct((PEER_HEADS, PEER_N_KEYS // 2, t), jnp.uint32)
    pspec = pl.BlockSpec((1, PEER_N_KEYS // 2, tb), lambda i, h: (h, 0, i))
    ospec = pl.BlockSpec((1, PEER_N_KEYS, tb), lambda i, h: (h, 0, i))
    kspec = pl.BlockSpec((1, PEER_N_KEYS, PEER_KEY_DIM), lambda i, h: (h, 0, 0))
    return pl.pallas_call(
        _route_kernel,
        grid=(t // tb, PEER_HEADS),
        in_specs=[pl.BlockSpec((D_MODEL, tb), lambda i, h: (0, i)),
                  pl.BlockSpec((kd2, D_MODEL), lambda i, h: (h, 0)),
                  kspec, kspec],
        out_specs=(ospec, ospec, pspec, pspec),
        out_shape=(out, out, out16, out16),
        compiler_params=_params(("parallel", "arbitrary")),
        name="route",
    )(hnt, wq_t, keys1, keys2)


def _peer_gates(n1_ref, e1_ref, r2_ref, e2_ref, firsts, act, d_ref, first0):
    nk = PEER_N_KEYS
    for lt in range(d_ref.shape[1] // LANE_TILE):
        lanes = slice(lt * LANE_TILE, (lt + 1) * LANE_TILE)
        ws = [None] * len(firsts)
        for h in range(PEER_HEADS):
            keys = slice(h * nk // 2, (h + 1) * nk // 2)
            r2 = _unpack_rows(r2_ref[keys, lanes])
            e2 = _unpack_rows(e2_ref[keys, lanes])
            for k, ii in enumerate(firsts):
                n1 = n1_ref[h, ii:ii + 1, lanes].astype(BF16)
                e1 = e1_ref[h, ii:ii + 1, lanes].astype(BF16)
                term = jnp.where(r2 < n1, e2, 0.0) * e1
                ws[k] = term if ws[k] is None else ws[k] + term
        for k, ii in enumerate(firsts):
            a = act[k * nk:(k + 1) * nk, lanes].astype(BF16)
            gelu = (0.5 * a) * (1.0 + lax.erf(a * (2.0 ** -0.5)))
            d_ref[(ii - first0) * nk // 2:(ii - first0 + 1) * nk // 2, lanes] = _pack_rows(ws[k] * gelu)


def _peer_kernel(hnt_ref, u_ref, vt_ref, n1_ref, e1_ref, r2_ref, e2_ref, x1_ref, g_ref,
                 o_ref, acc_ref, *d_refs):
    c = pl.program_id(1)
    nk = PEER_N_KEYS
    sub = PEER_SUB_CHUNK
    piece = sub // PEER_MXU_SPLIT
    n_sub = u_ref.shape[0] // sub

    @pl.when(c == 0)
    def _():
        acc_ref[...] = jnp.zeros_like(acc_ref)

    acts = {}

    def first(s):
        for p in range(PEER_MXU_SPLIT):
            rows = slice(s * sub + p * piece, s * sub + (p + 1) * piece)
            acts[s, p] = rows, jnp.dot(u_ref[rows, :], hnt_ref[...], preferred_element_type=F32)

    def gate_and_second(s):
        for p in range(PEER_MXU_SPLIT):
            rows, act = acts.pop((s, p))
            firsts = range(rows.start // nk, rows.stop // nk)
            _peer_gates(n1_ref, e1_ref, r2_ref, e2_ref, firsts, act, d_refs[s], s * sub // nk)
        d_t = _unpack_rows(d_refs[s][...])
        acc_ref[...] += jnp.dot(vt_ref[:, s * sub:(s + 1) * sub], d_t, preferred_element_type=F32)

    first(0)
    for s in range(n_sub):
        if s + 1 < n_sub:
            first(s + 1)
        gate_and_second(s)

    @pl.when(c == pl.num_programs(1) - 1)
    def _():
        y = x1_ref[...] + jnp.transpose(acc_ref[...])
        o_ref[...] = _rms(y, g_ref[...])


def _peer(hnt, u, v_t, n1, e1, r2, e2, x1, g):
    t = x1.shape[0]
    tb = min(PEER_TOKENS, t)
    ec = PEER_EXPERT_CHUNK
    packed_rows = PEER_HEADS * PEER_N_KEYS // 2
    dense = pl.BlockSpec((packed_rows, tb), lambda i, c: (0, i))
    r2 = r2.reshape(packed_rows, t)
    e2 = e2.reshape(packed_rows, t)
    rows = pl.BlockSpec((PEER_HEADS, ec // PEER_N_KEYS, tb), lambda i, c: (0, c, i))
    return pl.pallas_call(
        _peer_kernel,
        grid=(t // tb, PEER_N_EXPERTS // ec),
        in_specs=[pl.BlockSpec((D_MODEL, tb), lambda i, c: (0, i)),
                  pl.BlockSpec((ec, D_MODEL), lambda i, c: (c, 0)),
                  pl.BlockSpec((D_MODEL, ec), lambda i, c: (0, c)),
                  rows, rows, dense, dense,
                  pl.BlockSpec((tb, D_MODEL), lambda i, c: (i, 0)),
                  pl.BlockSpec((1, D_MODEL), lambda i, c: (0, 0))],
        out_specs=pl.BlockSpec((tb, D_MODEL), lambda i, c: (i, 0)),
        out_shape=jax.ShapeDtypeStruct((t, D_MODEL), F32),
        scratch_shapes=([pltpu.VMEM((D_MODEL, tb), F32)]
                        + [pltpu.VMEM((PEER_SUB_CHUNK // 2, tb), jnp.uint32)] * (ec // PEER_SUB_CHUNK)),
        compiler_params=_params(("parallel", "arbitrary")),
        name="peer",
    )(hnt, u, v_t, n1, e1, r2, e2, x1, g)


def kernel(x, norm_mix_g, w_in, attn_sinks, hgrn_lb_logits, hgrn_norm_g, w_attn_proj, w_hgrn_proj,
           w_out, norm_ffn_g, w_peer_q, peer_keys, peer_u, peer_v, norm_final_g):
    b, s, d = x.shape
    assert d == D_MODEL and norm_mix_g.shape[0] == 1
    t = b * s
    x2 = x.reshape(t, d)
    row = lambda a: a.reshape(1, -1).astype(F32)

    aq, ak, av, hq, hf, hi, hg, ga, gh = _inproj(x2, row(norm_mix_g[0]), w_in[0].astype(BF16))
    y_attn = _attention(aq, ak, av, attn_sinks[0].astype(F32), b, s)
    y_hgrn = _hgrn(hq, hf, hi, hg, hgrn_lb_logits.astype(F32), row(hgrn_norm_g[0]), b, s)
    x1, hnt = _merge(x2, y_attn, y_hgrn, ga, gh, w_attn_proj[0].astype(BF16),
                     w_hgrn_proj[0].astype(BF16), w_out[0].astype(BF16), row(norm_ffn_g[0]))
    wq_t = jnp.transpose(w_peer_q[0]).astype(BF16)
    keys = peer_keys[0].astype(BF16)
    n1, e1, r2, e2 = _route(hnt, wq_t, keys[0], keys[1])
    out = _peer(hnt, peer_u[0].astype(BF16), jnp.transpose(peer_v[0]).astype(BF16),
                n1, e1, r2, e2, x1, row(norm_final_g))
    return out.reshape(b, s, d)
```

```python
import jax
import jax.numpy as jnp
from jax import lax
from jax.experimental import pallas as pl
from jax.experimental.pallas import tpu as pltpu

F32 = jnp.float32
BF16 = jnp.bfloat16

D_MODEL = 1024
ATTN_HEADS = 8
ATTN_KV_HEADS = 2
ATTN_GROUP = ATTN_HEADS // ATTN_KV_HEADS
ATTN_HEAD_DIM = 64
ATTN_WIDTH = ATTN_HEADS * ATTN_HEAD_DIM
KV_WIDTH = ATTN_KV_HEADS * ATTN_HEAD_DIM
WINDOW = 128
ATTN_BLOCK = 128
HGRN_HEADS = 4
HGRN_DK = 128
HGRN_DV = 128
HGRN_WIDTH = HGRN_HEADS * HGRN_DK
HGRN_CHUNK = 32
PEER_HEADS = 8
PEER_N_KEYS = 128
PEER_N_EXPERTS = PEER_N_KEYS * PEER_N_KEYS
PEER_KEY_DIM = 128
PEER_TOPK = 16
EPS = 1e-6
MASK_VALUE = -1e30

SPLIT_SIZES = (ATTN_WIDTH, KV_WIDTH, KV_WIDTH, HGRN_WIDTH, HGRN_WIDTH,
               HGRN_WIDTH, HGRN_WIDTH, D_MODEL, D_MODEL)
IN_WIDTH = sum(SPLIT_SIZES)

VMEM_LIMIT_BYTES = 56 * 1024 * 1024

INPROJ_TOKENS = 512
HGRN_TOKENS = 256
HGRN_BATCH_ROWS = 2
MERGE_TOKENS = 512
ROUTE_TOKENS = 512
PEER_TOKENS = 512
PEER_EXPERT_CHUNK = 2048
PEER_SUB_CHUNK = 1024
LANE_TILE = 128
SUBLANES = 8
PEER_MXU_SPLIT = 4


def _params(semantics):
    return pltpu.CompilerParams(dimension_semantics=semantics,
                                vmem_limit_bytes=VMEM_LIMIT_BYTES)


def _nt_dot(a, b):
    return lax.dot_general(a, b, (((1,), (1,)), ((), ())), preferred_element_type=F32)


def _rms(x, g):
    ms = jnp.mean(x * x, axis=-1, keepdims=True)
    return x * lax.rsqrt(ms + EPS) * g


def _inproj_kernel(x_ref, g_ref, w_ref, *out_refs):
    h = _rms(x_ref[...], g_ref[...]).astype(BF16)
    off = 0
    for o_ref, width in zip(out_refs, SPLIT_SIZES):
        o_ref[...] = jnp.dot(h, w_ref[:, off:off + width],
                             preferred_element_type=F32).astype(o_ref.dtype)
        off += width


def _inproj(x2, g, w_in):
    t = x2.shape[0]
    tm = min(INPROJ_TOKENS, t)
    out_shape = tuple(jax.ShapeDtypeStruct((t, w), BF16) for w in SPLIT_SIZES)
    return pl.pallas_call(
        _inproj_kernel,
        grid=(t // tm,),
        in_specs=[pl.BlockSpec((tm, D_MODEL), lambda i: (i, 0)),
                  pl.BlockSpec((1, D_MODEL), lambda i: (0, 0)),
                  pl.BlockSpec((D_MODEL, IN_WIDTH), lambda i: (0, 0))],
        out_specs=tuple(pl.BlockSpec((tm, w), lambda i: (i, 0)) for w in SPLIT_SIZES),
        out_shape=out_shape,
        compiler_params=_params(("parallel",)),
        name="inproj",
    )(x2, g, w_in)


def _alibi_slopes():
    return [float(2.0 ** (-8.0 * h / ATTN_HEADS)) for h in range(1, ATTN_HEADS + 1)]


def _attn_kernel(sink_ref, q_ref, kp_ref, kc_ref, vp_ref, vc_ref, o_ref):
    n = pl.program_id(1)
    w = ATTN_BLOCK
    q = q_ref[0]
    k = jnp.concatenate([kp_ref[0], kc_ref[0]], axis=0)
    v = jnp.concatenate([vp_ref[0], vc_ref[0]], axis=0)
    qpos = lax.broadcasted_iota(jnp.int32, (w, 2 * w), 0) + w
    kpos = lax.broadcasted_iota(jnp.int32, (w, 2 * w), 1)
    dist = qpos - kpos
    first_ok = jnp.where(n > 0, 0, w)
    in_window = jnp.where(dist >= 0, jnp.where(dist < WINDOW, 1, 0), 0)
    valid = jnp.where(kpos >= first_ok, in_window, 0) == 1
    distf = dist.astype(F32)
    scale = ATTN_HEAD_DIM ** -0.5
    slopes = _alibi_slopes()
    outs = []
    scores = []
    for kh in range(ATTN_KV_HEADS):
        ksl = k[:, kh * ATTN_HEAD_DIM:(kh + 1) * ATTN_HEAD_DIM]
        qg = jnp.concatenate(
            [q[:, (kh * ATTN_GROUP + g) * ATTN_HEAD_DIM:(kh * ATTN_GROUP + g + 1) * ATTN_HEAD_DIM]
             for g in range(ATTN_GROUP)], axis=0)
        scores.append(_nt_dot(qg, ksl) * scale)
    for kh in range(ATTN_KV_HEADS):
        vsl = v[:, kh * ATTN_HEAD_DIM:(kh + 1) * ATTN_HEAD_DIM]
        s_all = scores[kh]
        ps, denoms = [], []
        for g in range(ATTN_GROUP):
            head = kh * ATTN_GROUP + g
            s = s_all[g * w:(g + 1) * w] - slopes[head] * distf
            s = jnp.where(valid, s, MASK_VALUE)
            sink = sink_ref[head]
            m = jnp.maximum(jnp.max(s, axis=-1, keepdims=True), sink)
            p = jnp.exp(s - m)
            denoms.append(jnp.sum(p, axis=-1, keepdims=True) + jnp.exp(sink - m))
            ps.append(p.astype(BF16))
        o_all = jnp.dot(jnp.concatenate(ps, axis=0), vsl, preferred_element_type=F32)
        for g in range(ATTN_GROUP):
            outs.append(o_all[g * w:(g + 1) * w] / denoms[g])
    o_ref[0] = jnp.concatenate(outs, axis=-1).astype(o_ref.dtype)


def _attention(aq, ak, av, sinks, b, s):
    nb = s // ATTN_BLOCK
    q3 = aq.reshape(b, s, ATTN_WIDTH)
    k3 = ak.reshape(b, s, KV_WIDTH)
    v3 = av.reshape(b, s, KV_WIDTH)
    cur = lambda i, j: (i, j, 0)
    prev = lambda i, j: (i, jnp.maximum(j - 1, 0), 0)
    out = pl.pallas_call(
        _attn_kernel,
        grid=(b, nb),
        in_specs=[pl.BlockSpec(memory_space=pltpu.SMEM),
                  pl.BlockSpec((1, ATTN_BLOCK, ATTN_WIDTH), cur),
                  pl.BlockSpec((1, ATTN_BLOCK, KV_WIDTH), prev),
                  pl.BlockSpec((1, ATTN_BLOCK, KV_WIDTH), cur),
                  pl.BlockSpec((1, ATTN_BLOCK, KV_WIDTH), prev),
                  pl.BlockSpec((1, ATTN_BLOCK, KV_WIDTH), cur)],
        out_specs=pl.BlockSpec((1, ATTN_BLOCK, ATTN_WIDTH), cur),
        out_shape=jax.ShapeDtypeStruct((b, s, ATTN_WIDTH), BF16),
        compiler_params=_params(("parallel", "parallel")),
        name="attn",
    )(sinks, q3, k3, k3, v3, v3)
    return out.reshape(b * s, ATTN_WIDTH)


def _split3(x):
    hi = x.astype(BF16)
    r1 = x - hi.astype(F32)
    mid = r1.astype(BF16)
    lo = (r1 - mid.astype(F32)).astype(BF16)
    return hi, mid, lo


def _hgrn_kernel(lb_ref, gn_ref, hq_ref, hf_ref, hi_ref, hg_ref, o_ref, st_ref, o_scr, ds_scr):
    n_batch, sb = hq_ref.shape[0], hq_ref.shape[1]
    c = HGRN_CHUNK

    @pl.when(pl.program_id(1) == 0)
    def _():
        st_ref[...] = jnp.zeros_like(st_ref)

    logits = lb_ref[...]
    e = jnp.exp(logits - jnp.max(logits, axis=0, keepdims=True))
    lb = e[0:1] / jnp.sum(e, axis=0, keepdims=True)

    r = lax.broadcasted_iota(jnp.int32, (sb, sb), 0)
    cc = lax.broadcasted_iota(jnp.int32, (sb, sb), 1)
    same = (r // c) == (cc // c)
    tri = jnp.where(same, jnp.where(cc <= r, 1.0, 0.0), 0.0).astype(BF16)
    blk = jnp.where(same, 1.0, 0.0).astype(BF16)
    tr = lax.broadcasted_iota(jnp.int32, (c, c), 0)
    tc = lax.broadcasted_iota(jnp.int32, (c, c), 1)
    causal = tc <= tr

    def prepare(bb):
        f = lb + (1.0 - lb) * jax.nn.sigmoid(hf_ref[bb].astype(F32))
        kk = 1.0 - f
        parts = _split3(jnp.log(f))
        bcum = sum(jnp.dot(tri, p, preferred_element_type=F32) for p in parts)
        blast = sum(jnp.dot(blk, p, preferred_element_type=F32) for p in parts)
        hq = hq_ref[bb].astype(F32)
        q_dec = (hq * jax.nn.sigmoid(hq) * jnp.exp(bcum)).astype(BF16)
        k_inv = (kk * jnp.exp(-bcum)).astype(BF16)
        k_end = (kk * jnp.exp(blast - bcum)).astype(BF16)
        vv = hi_ref[bb]
        v_t = jnp.transpose(vv.astype(F32)).astype(BF16)
        return q_dec, k_inv, k_end, jnp.exp(blast), vv, v_t

    rows_of_batch = [prepare(bb) for bb in range(n_batch)]
    n_chunks = sb // c
    units = [(ci, bb, h) for ci in range(n_chunks) for bb in range(n_batch) for h in range(HGRN_HEADS)]
    rows = lambda ci: slice(ci * c, (ci + 1) * c)
    lanes = lambda h: slice(h * HGRN_DK, (h + 1) * HGRN_DK)

    intra = {}
    for ci, bb, h in units:
        q_dec, k_inv = rows_of_batch[bb][0], rows_of_batch[bb][1]
        a = _nt_dot(q_dec[rows(ci), lanes(h)], k_inv[rows(ci), lanes(h)])
        intra[ci, bb, h] = jnp.where(causal, a, 0.0).astype(BF16)
    for ci, bb, h in units:
        k_end, v_t = rows_of_batch[bb][2], rows_of_batch[bb][5]
        ds_scr[ci, bb, h] = jnp.dot(v_t[lanes(h), rows(ci)], k_end[rows(ci), lanes(h)],
                                    preferred_element_type=F32)
    for ci, bb, h in units:
        vv = rows_of_batch[bb][4]
        o_scr[bb, rows(ci), lanes(h)] = jnp.dot(intra.pop((ci, bb, h)), vv[rows(ci), lanes(h)],
                                                preferred_element_type=F32)

    for ci in range(n_chunks):
        states = {}
        for bb in range(n_batch):
            q_dec = rows_of_batch[bb][0]
            for h in range(HGRN_HEADS):
                states[bb, h] = st = st_ref[bb, h]
                o_scr[bb, rows(ci), lanes(h)] += _nt_dot(q_dec[rows(ci), lanes(h)], st.astype(BF16))
        for bb in range(n_batch):
            decay = rows_of_batch[bb][3]
            for h in range(HGRN_HEADS):
                st_ref[bb, h] = states[bb, h] * decay[ci * c:ci * c + 1, lanes(h)] + ds_scr[ci, bb, h]

    for bb in range(n_batch):
        hg = hg_ref[bb].astype(F32)
        gate = hg * jax.nn.sigmoid(hg)
        ys = []
        for h in range(HGRN_HEADS):
            lanes = slice(h * HGRN_DV, (h + 1) * HGRN_DV)
            ys.append(_rms(o_scr[bb, :, lanes], gn_ref[...]) * gate[:, lanes])
        o_ref[bb] = jnp.concatenate(ys, axis=-1).astype(o_ref.dtype)


def _hgrn(hq, hf, hi, hg, lb_logits, gn, b, s):
    sb = min(HGRN_TOKENS, s)
    nb = HGRN_BATCH_ROWS if b % HGRN_BATCH_ROWS == 0 else 1
    spec = pl.BlockSpec((nb, sb, HGRN_WIDTH), lambda i, j: (i, j, 0))
    r3 = lambda a: a.reshape(b, s, HGRN_WIDTH)
    out = pl.pallas_call(
        _hgrn_kernel,
        grid=(b // nb, s // sb),
        in_specs=[pl.BlockSpec(lb_logits.shape, lambda i, j: (0, 0)),
                  pl.BlockSpec((1, HGRN_DV), lambda i, j: (0, 0)),
                  spec, spec, spec, spec],
        out_specs=spec,
        out_shape=jax.ShapeDtypeStruct((b, s, HGRN_WIDTH), BF16),
        scratch_shapes=[pltpu.VMEM((nb, HGRN_HEADS, HGRN_DV, HGRN_DK), F32),
                        pltpu.VMEM((nb, sb, HGRN_WIDTH), F32),
                        pltpu.VMEM((sb // HGRN_CHUNK, nb, HGRN_HEADS, HGRN_DV, HGRN_DK), F32)],
        compiler_params=_params(("parallel", "arbitrary")),
        name="hgrn",
    )(lb_logits, gn, r3(hq), r3(hf), r3(hi), r3(hg))
    return out.reshape(b * s, HGRN_WIDTH)


def _merge_kernel(x_ref, ya_ref, yh_ref, ga_ref, gh_ref, wa_ref, wh_ref, wo_ref, g_ref,
                  x1_ref, hnt_ref):
    pa = jnp.dot(ya_ref[...], wa_ref[...], preferred_element_type=F32)
    ph = jnp.dot(yh_ref[...], wh_ref[...], preferred_element_type=F32)
    merged = (jax.nn.sigmoid(ga_ref[...].astype(F32)) * pa
              + jax.nn.sigmoid(gh_ref[...].astype(F32)) * ph)
    x1 = x_ref[...] + jnp.dot(merged.astype(BF16), wo_ref[...], preferred_element_type=F32)
    x1_ref[...] = x1
    hnt_ref[...] = jnp.transpose(_rms(x1, g_ref[...])).astype(BF16)


def _merge(x2, ya, yh, ga, gh, wa, wh, wo, g):
    t = x2.shape[0]
    tm = min(MERGE_TOKENS, t)
    row = lambda w: pl.BlockSpec((tm, w), lambda i: (i, 0))
    full = lambda a: pl.BlockSpec(a.shape, lambda i: (0, 0))
    return pl.pallas_call(
        _merge_kernel,
        grid=(t // tm,),
        in_specs=[row(D_MODEL), row(ATTN_WIDTH), row(HGRN_WIDTH), row(D_MODEL), row(D_MODEL),
                  full(wa), full(wh), full(wo), full(g)],
        out_specs=(row(D_MODEL), pl.BlockSpec((D_MODEL, tm), lambda i: (0, i))),
        out_shape=(jax.ShapeDtypeStruct((t, D_MODEL), F32),
                   jax.ShapeDtypeStruct((D_MODEL, t), BF16)),
        compiler_params=_params(("parallel",)),
        name="merge",
    )(x2, ya, yh, ga, gh, wa, wh, wo, g)


def _as_f32(i):
    return jnp.asarray(i, jnp.int32).astype(F32)


def _topk_exact(s, k):
    n_rows, lanes = s.shape
    row = lax.broadcasted_iota(jnp.int32, (n_rows, lanes), 0).astype(F32)
    kid = lax.broadcasted_iota(jnp.int32, (k, lanes), 0).astype(F32)

    def body(a, carry):
        s, rank, vals = carry
        af = _as_f32(a)
        m = jnp.max(s, axis=0, keepdims=True)
        first = jnp.min(jnp.where(s == m, row, float(n_rows)), axis=0, keepdims=True)
        sel = row == first
        return jnp.where(sel, -jnp.inf, s), jnp.where(sel, af, rank), jnp.where(kid == af, m, vals)

    init = (s, jnp.full((n_rows, lanes), float(k), F32), jnp.zeros((k, lanes), F32))
    _, rank, vals = lax.fori_loop(0, k, body, init)
    return rank, vals


RANK_CODE_SCALE = 2.0 ** 123
RANK_CODE_LIMIT = -float(PEER_TOPK) * RANK_CODE_SCALE


def _topk_quick_pair(s1, s2, k):
    c1, c2, v1, v2 = s1, s2, [], []
    for a in range(k):
        code = -float(a + k) * RANK_CODE_SCALE
        m1 = jnp.max(c1, axis=0, keepdims=True)
        m2 = jnp.max(c2, axis=0, keepdims=True)
        c1 = jnp.where(c1 == m1, code, c1)
        c2 = jnp.where(c2 == m2, code, c2)
        v1.append(m1)
        v2.append(m2)
    v1 = jnp.concatenate(v1, axis=0)
    v2 = jnp.concatenate(v2, axis=0)

    def decode(c, s):
        marked = c <= RANK_CODE_LIMIT
        rank = jnp.where(marked, c * (-1.0 / RANK_CODE_SCALE) - float(k), float(k))
        n_marked = jnp.sum(jnp.where(marked, 1.0, 0.0), axis=0, keepdims=True)
        in_range = jnp.min(s, axis=0, keepdims=True) > RANK_CODE_LIMIT
        return rank, jnp.where(n_marked == float(k), jnp.where(in_range, 1.0, 0.0), 0.0)

    rank1, ok1 = decode(c1, s1)
    rank2, ok2 = decode(c2, s2)
    return rank1, v1, rank2, v2, ok1 * ok2


def _cand_row_lengths(k):
    return [k // (a + 1) for a in range(k)]


def _candidates(v1, v2, k):
    lengths = _cand_row_lengths(k)
    rows = [v1[a:a + 1] + v2[:n] for a, n in enumerate(lengths)]
    pad = -sum(lengths) % SUBLANES
    if pad:
        rows.append(jnp.full((pad, v1.shape[1]), -jnp.inf, F32))
    return jnp.concatenate(rows, axis=0)


def _choose_exact(cand, k):
    n_cand, lanes = cand.shape
    crow = lax.broadcasted_iota(jnp.int32, (n_cand, lanes), 0).astype(F32)

    def body(_, c):
        m = jnp.max(c, axis=0, keepdims=True)
        first = jnp.min(jnp.where(c == m, crow, float(n_cand)), axis=0, keepdims=True)
        return jnp.where(crow == first, -jnp.inf, c)

    return lax.fori_loop(0, k, body, cand)


def _choose_quick(cand, k):
    for _ in range(k):
        cand = jnp.where(cand == jnp.max(cand, axis=0, keepdims=True), -jnp.inf, cand)
    return cand


def _pack_rows(x):
    return pltpu.bitcast(x, jnp.uint32)


def _unpack_rows(x):
    return pltpu.bitcast(x, BF16)


def _route_outputs(s1, s2, rank1, v1, rank2, v2, cand, left, k):
    chosen = jnp.where(left == -jnp.inf, jnp.where(cand > -jnp.inf, 1.0, 0.0), 0.0)
    top = v1[0:1] + v2[0:1]
    z = jnp.sum(chosen * jnp.exp(cand - top), axis=0, keepdims=True)
    lengths = _cand_row_lengths(k)
    offsets = [sum(lengths[:a]) for a in range(k)]
    counts = [jnp.sum(chosen[offsets[a]:offsets[a] + lengths[a]], axis=0, keepdims=True) for a in range(k)]
    nonempty = sum(counts[a] if lengths[a] == 1 else jnp.minimum(counts[a], 1.0) for a in range(k))
    n1 = jnp.where(rank1 < nonempty, 1.0, 0.0)
    for a in range(k):
        if lengths[a] > 1:
            n1 = jnp.where(rank1 == float(a), counts[a], n1)
    e1 = jnp.exp(s1 - v1[0:1])
    r2 = _pack_rows(rank2.astype(BF16))
    e2 = _pack_rows((jnp.exp(s2 - v2[0:1]) / z).astype(BF16))
    return (n1, e1, r2, e2), jnp.sum(chosen, axis=0, keepdims=True)


def _route_kernel(hnt_ref, wq_ref, k1_ref, k2_ref, n1_ref, e1_ref, r2_ref, e2_ref):
    k = PEER_TOPK
    out_refs = (n1_ref, e1_ref, r2_ref, e2_ref)
    q_t = jnp.dot(wq_ref[...], hnt_ref[...], preferred_element_type=F32).astype(BF16)
    s1 = jnp.dot(k1_ref[0], q_t[:PEER_KEY_DIM], preferred_element_type=F32)
    s2 = jnp.dot(k2_ref[0], q_t[PEER_KEY_DIM:], preferred_element_type=F32)

    def store(outs):
        for ref, val in zip(out_refs, outs):
            ref[0] = val

    tiles = [_topk_quick_pair(s1[:, lt * LANE_TILE:(lt + 1) * LANE_TILE],
                              s2[:, lt * LANE_TILE:(lt + 1) * LANE_TILE], k)
             for lt in range(s1.shape[1] // LANE_TILE)]
    rank1, v1, rank2, v2, ok = (jnp.concatenate(parts, axis=1) for parts in zip(*tiles))
    cand = _candidates(v1, v2, k)
    outs, n_chosen = _route_outputs(s1, s2, rank1, v1, rank2, v2, cand, _choose_quick(cand, k), k)
    store(outs)
    ok = ok * jnp.where(n_chosen == float(k), 1.0, 0.0)

    @pl.when(jnp.min(ok) < 1.0)
    def _():
        for lt in range(s1.shape[1] // LANE_TILE):
            lanes = slice(lt * LANE_TILE, (lt + 1) * LANE_TILE)

            @pl.when(jnp.min(ok[:, lanes]) < 1.0)
            def _():
                t1, t2 = s1[:, lanes], s2[:, lanes]
                rank1, v1 = _topk_exact(t1, k)
                rank2, v2 = _topk_exact(t2, k)
                cand = _candidates(v1, v2, k)
                outs, _ = _route_outputs(t1, t2, rank1, v1, rank2, v2, cand, _choose_exact(cand, k), k)
                for ref, val in zip(out_refs, outs):
                    ref[0, :, lanes] = val


def _route(hnt, wq_t, keys1, keys2):
    t = hnt.shape[1]
    tb = min(ROUTE_TOKENS, t)
    kd2 = 2 * PEER_KEY_DIM
    out = jax.ShapeDtypeStruct((PEER_HEADS, PEER_N_KEYS, t), F32)
    out16 = jax.ShapeDtypeStruct((PEER_HEADS, PEER_N_KEYS // 2, t), jnp.uint32)
    pspec = pl.BlockSpec((1, PEER_N_KEYS // 2, tb), lambda i, h: (h, 0, i))
    ospec = pl.BlockSpec((1, PEER_N_KEYS, tb), lambda i, h: (h, 0, i))
    kspec = pl.BlockSpec((1, PEER_N_KEYS, PEER_KEY_DIM), lambda i, h: (h, 0, 0))
    return pl.pallas_call(
        _route_kernel,
        grid=(t // tb, PEER_HEADS),
        in_specs=[pl.BlockSpec((D_MODEL, tb), lambda i, h: (0, i)),
                  pl.BlockSpec((kd2, D_MODEL), lambda i, h: (h, 0)),
                  kspec, kspec],
        out_specs=(ospec, ospec, pspec, pspec),
        out_shape=(out, out, out16, out16),
        compiler_params=_params(("parallel", "arbitrary")),
        name="route",
    )(hnt, wq_t, keys1, keys2)


def _peer_gates(n1_ref, e1_ref, r2_ref, e2_ref, firsts, act, d_ref, first0):
    nk = PEER_N_KEYS
    for lt in range(d_ref.shape[1] // LANE_TILE):
        lanes = slice(lt * LANE_TILE, (lt + 1) * LANE_TILE)
        ws = [None] * len(firsts)
        for h in range(PEER_HEADS):
            keys = slice(h * nk // 2, (h + 1) * nk // 2)
            r2 = _unpack_rows(r2_ref[keys, lanes])
            e2 = _unpack_rows(e2_ref[keys, lanes])
            for k, ii in enumerate(firsts):
                n1 = n1_ref[h, ii:ii + 1, lanes].astype(BF16)
                e1 = e1_ref[h, ii:ii + 1, lanes].astype(BF16)
                term = jnp.where(r2 < n1, e2, 0.0) * e1
                ws[k] = term if ws[k] is None else ws[k] + term
        for k, ii in enumerate(firsts):
            a = act[k * nk:(k + 1) * nk, lanes].astype(BF16)
            gelu = (0.5 * a) * (1.0 + lax.erf(a * (2.0 ** -0.5)))
            d_ref[(ii - first0) * nk // 2:(ii - first0 + 1) * nk // 2, lanes] = _pack_rows(ws[k] * gelu)


def _peer_kernel(hnt_ref, u_ref, vt_ref, n1_ref, e1_ref, r2_ref, e2_ref, x1_ref, g_ref,
                 o_ref, acc_ref, *d_refs):
    c = pl.program_id(1)
    nk = PEER_N_KEYS
    sub = PEER_SUB_CHUNK
    piece = sub // PEER_MXU_SPLIT
    n_sub = u_ref.shape[0] // sub

    @pl.when(c == 0)
    def _():
        acc_ref[...] = jnp.zeros_like(acc_ref)

    acts = {}

    def first(s):
        for p in range(PEER_MXU_SPLIT):
            rows = slice(s * sub + p * piece, s * sub + (p + 1) * piece)
            acts[s, p] = rows, jnp.dot(u_ref[rows, :], hnt_ref[...], preferred_element_type=F32)

    def gate_and_second(s):
        for p in range(PEER_MXU_SPLIT):
            rows, act = acts.pop((s, p))
            firsts = range(rows.start // nk, rows.stop // nk)
            _peer_gates(n1_ref, e1_ref, r2_ref, e2_ref, firsts, act, d_refs[s], s * sub // nk)
        d_t = _unpack_rows(d_refs[s][...])
        acc_ref[...] += jnp.dot(vt_ref[:, s * sub:(s + 1) * sub], d_t, preferred_element_type=F32)

    first(0)
    for s in range(n_sub):
        if s + 1 < n_sub:
            first(s + 1)
        gate_and_second(s)

    @pl.when(c == pl.num_programs(1) - 1)
    def _():
        y = x1_ref[...] + jnp.transpose(acc_ref[...])
        o_ref[...] = _rms(y, g_ref[...])


def _peer(hnt, u, v_t, n1, e1, r2, e2, x1, g):
    t = x1.shape[0]
    tb = min(PEER_TOKENS, t)
    ec = PEER_EXPERT_CHUNK
    packed_rows = PEER_HEADS * PEER_N_KEYS // 2
    dense = pl.BlockSpec((packed_rows, tb), lambda i, c: (0, i))
    r2 = r2.reshape(packed_rows, t)
    e2 = e2.reshape(packed_rows, t)
    rows = pl.BlockSpec((PEER_HEADS, ec // PEER_N_KEYS, tb), lambda i, c: (0, c, i))
    return pl.pallas_call(
        _peer_kernel,
        grid=(t // tb, PEER_N_EXPERTS // ec),
        in_specs=[pl.BlockSpec((D_MODEL, tb), lambda i, c: (0, i)),
                  pl.BlockSpec((ec, D_MODEL), lambda i, c: (c, 0)),
                  pl.BlockSpec((D_MODEL, ec), lambda i, c: (0, c)),
                  rows, rows, dense, dense,
                  pl.BlockSpec((tb, D_MODEL), lambda i, c: (i, 0)),
                  pl.BlockSpec((1, D_MODEL), lambda i, c: (0, 0))],
        out_specs=pl.BlockSpec((tb, D_MODEL), lambda i, c: (i, 0)),
        out_shape=jax.ShapeDtypeStruct((t, D_MODEL), F32),
        scratch_shapes=([pltpu.VMEM((D_MODEL, tb), F32)]
                        + [pltpu.VMEM((PEER_SUB_CHUNK // 2, tb), jnp.uint32)] * (ec // PEER_SUB_CHUNK)),
        compiler_params=_params(("parallel", "arbitrary")),
        name="peer",
    )(hnt, u, v_t, n1, e1, r2, e2, x1, g)


def kernel(x, norm_mix_g, w_in, attn_sinks, hgrn_lb_logits, hgrn_norm_g, w_attn_proj, w_hgrn_proj,
           w_out, norm_ffn_g, w_peer_q, peer_keys, peer_u, peer_v, norm_final_g):
    b, s, d = x.shape
    assert d == D_MODEL and norm_mix_g.shape[0] == 1
    t = b * s
    x2 = x.reshape(t, d)
    row = lambda a: a.reshape(1, -1).astype(F32)

    aq, ak, av, hq, hf, hi, hg, ga, gh = _inproj(x2, row(norm_mix_g[0]), w_in[0].astype(BF16))
    y_attn = _attention(aq, ak, av, attn_sinks[0].astype(F32), b, s)
    y_hgrn = _hgrn(hq, hf, hi, hg, hgrn_lb_logits.astype(F32), row(hgrn_norm_g[0]), b, s)
    x1, hnt = _merge(x2, y_attn, y_hgrn, ga, gh, w_attn_proj[0].astype(BF16),
                     w_hgrn_proj[0].astype(BF16), w_out[0].astype(BF16), row(norm_ffn_g[0]))
    wq_t = jnp.transpose(w_peer_q[0]).astype(BF16)
    keys = peer_keys[0].astype(BF16)
    n1, e1, r2, e2 = _route(hnt, wq_t, keys[0], keys[1])
    out = _peer(hnt, peer_u[0].astype(BF16), jnp.transpose(peer_v[0]).astype(BF16),
                n1, e1, r2, e2, x1, row(norm_final_g))
    return out.reshape(b, s, d)
```

```python
import jax
import jax.numpy as jnp
from jax import lax
from jax.experimental import pallas as pl
from jax.experimental.pallas import tpu as pltpu

F32 = jnp.float32
BF16 = jnp.bfloat16

D_MODEL = 1024
ATTN_HEADS = 8
ATTN_KV_HEADS = 2
ATTN_GROUP = ATTN_HEADS // ATTN_KV_HEADS
ATTN_HEAD_DIM = 64
ATTN_WIDTH = ATTN_HEADS * ATTN_HEAD_DIM
KV_WIDTH = ATTN_KV_HEADS * ATTN_HEAD_DIM
WINDOW = 128
ATTN_BLOCK = 128
HGRN_HEADS = 4
HGRN_DK = 128
HGRN_DV = 128
HGRN_WIDTH = HGRN_HEADS * HGRN_DK
HGRN_CHUNK = 32
PEER_HEADS = 8
PEER_N_KEYS = 128
PEER_N_EXPERTS = PEER_N_KEYS * PEER_N_KEYS
PEER_KEY_DIM = 128
PEER_TOPK = 16
EPS = 1e-6
MASK_VALUE = -1e30

SPLIT_SIZES = (ATTN_WIDTH, KV_WIDTH, KV_WIDTH, HGRN_WIDTH, HGRN_WIDTH,
               HGRN_WIDTH, HGRN_WIDTH, D_MODEL, D_MODEL)
IN_WIDTH = sum(SPLIT_SIZES)

VMEM_LIMIT_BYTES = 56 * 1024 * 1024

INPROJ_TOKENS = 512
HGRN_TOKENS = 256
HGRN_BATCH_ROWS = 2
MERGE_TOKENS = 512
ROUTE_TOKENS = 512
PEER_TOKENS = 512
PEER_EXPERT_CHUNK = 2048
PEER_SUB_CHUNK = 1024
LANE_TILE = 128
SUBLANES = 8
PEER_MXU_SPLIT = 4


def _params(semantics):
    return pltpu.CompilerParams(dimension_semantics=semantics,
                                vmem_limit_bytes=VMEM_LIMIT_BYTES)


def _nt_dot(a, b):
    return lax.dot_general(a, b, (((1,), (1,)), ((), ())), preferred_element_type=F32)


def _rms(x, g):
    ms = jnp.mean(x * x, axis=-1, keepdims=True)
    return x * lax.rsqrt(ms + EPS) * g


def _inproj_kernel(x_ref, g_ref, w_ref, *out_refs):
    h = _rms(x_ref[...], g_ref[...]).astype(BF16)
    off = 0
    for o_ref, width in zip(out_refs, SPLIT_SIZES):
        o_ref[...] = jnp.dot(h, w_ref[:, off:off + width],
                             preferred_element_type=F32).astype(o_ref.dtype)
        off += width


def _inproj(x2, g, w_in):
    t = x2.shape[0]
    tm = min(INPROJ_TOKENS, t)
    out_shape = tuple(jax.ShapeDtypeStruct((t, w), BF16) for w in SPLIT_SIZES)
    return pl.pallas_call(
        _inproj_kernel,
        grid=(t // tm,),
        in_specs=[pl.BlockSpec((tm, D_MODEL), lambda i: (i, 0)),
                  pl.BlockSpec((1, D_MODEL), lambda i: (0, 0)),
                  pl.BlockSpec((D_MODEL, IN_WIDTH), lambda i: (0, 0))],
        out_specs=tuple(pl.BlockSpec((tm, w), lambda i: (i, 0)) for w in SPLIT_SIZES),
        out_shape=out_shape,
        compiler_params=_params(("parallel",)),
        name="inproj",
    )(x2, g, w_in)


def _alibi_slopes():
    return [float(2.0 ** (-8.0 * h / ATTN_HEADS)) for h in range(1, ATTN_HEADS + 1)]


def _attn_kernel(sink_ref, q_ref, kp_ref, kc_ref, vp_ref, vc_ref, o_ref):
    n = pl.program_id(1)
    w = ATTN_BLOCK
    q = q_ref[0]
    k = jnp.concatenate([kp_ref[0], kc_ref[0]], axis=0)
    v = jnp.concatenate([vp_ref[0], vc_ref[0]], axis=0)
    qpos = lax.broadcasted_iota(jnp.int32, (w, 2 * w), 0) + w
    kpos = lax.broadcasted_iota(jnp.int32, (w, 2 * w), 1)
    dist = qpos - kpos
    first_ok = jnp.where(n > 0, 0, w)
    in_window = jnp.where(dist >= 0, jnp.where(dist < WINDOW, 1, 0), 0)
    valid = jnp.where(kpos >= first_ok, in_window, 0) == 1
    distf = dist.astype(F32)
    scale = ATTN_HEAD_DIM ** -0.5
    slopes = _alibi_slopes()
    outs = []
    scores = []
    for kh in range(ATTN_KV_HEADS):
        ksl = k[:, kh * ATTN_HEAD_DIM:(kh + 1) * ATTN_HEAD_DIM]
        qg = jnp.concatenate(
            [q[:, (kh * ATTN_GROUP + g) * ATTN_HEAD_DIM:(kh * ATTN_GROUP + g + 1) * ATTN_HEAD_DIM]
             for g in range(ATTN_GROUP)], axis=0)
        scores.append(_nt_dot(qg, ksl) * scale)
    for kh in range(ATTN_KV_HEADS):
        vsl = v[:, kh * ATTN_HEAD_DIM:(kh + 1) * ATTN_HEAD_DIM]
        s_all = scores[kh]
        ps, denoms = [], []
        for g in range(ATTN_GROUP):
            head = kh * ATTN_GROUP + g
            s = s_all[g * w:(g + 1) * w] - slopes[head] * distf
            s = jnp.where(valid, s, MASK_VALUE)
            sink = sink_ref[head]
            m = jnp.maximum(jnp.max(s, axis=-1, keepdims=True), sink)
            p = jnp.exp(s - m)
            denoms.append(jnp.sum(p, axis=-1, keepdims=True) + jnp.exp(sink - m))
            ps.append(p.astype(BF16))
        o_all = jnp.dot(jnp.concatenate(ps, axis=0), vsl, preferred_element_type=F32)
        for g in range(ATTN_GROUP):
            outs.append(o_all[g * w:(g + 1) * w] / denoms[g])
    o_ref[0] = jnp.concatenate(outs, axis=-1).astype(o_ref.dtype)


def _attention(aq, ak, av, sinks, b, s):
    nb = s // ATTN_BLOCK
    q3 = aq.reshape(b, s, ATTN_WIDTH)
    k3 = ak.reshape(b, s, KV_WIDTH)
    v3 = av.reshape(b, s, KV_WIDTH)
    cur = lambda i, j: (i, j, 0)
    prev = lambda i, j: (i, jnp.maximum(j - 1, 0), 0)
    out = pl.pallas_call(
        _attn_kernel,
        grid=(b, nb),
        in_specs=[pl.BlockSpec(memory_space=pltpu.SMEM),
                  pl.BlockSpec((1, ATTN_BLOCK, ATTN_WIDTH), cur),
                  pl.BlockSpec((1, ATTN_BLOCK, KV_WIDTH), prev),
                  pl.BlockSpec((1, ATTN_BLOCK, KV_WIDTH), cur),
                  pl.BlockSpec((1, ATTN_BLOCK, KV_WIDTH), prev),
                  pl.BlockSpec((1, ATTN_BLOCK, KV_WIDTH), cur)],
        out_specs=pl.BlockSpec((1, ATTN_BLOCK, ATTN_WIDTH), cur),
        out_shape=jax.ShapeDtypeStruct((b, s, ATTN_WIDTH), BF16),
        compiler_params=_params(("parallel", "parallel")),
        name="attn",
    )(sinks, q3, k3, k3, v3, v3)
    return out.reshape(b * s, ATTN_WIDTH)


def _split3(x):
    hi = x.astype(BF16)
    r1 = x - hi.astype(F32)
    mid = r1.astype(BF16)
    lo = (r1 - mid.astype(F32)).astype(BF16)
    return hi, mid, lo


def _hgrn_kernel(lb_ref, gn_ref, hq_ref, hf_ref, hi_ref, hg_ref, o_ref, st_ref, o_scr, ds_scr):
    n_batch, sb = hq_ref.shape[0], hq_ref.shape[1]
    c = HGRN_CHUNK

    @pl.when(pl.program_id(1) == 0)
    def _():
        st_ref[...] = jnp.zeros_like(st_ref)

    logits = lb_ref[...]
    e = jnp.exp(logits - jnp.max(logits, axis=0, keepdims=True))
    lb = e[0:1] / jnp.sum(e, axis=0, keepdims=True)

    r = lax.broadcasted_iota(jnp.int32, (sb, sb), 0)
    cc = lax.broadcasted_iota(jnp.int32, (sb, sb), 1)
    same = (r // c) == (cc // c)
    tri = jnp.where(same, jnp.where(cc <= r, 1.0, 0.0), 0.0).astype(BF16)
    blk = jnp.where(same, 1.0, 0.0).astype(BF16)
    tr = lax.broadcasted_iota(jnp.int32, (c, c), 0)
    tc = lax.broadcasted_iota(jnp.int32, (c, c), 1)
    causal = tc <= tr

    def prepare(bb):
        f = lb + (1.0 - lb) * jax.nn.sigmoid(hf_ref[bb].astype(F32))
        kk = 1.0 - f
        parts = _split3(jnp.log(f))
        bcum = sum(jnp.dot(tri, p, preferred_element_type=F32) for p in parts)
        blast = sum(jnp.dot(blk, p, preferred_element_type=F32) for p in parts)
        hq = hq_ref[bb].astype(F32)
        q_dec = (hq * jax.nn.sigmoid(hq) * jnp.exp(bcum)).astype(BF16)
        k_inv = (kk * jnp.exp(-bcum)).astype(BF16)
        k_end = (kk * jnp.exp(blast - bcum)).astype(BF16)
        vv = hi_ref[bb]
        v_t = jnp.transpose(vv.astype(F32)).astype(BF16)
        return q_dec, k_inv, k_end, jnp.exp(blast), vv, v_t

    rows_of_batch = [prepare(bb) for bb in range(n_batch)]
    n_chunks = sb // c
    units = [(ci, bb, h) for ci in range(n_chunks) for bb in range(n_batch) for h in range(HGRN_HEADS)]
    rows = lambda ci: slice(ci * c, (ci + 1) * c)
    lanes = lambda h: slice(h * HGRN_DK, (h + 1) * HGRN_DK)

    intra = {}
    for ci, bb, h in units:
        q_dec, k_inv = rows_of_batch[bb][0], rows_of_batch[bb][1]
        a = _nt_dot(q_dec[rows(ci), lanes(h)], k_inv[rows(ci), lanes(h)])
        intra[ci, bb, h] = jnp.where(causal, a, 0.0).astype(BF16)
    for ci, bb, h in units:
        k_end, v_t = rows_of_batch[bb][2], rows_of_batch[bb][5]
        ds_scr[ci, bb, h] = jnp.dot(v_t[lanes(h), rows(ci)], k_end[rows(ci), lanes(h)],
                                    preferred_element_type=F32)
    for ci, bb, h in units:
        vv = rows_of_batch[bb][4]
        o_scr[bb, rows(ci), lanes(h)] = jnp.dot(intra.pop((ci, bb, h)), vv[rows(ci), lanes(h)],
                                                preferred_element_type=F32)

    for ci in range(n_chunks):
        states = {}
        for bb in range(n_batch):
            q_dec = rows_of_batch[bb][0]
            for h in range(HGRN_HEADS):
                states[bb, h] = st = st_ref[bb, h]
                o_scr[bb, rows(ci), lanes(h)] += _nt_dot(q_dec[rows(ci), lanes(h)], st.astype(BF16))
        for bb in range(n_batch):
            decay = rows_of_batch[bb][3]
            for h in range(HGRN_HEADS):
                st_ref[bb, h] = states[bb, h] * decay[ci * c:ci * c + 1, lanes(h)] + ds_scr[ci, bb, h]

    for bb in range(n_batch):
        hg = hg_ref[bb].astype(F32)
        gate = hg * jax.nn.sigmoid(hg)
        ys = []
        for h in range(HGRN_HEADS):
            lanes = slice(h * HGRN_DV, (h + 1) * HGRN_DV)
            ys.append(_rms(o_scr[bb, :, lanes], gn_ref[...]) * gate[:, lanes])
        o_ref[bb] = jnp.concatenate(ys, axis=-1).astype(o_ref.dtype)


def _hgrn(hq, hf, hi, hg, lb_logits, gn, b, s):
    sb = min(HGRN_TOKENS, s)
    nb = HGRN_BATCH_ROWS if b % HGRN_BATCH_ROWS == 0 else 1
    spec = pl.BlockSpec((nb, sb, HGRN_WIDTH), lambda i, j: (i, j, 0))
    r3 = lambda a: a.reshape(b, s, HGRN_WIDTH)
    out = pl.pallas_call(
        _hgrn_kernel,
        grid=(b // nb, s // sb),
        in_specs=[pl.BlockSpec(lb_logits.shape, lambda i, j: (0, 0)),
                  pl.BlockSpec((1, HGRN_DV), lambda i, j: (0, 0)),
                  spec, spec, spec, spec],
        out_specs=spec,
        out_shape=jax.ShapeDtypeStruct((b, s, HGRN_WIDTH), BF16),
        scratch_shapes=[pltpu.VMEM((nb, HGRN_HEADS, HGRN_DV, HGRN_DK), F32),
                        pltpu.VMEM((nb, sb, HGRN_WIDTH), F32),
                        pltpu.VMEM((sb // HGRN_CHUNK, nb, HGRN_HEADS, HGRN_DV, HGRN_DK), F32)],
        compiler_params=_params(("parallel", "arbitrary")),
        name="hgrn",
    )(lb_logits, gn, r3(hq), r3(hf), r3(hi), r3(hg))
    return out.reshape(b * s, HGRN_WIDTH)


def _merge_kernel(x_ref, ya_ref, yh_ref, ga_ref, gh_ref, wa_ref, wh_ref, wo_ref, g_ref,
                  x1_ref, hnt_ref):
    pa = jnp.dot(ya_ref[...], wa_ref[...], preferred_element_type=F32)
    ph = jnp.dot(yh_ref[...], wh_ref[...], preferred_element_type=F32)
    merged = (jax.nn.sigmoid(ga_ref[...].astype(F32)) * pa
              + jax.nn.sigmoid(gh_ref[...].astype(F32)) * ph)
    x1 = x_ref[...] + jnp.dot(merged.astype(BF16), wo_ref[...], preferred_element_type=F32)
    x1_ref[...] = x1
    hnt_ref[...] = jnp.transpose(_rms(x1, g_ref[...])).astype(BF16)


def _merge(x2, ya, yh, ga, gh, wa, wh, wo, g):
    t = x2.shape[0]
    tm = min(MERGE_TOKENS, t)
    row = lambda w: pl.BlockSpec((tm, w), lambda i: (i, 0))
    full = lambda a: pl.BlockSpec(a.shape, lambda i: (0, 0))
    return pl.pallas_call(
        _merge_kernel,
        grid=(t // tm,),
        in_specs=[row(D_MODEL), row(ATTN_WIDTH), row(HGRN_WIDTH), row(D_MODEL), row(D_MODEL),
                  full(wa), full(wh), full(wo), full(g)],
        out_specs=(row(D_MODEL), pl.BlockSpec((D_MODEL, tm), lambda i: (0, i))),
        out_shape=(jax.ShapeDtypeStruct((t, D_MODEL), F32),
                   jax.ShapeDtypeStruct((D_MODEL, t), BF16)),
        compiler_params=_params(("parallel",)),
        name="merge",
    )(x2, ya, yh, ga, gh, wa, wh, wo, g)


def _as_f32(i):
    return jnp.asarray(i, jnp.int32).astype(F32)


def _topk_exact(s, k):
    n_rows, lanes = s.shape
    row = lax.broadcasted_iota(jnp.int32, (n_rows, lanes), 0).astype(F32)
    kid = lax.broadcasted_iota(jnp.int32, (k, lanes), 0).astype(F32)

    def body(a, carry):
        s, rank, vals = carry
        af = _as_f32(a)
        m = jnp.max(s, axis=0, keepdims=True)
        first = jnp.min(jnp.where(s == m, row, float(n_rows)), axis=0, keepdims=True)
        sel = row == first
        return jnp.where(sel, -jnp.inf, s), jnp.where(sel, af, rank), jnp.where(kid == af, m, vals)

    init = (s, jnp.full((n_rows, lanes), float(k), F32), jnp.zeros((k, lanes), F32))
    _, rank, vals = lax.fori_loop(0, k, body, init)
    return rank, vals


RANK_CODE_SCALE = 2.0 ** 123
RANK_CODE_LIMIT = -float(PEER_TOPK) * RANK_CODE_SCALE


def _topk_quick_pair(s1, s2, k):
    c1, c2, v1, v2 = s1, s2, [], []
    for a in range(k):
        code = -float(a + k) * RANK_CODE_SCALE
        m1 = jnp.max(c1, axis=0, keepdims=True)
        m2 = jnp.max(c2, axis=0, keepdims=True)
        c1 = jnp.where(c1 == m1, code, c1)
        c2 = jnp.where(c2 == m2, code, c2)
        v1.append(m1)
        v2.append(m2)
    v1 = jnp.concatenate(v1, axis=0)
    v2 = jnp.concatenate(v2, axis=0)

    def decode(c, s):
        marked = c <= RANK_CODE_LIMIT
        rank = jnp.where(marked, c * (-1.0 / RANK_CODE_SCALE) - float(k), float(k))
        n_marked = jnp.sum(jnp.where(marked, 1.0, 0.0), axis=0, keepdims=True)
        in_range = jnp.min(s, axis=0, keepdims=True) > RANK_CODE_LIMIT
        return rank, jnp.where(n_marked == float(k), jnp.where(in_range, 1.0, 0.0), 0.0)

    rank1, ok1 = decode(c1, s1)
    rank2, ok2 = decode(c2, s2)
    return rank1, v1, rank2, v2, ok1 * ok2


def _cand_row_lengths(k):
    return [k // (a + 1) for a in range(k)]


def _candidates(v1, v2, k):
    lengths = _cand_row_lengths(k)
    rows = [v1[a:a + 1] + v2[:n] for a, n in enumerate(lengths)]
    pad = -sum(lengths) % SUBLANES
    if pad:
        rows.append(jnp.full((pad, v1.shape[1]), -jnp.inf, F32))
    return jnp.concatenate(rows, axis=0)


def _choose_exact(cand, k):
    n_cand, lanes = cand.shape
    crow = lax.broadcasted_iota(jnp.int32, (n_cand, lanes), 0).astype(F32)

    def body(_, c):
        m = jnp.max(c, axis=0, keepdims=True)
        first = jnp.min(jnp.where(c == m, crow, float(n_cand)), axis=0, keepdims=True)
        return jnp.where(crow == first, -jnp.inf, c)

    return lax.fori_loop(0, k, body, cand)


def _choose_quick(cand, k):
    for _ in range(k):
        cand = jnp.where(cand == jnp.max(cand, axis=0, keepdims=True), -jnp.inf, cand)
    return cand


def _pack_rows(x):
    return pltpu.bitcast(x, jnp.uint32)


def _unpack_rows(x):
    return pltpu.bitcast(x, BF16)


def _drop_last_of_one_tie(chosen, cand, k):
    crow = lax.broadcasted_iota(jnp.int32, cand.shape, 0).astype(F32)
    one_extra = jnp.sum(chosen, axis=0, keepdims=True) == float(k + 1)
    lowest = jnp.min(jnp.where(chosen > 0.0, cand, jnp.inf), axis=0, keepdims=True)
    at_lowest = jnp.where(chosen > 0.0, jnp.where(cand == lowest, crow, -1.0), -1.0)
    last = jnp.max(at_lowest, axis=0, keepdims=True)
    return chosen - jnp.where(one_extra, jnp.where(crow == last, 1.0, 0.0), 0.0)


def _route_outputs(s1, s2, rank1, v1, rank2, v2, cand, left, k, repair_tie=False):
    chosen = jnp.where(left == -jnp.inf, jnp.where(cand > -jnp.inf, 1.0, 0.0), 0.0)
    if repair_tie:
        chosen = _drop_last_of_one_tie(chosen, cand, k)
    top = v1[0:1] + v2[0:1]
    z = jnp.sum(chosen * jnp.exp(cand - top), axis=0, keepdims=True)
    lengths = _cand_row_lengths(k)
    offsets = [sum(lengths[:a]) for a in range(k)]
    counts = [jnp.sum(chosen[offsets[a]:offsets[a] + lengths[a]], axis=0, keepdims=True) for a in range(k)]
    nonempty = sum(counts[a] if lengths[a] == 1 else jnp.minimum(counts[a], 1.0) for a in range(k))
    n1 = jnp.where(rank1 < nonempty, 1.0, 0.0)
    for a in range(k):
        if lengths[a] > 1:
            n1 = jnp.where(rank1 == float(a), counts[a], n1)
    e1 = jnp.exp(s1 - v1[0:1])
    r2 = _pack_rows(rank2.astype(BF16))
    e2 = _pack_rows((jnp.exp(s2 - v2[0:1]) / z).astype(BF16))
    return (n1, e1, r2, e2), jnp.sum(chosen, axis=0, keepdims=True)


def _route_kernel(hnt_ref, wq_ref, k1_ref, k2_ref, n1_ref, e1_ref, r2_ref, e2_ref):
    k = PEER_TOPK
    out_refs = (n1_ref, e1_ref, r2_ref, e2_ref)
    q_t = jnp.dot(wq_ref[...], hnt_ref[...], preferred_element_type=F32).astype(BF16)
    s1 = jnp.dot(k1_ref[0], q_t[:PEER_KEY_DIM], preferred_element_type=F32)
    s2 = jnp.dot(k2_ref[0], q_t[PEER_KEY_DIM:], preferred_element_type=F32)

    def store(outs):
        for ref, val in zip(out_refs, outs):
            ref[0] = val

    tiles = [_topk_quick_pair(s1[:, lt * LANE_TILE:(lt + 1) * LANE_TILE],
                              s2[:, lt * LANE_TILE:(lt + 1) * LANE_TILE], k)
             for lt in range(s1.shape[1] // LANE_TILE)]
    rank1, v1, rank2, v2, ok = (jnp.concatenate(parts, axis=1) for parts in zip(*tiles))
    cand = _candidates(v1, v2, k)
    outs, n_chosen = _route_outputs(s1, s2, rank1, v1, rank2, v2, cand, _choose_quick(cand, k), k,
                                    repair_tie=True)
    store(outs)
    ok = ok * jnp.where(n_chosen == float(k), 1.0, 0.0)

    @pl.when(jnp.min(ok) < 1.0)
    def _():
        for lt in range(s1.shape[1] // LANE_TILE):
            lanes = slice(lt * LANE_TILE, (lt + 1) * LANE_TILE)

            @pl.when(jnp.min(ok[:, lanes]) < 1.0)
            def _():
                t1, t2 = s1[:, lanes], s2[:, lanes]
                rank1, v1 = _topk_exact(t1, k)
                rank2, v2 = _topk_exact(t2, k)
                cand = _candidates(v1, v2, k)
                outs, _ = _route_outputs(t1, t2, rank1, v1, rank2, v2, cand, _choose_exact(cand, k), k)
                for ref, val in zip(out_refs, outs):
                    ref[0, :, lanes] = val


def _route(hnt, wq_t, keys1, keys2):
    t = hnt.shape[1]
    tb = min(ROUTE_TOKENS, t)
    kd2 = 2 * PEER_KEY_DIM
    out = jax.ShapeDtypeStruct((PEER_HEADS, PEER_N_KEYS, t), F32)
    out16 = jax.ShapeDtypeStruct((PEER_HEADS, PEER_N_KEYS // 2, t), jnp.uint32)
    pspec = pl.BlockSpec((1, PEER_N_KEYS // 2, tb), lambda i, h: (h, 0, i))
    ospec = pl.BlockSpec((1, PEER_N_KEYS, tb), lambda i, h: (h, 0, i))
    kspec = pl.BlockSpec((1, PEER_N_KEYS, PEER_KEY_DIM), lambda i, h: (h, 0, 0))
    return pl.pallas_call(
        _route_kernel,
        grid=(t // tb, PEER_HEADS),
        in_specs=[pl.BlockSpec((D_MODEL, tb), lambda i, h: (0, i)),
                  pl.BlockSpec((kd2, D_MODEL), lambda i, h: (h, 0)),
                  kspec, kspec],
        out_specs=(ospec, ospec, pspec, pspec),
        out_shape=(out, out, out16, out16),
        compiler_params=_params(("parallel", "arbitrary")),
        name="route",
    )(hnt, wq_t, keys1, keys2)


def _peer_gates(n1_ref, e1_ref, r2_ref, e2_ref, firsts, act, d_ref, first0):
    nk = PEER_N_KEYS
    for lt in range(d_ref.shape[1] // LANE_TILE):
        lanes = slice(lt * LANE_TILE, (lt + 1) * LANE_TILE)
        ws = [None] * len(firsts)
        for h in range(PEER_HEADS):
            keys = slice(h * nk // 2, (h + 1) * nk // 2)
            r2 = _unpack_rows(r2_ref[keys, lanes])
            e2 = _unpack_rows(e2_ref[keys, lanes])
            for k, ii in enumerate(firsts):
                n1 = n1_ref[h, ii:ii + 1, lanes].astype(BF16)
                e1 = e1_ref[h, ii:ii + 1, lanes].astype(BF16)
                term = jnp.where(r2 < n1, e2, 0.0) * e1
                ws[k] = term if ws[k] is None else ws[k] + term
        for k, ii in enumerate(firsts):
            a = act[k * nk:(k + 1) * nk, lanes].astype(BF16)
            gelu = (0.5 * a) * (1.0 + lax.erf(a * (2.0 ** -0.5)))
            d_ref[(ii - first0) * nk // 2:(ii - first0 + 1) * nk // 2, lanes] = _pack_rows(ws[k] * gelu)


def _peer_kernel(hnt_ref, u_ref, vt_ref, n1_ref, e1_ref, r2_ref, e2_ref, x1_ref, g_ref,
                 o_ref, acc_ref, *d_refs):
    c = pl.program_id(1)
    nk = PEER_N_KEYS
    sub = PEER_SUB_CHUNK
    piece = sub // PEER_MXU_SPLIT
    n_sub = u_ref.shape[0] // sub

    @pl.when(c == 0)
    def _():
        acc_ref[...] = jnp.zeros_like(acc_ref)

    acts = {}

    def first(s):
        for p in range(PEER_MXU_SPLIT):
            rows = slice(s * sub + p * piece, s * sub + (p + 1) * piece)
            acts[s, p] = rows, jnp.dot(u_ref[rows, :], hnt_ref[...], preferred_element_type=F32)

    def gate_and_second(s):
        for p in range(PEER_MXU_SPLIT):
            rows, act = acts.pop((s, p))
            firsts = range(rows.start // nk, rows.stop // nk)
            _peer_gates(n1_ref, e1_ref, r2_ref, e2_ref, firsts, act, d_refs[s], s * sub // nk)
        d_t = _unpack_rows(d_refs[s][...])
        acc_ref[...] += jnp.dot(vt_ref[:, s * sub:(s + 1) * sub], d_t, preferred_element_type=F32)

    first(0)
    for s in range(n_sub):
        if s + 1 < n_sub:
            first(s + 1)
        gate_and_second(s)

    @pl.when(c == pl.num_programs(1) - 1)
    def _():
        y = x1_ref[...] + jnp.transpose(acc_ref[...])
        o_ref[...] = _rms(y, g_ref[...])


def _peer(hnt, u, v_t, n1, e1, r2, e2, x1, g):
    t = x1.shape[0]
    tb = min(PEER_TOKENS, t)
    ec = PEER_EXPERT_CHUNK
    packed_rows = PEER_HEADS * PEER_N_KEYS // 2
    dense = pl.BlockSpec((packed_rows, tb), lambda i, c: (0, i))
    r2 = r2.reshape(packed_rows, t)
    e2 = e2.reshape(packed_rows, t)
    rows = pl.BlockSpec((PEER_HEADS, ec // PEER_N_KEYS, tb), lambda i, c: (0, c, i))
    return pl.pallas_call(
        _peer_kernel,
        grid=(t // tb, PEER_N_EXPERTS // ec),
        in_specs=[pl.BlockSpec((D_MODEL, tb), lambda i, c: (0, i)),
                  pl.BlockSpec((ec, D_MODEL), lambda i, c: (c, 0)),
                  pl.BlockSpec((D_MODEL, ec), lambda i, c: (0, c)),
                  rows, rows, dense, dense,
                  pl.BlockSpec((tb, D_MODEL), lambda i, c: (i, 0)),
                  pl.BlockSpec((1, D_MODEL), lambda i, c: (0, 0))],
        out_specs=pl.BlockSpec((tb, D_MODEL), lambda i, c: (i, 0)),
        out_shape=jax.ShapeDtypeStruct((t, D_MODEL), F32),
        scratch_shapes=([pltpu.VMEM((D_MODEL, tb), F32)]
                        + [pltpu.VMEM((PEER_SUB_CHUNK // 2, tb), jnp.uint32)] * (ec // PEER_SUB_CHUNK)),
        compiler_params=_params(("parallel", "arbitrary")),
        name="peer",
    )(hnt, u, v_t, n1, e1, r2, e2, x1, g)


def kernel(x, norm_mix_g, w_in, attn_sinks, hgrn_lb_logits, hgrn_norm_g, w_attn_proj, w_hgrn_proj,
           w_out, norm_ffn_g, w_peer_q, peer_keys, peer_u, peer_v, norm_final_g):
    b, s, d = x.shape
    assert d == D_MODEL and norm_mix_g.shape[0] == 1
    t = b * s
    x2 = x.reshape(t, d)
    row = lambda a: a.reshape(1, -1).astype(F32)

    aq, ak, av, hq, hf, hi, hg, ga, gh = _inproj(x2, row(norm_mix_g[0]), w_in[0].astype(BF16))
    y_attn = _attention(aq, ak, av, attn_sinks[0].astype(F32), b, s)
    y_hgrn = _hgrn(hq, hf, hi, hg, hgrn_lb_logits.astype(F32), row(hgrn_norm_g[0]), b, s)
    x1, hnt = _merge(x2, y_attn, y_hgrn, ga, gh, w_attn_proj[0].astype(BF16),
                     w_hgrn_proj[0].astype(BF16), w_out[0].astype(BF16), row(norm_ffn_g[0]))
    wq_t = jnp.transpose(w_peer_q[0]).astype(BF16)
    keys = peer_keys[0].astype(BF16)
    n1, e1, r2, e2 = _route(hnt, wq_t, keys[0], keys[1])
    out = _peer(hnt, peer_u[0].astype(BF16), jnp.transpose(peer_v[0]).astype(BF16),
                n1, e1, r2, e2, x1, row(norm_final_g))
    return out.reshape(b, s, d)
```

```python
import jax
import jax.numpy as jnp
from jax import lax
from jax.experimental import pallas as pl
from jax.experimental.pallas import tpu as pltpu

F32 = jnp.float32
BF16 = jnp.bfloat16

D_MODEL = 1024
ATTN_HEADS = 8
ATTN_KV_HEADS = 2
ATTN_GROUP = ATTN_HEADS // ATTN_KV_HEADS
ATTN_HEAD_DIM = 64
ATTN_WIDTH = ATTN_HEADS * ATTN_HEAD_DIM
KV_WIDTH = ATTN_KV_HEADS * ATTN_HEAD_DIM
WINDOW = 128
ATTN_BLOCK = 128
HGRN_HEADS = 4
HGRN_DK = 128
HGRN_DV = 128
HGRN_WIDTH = HGRN_HEADS * HGRN_DK
HGRN_CHUNK = 32
PEER_HEADS = 8
PEER_N_KEYS = 128
PEER_N_EXPERTS = PEER_N_KEYS * PEER_N_KEYS
PEER_KEY_DIM = 128
PEER_TOPK = 16
EPS = 1e-6
MASK_VALUE = -1e30

SPLIT_SIZES = (ATTN_WIDTH, KV_WIDTH, KV_WIDTH, HGRN_WIDTH, HGRN_WIDTH,
               HGRN_WIDTH, HGRN_WIDTH, D_MODEL, D_MODEL)
IN_WIDTH = sum(SPLIT_SIZES)

VMEM_LIMIT_BYTES = 56 * 1024 * 1024

INPROJ_TOKENS = 512
HGRN_TOKENS = 256
HGRN_BATCH_ROWS = 2
MERGE_TOKENS = 512
ROUTE_TOKENS = 512
ROUTE_HEADS = 2
PEER_TOKENS = 512
PEER_EXPERT_CHUNK = 2048
PEER_SUB_CHUNK = 1024
LANE_TILE = 128
SUBLANES = 8
PEER_MXU_SPLIT = 4


def _params(semantics):
    return pltpu.CompilerParams(dimension_semantics=semantics,
                                vmem_limit_bytes=VMEM_LIMIT_BYTES)


def _nt_dot(a, b):
    return lax.dot_general(a, b, (((1,), (1,)), ((), ())), preferred_element_type=F32)


def _rms(x, g):
    ms = jnp.mean(x * x, axis=-1, keepdims=True)
    return x * lax.rsqrt(ms + EPS) * g


def _inproj_kernel(x_ref, g_ref, w_ref, *out_refs):
    h = _rms(x_ref[...], g_ref[...]).astype(BF16)
    off = 0
    for o_ref, width in zip(out_refs, SPLIT_SIZES):
        o_ref[...] = jnp.dot(h, w_ref[:, off:off + width],
                             preferred_element_type=F32).astype(o_ref.dtype)
        off += width


def _inproj(x2, g, w_in):
    t = x2.shape[0]
    tm = min(INPROJ_TOKENS, t)
    out_shape = tuple(jax.ShapeDtypeStruct((t, w), BF16) for w in SPLIT_SIZES)
    return pl.pallas_call(
        _inproj_kernel,
        grid=(t // tm,),
        in_specs=[pl.BlockSpec((tm, D_MODEL), lambda i: (i, 0)),
                  pl.BlockSpec((1, D_MODEL), lambda i: (0, 0)),
                  pl.BlockSpec((D_MODEL, IN_WIDTH), lambda i: (0, 0))],
        out_specs=tuple(pl.BlockSpec((tm, w), lambda i: (i, 0)) for w in SPLIT_SIZES),
        out_shape=out_shape,
        compiler_params=_params(("parallel",)),
        name="inproj",
    )(x2, g, w_in)


def _alibi_slopes():
    return [float(2.0 ** (-8.0 * h / ATTN_HEADS)) for h in range(1, ATTN_HEADS + 1)]


def _attn_kernel(sink_ref, q_ref, kp_ref, kc_ref, vp_ref, vc_ref, o_ref):
    n = pl.program_id(1)
    w = ATTN_BLOCK
    q = q_ref[0]
    k = jnp.concatenate([kp_ref[0], kc_ref[0]], axis=0)
    v = jnp.concatenate([vp_ref[0], vc_ref[0]], axis=0)
    qpos = lax.broadcasted_iota(jnp.int32, (w, 2 * w), 0) + w
    kpos = lax.broadcasted_iota(jnp.int32, (w, 2 * w), 1)
    dist = qpos - kpos
    first_ok = jnp.where(n > 0, 0, w)
    in_window = jnp.where(dist >= 0, jnp.where(dist < WINDOW, 1, 0), 0)
    valid = jnp.where(kpos >= first_ok, in_window, 0) == 1
    distf = dist.astype(F32)
    scale = ATTN_HEAD_DIM ** -0.5
    slopes = _alibi_slopes()
    outs = []
    scores = []
    for kh in range(ATTN_KV_HEADS):
        ksl = k[:, kh * ATTN_HEAD_DIM:(kh + 1) * ATTN_HEAD_DIM]
        qg = jnp.concatenate(
            [q[:, (kh * ATTN_GROUP + g) * ATTN_HEAD_DIM:(kh * ATTN_GROUP + g + 1) * ATTN_HEAD_DIM]
             for g in range(ATTN_GROUP)], axis=0)
        scores.append(_nt_dot(qg, ksl) * scale)
    for kh in range(ATTN_KV_HEADS):
        vsl = v[:, kh * ATTN_HEAD_DIM:(kh + 1) * ATTN_HEAD_DIM]
        s_all = scores[kh]
        ps, denoms = [], []
        for g in range(ATTN_GROUP):
            head = kh * ATTN_GROUP + g
            s = s_all[g * w:(g + 1) * w] - slopes[head] * distf
            s = jnp.where(valid, s, MASK_VALUE)
            sink = sink_ref[head]
            m = jnp.maximum(jnp.max(s, axis=-1, keepdims=True), sink)
            p = jnp.exp(s - m)
            denoms.append(jnp.sum(p, axis=-1, keepdims=True) + jnp.exp(sink - m))
            ps.append(p.astype(BF16))
        o_all = jnp.dot(jnp.concatenate(ps, axis=0), vsl, preferred_element_type=F32)
        for g in range(ATTN_GROUP):
            outs.append(o_all[g * w:(g + 1) * w] / denoms[g])
    o_ref[0] = jnp.concatenate(outs, axis=-1).astype(o_ref.dtype)


def _attention(aq, ak, av, sinks, b, s):
    nb = s // ATTN_BLOCK
    q3 = aq.reshape(b, s, ATTN_WIDTH)
    k3 = ak.reshape(b, s, KV_WIDTH)
    v3 = av.reshape(b, s, KV_WIDTH)
    cur = lambda i, j: (i, j, 0)
    prev = lambda i, j: (i, jnp.maximum(j - 1, 0), 0)
    out = pl.pallas_call(
        _attn_kernel,
        grid=(b, nb),
        in_specs=[pl.BlockSpec(memory_space=pltpu.SMEM),
                  pl.BlockSpec((1, ATTN_BLOCK, ATTN_WIDTH), cur),
                  pl.BlockSpec((1, ATTN_BLOCK, KV_WIDTH), prev),
                  pl.BlockSpec((1, ATTN_BLOCK, KV_WIDTH), cur),
                  pl.BlockSpec((1, ATTN_BLOCK, KV_WIDTH), prev),
                  pl.BlockSpec((1, ATTN_BLOCK, KV_WIDTH), cur)],
        out_specs=pl.BlockSpec((1, ATTN_BLOCK, ATTN_WIDTH), cur),
        out_shape=jax.ShapeDtypeStruct((b, s, ATTN_WIDTH), BF16),
        compiler_params=_params(("parallel", "parallel")),
        name="attn",
    )(sinks, q3, k3, k3, v3, v3)
    return out.reshape(b * s, ATTN_WIDTH)


def _split3(x):
    hi = x.astype(BF16)
    r1 = x - hi.astype(F32)
    mid = r1.astype(BF16)
    lo = (r1 - mid.astype(F32)).astype(BF16)
    return hi, mid, lo


def _hgrn_kernel(lb_ref, gn_ref, hq_ref, hf_ref, hi_ref, hg_ref, o_ref, st_ref, o_scr, ds_scr):
    n_batch, sb = hq_ref.shape[0], hq_ref.shape[1]
    c = HGRN_CHUNK

    @pl.when(pl.program_id(1) == 0)
    def _():
        st_ref[...] = jnp.zeros_like(st_ref)

    logits = lb_ref[...]
    e = jnp.exp(logits - jnp.max(logits, axis=0, keepdims=True))
    lb = e[0:1] / jnp.sum(e, axis=0, keepdims=True)

    r = lax.broadcasted_iota(jnp.int32, (sb, sb), 0)
    cc = lax.broadcasted_iota(jnp.int32, (sb, sb), 1)
    same = (r // c) == (cc // c)
    tri = jnp.where(same, jnp.where(cc <= r, 1.0, 0.0), 0.0).astype(BF16)
    blk = jnp.where(same, 1.0, 0.0).astype(BF16)
    tr = lax.broadcasted_iota(jnp.int32, (c, c), 0)
    tc = lax.broadcasted_iota(jnp.int32, (c, c), 1)
    causal = tc <= tr

    def prepare(bb):
        f = lb + (1.0 - lb) * jax.nn.sigmoid(hf_ref[bb].astype(F32))
        kk = 1.0 - f
        parts = _split3(jnp.log(f))
        bcum = sum(jnp.dot(tri, p, preferred_element_type=F32) for p in parts)
        blast = sum(jnp.dot(blk, p, preferred_element_type=F32) for p in parts)
        hq = hq_ref[bb].astype(F32)
        q_dec = (hq * jax.nn.sigmoid(hq) * jnp.exp(bcum)).astype(BF16)
        k_inv = (kk * jnp.exp(-bcum)).astype(BF16)
        k_end = (kk * jnp.exp(blast - bcum)).astype(BF16)
        vv = hi_ref[bb]
        v_t = jnp.transpose(vv.astype(F32)).astype(BF16)
        return q_dec, k_inv, k_end, jnp.exp(blast), vv, v_t

    rows_of_batch = [prepare(bb) for bb in range(n_batch)]
    n_chunks = sb // c
    units = [(ci, bb, h) for ci in range(n_chunks) for bb in range(n_batch) for h in range(HGRN_HEADS)]
    rows = lambda ci: slice(ci * c, (ci + 1) * c)
    lanes = lambda h: slice(h * HGRN_DK, (h + 1) * HGRN_DK)

    intra = {}
    for ci, bb, h in units:
        q_dec, k_inv = rows_of_batch[bb][0], rows_of_batch[bb][1]
        a = _nt_dot(q_dec[rows(ci), lanes(h)], k_inv[rows(ci), lanes(h)])
        intra[ci, bb, h] = jnp.where(causal, a, 0.0).astype(BF16)
    for ci, bb, h in units:
        k_end, v_t = rows_of_batch[bb][2], rows_of_batch[bb][5]
        ds_scr[ci, bb, h] = jnp.dot(v_t[lanes(h), rows(ci)], k_end[rows(ci), lanes(h)],
                                    preferred_element_type=F32)
    for ci, bb, h in units:
        vv = rows_of_batch[bb][4]
        o_scr[bb, rows(ci), lanes(h)] = jnp.dot(intra.pop((ci, bb, h)), vv[rows(ci), lanes(h)],
                                                preferred_element_type=F32)

    for ci in range(n_chunks):
        states = {}
        for bb in range(n_batch):
            q_dec = rows_of_batch[bb][0]
            for h in range(HGRN_HEADS):
                states[bb, h] = st = st_ref[bb, h]
                o_scr[bb, rows(ci), lanes(h)] += _nt_dot(q_dec[rows(ci), lanes(h)], st.astype(BF16))
        for bb in range(n_batch):
            decay = rows_of_batch[bb][3]
            for h in range(HGRN_HEADS):
                st_ref[bb, h] = states[bb, h] * decay[ci * c:ci * c + 1, lanes(h)] + ds_scr[ci, bb, h]

    for bb in range(n_batch):
        hg = hg_ref[bb].astype(F32)
        gate = hg * jax.nn.sigmoid(hg)
        ys = []
        for h in range(HGRN_HEADS):
            lanes = slice(h * HGRN_DV, (h + 1) * HGRN_DV)
            ys.append(_rms(o_scr[bb, :, lanes], gn_ref[...]) * gate[:, lanes])
        o_ref[bb] = jnp.concatenate(ys, axis=-1).astype(o_ref.dtype)


def _hgrn(hq, hf, hi, hg, lb_logits, gn, b, s):
    sb = min(HGRN_TOKENS, s)
    nb = HGRN_BATCH_ROWS if b % HGRN_BATCH_ROWS == 0 else 1
    spec = pl.BlockSpec((nb, sb, HGRN_WIDTH), lambda i, j: (i, j, 0))
    r3 = lambda a: a.reshape(b, s, HGRN_WIDTH)
    out = pl.pallas_call(
        _hgrn_kernel,
        grid=(b // nb, s // sb),
        in_specs=[pl.BlockSpec(lb_logits.shape, lambda i, j: (0, 0)),
                  pl.BlockSpec((1, HGRN_DV), lambda i, j: (0, 0)),
                  spec, spec, spec, spec],
        out_specs=spec,
        out_shape=jax.ShapeDtypeStruct((b, s, HGRN_WIDTH), BF16),
        scratch_shapes=[pltpu.VMEM((nb, HGRN_HEADS, HGRN_DV, HGRN_DK), F32),
                        pltpu.VMEM((nb, sb, HGRN_WIDTH), F32),
                        pltpu.VMEM((sb // HGRN_CHUNK, nb, HGRN_HEADS, HGRN_DV, HGRN_DK), F32)],
        compiler_params=_params(("parallel", "arbitrary")),
        name="hgrn",
    )(lb_logits, gn, r3(hq), r3(hf), r3(hi), r3(hg))
    return out.reshape(b * s, HGRN_WIDTH)


def _merge_kernel(x_ref, ya_ref, yh_ref, ga_ref, gh_ref, wa_ref, wh_ref, wo_ref, g_ref,
                  x1_ref, hnt_ref):
    pa = jnp.dot(ya_ref[...], wa_ref[...], preferred_element_type=F32)
    ph = jnp.dot(yh_ref[...], wh_ref[...], preferred_element_type=F32)
    merged = (jax.nn.sigmoid(ga_ref[...].astype(F32)) * pa
              + jax.nn.sigmoid(gh_ref[...].astype(F32)) * ph)
    x1 = x_ref[...] + jnp.dot(merged.astype(BF16), wo_ref[...], preferred_element_type=F32)
    x1_ref[...] = x1
    hnt_ref[...] = jnp.transpose(_rms(x1, g_ref[...])).astype(BF16)


def _merge(x2, ya, yh, ga, gh, wa, wh, wo, g):
    t = x2.shape[0]
    tm = min(MERGE_TOKENS, t)
    row = lambda w: pl.BlockSpec((tm, w), lambda i: (i, 0))
    full = lambda a: pl.BlockSpec(a.shape, lambda i: (0, 0))
    return pl.pallas_call(
        _merge_kernel,
        grid=(t // tm,),
        in_specs=[row(D_MODEL), row(ATTN_WIDTH), row(HGRN_WIDTH), row(D_MODEL), row(D_MODEL),
                  full(wa), full(wh), full(wo), full(g)],
        out_specs=(row(D_MODEL), pl.BlockSpec((D_MODEL, tm), lambda i: (0, i))),
        out_shape=(jax.ShapeDtypeStruct((t, D_MODEL), F32),
                   jax.ShapeDtypeStruct((D_MODEL, t), BF16)),
        compiler_params=_params(("parallel",)),
        name="merge",
    )(x2, ya, yh, ga, gh, wa, wh, wo, g)


def _as_f32(i):
    return jnp.asarray(i, jnp.int32).astype(F32)


def _topk_exact(s, k):
    n_rows, lanes = s.shape
    row = lax.broadcasted_iota(jnp.int32, (n_rows, lanes), 0).astype(F32)
    kid = lax.broadcasted_iota(jnp.int32, (k, lanes), 0).astype(F32)

    def body(a, carry):
        s, rank, vals = carry
        af = _as_f32(a)
        m = jnp.max(s, axis=0, keepdims=True)
        first = jnp.min(jnp.where(s == m, row, float(n_rows)), axis=0, keepdims=True)
        sel = row == first
        return jnp.where(sel, -jnp.inf, s), jnp.where(sel, af, rank), jnp.where(kid == af, m, vals)

    init = (s, jnp.full((n_rows, lanes), float(k), F32), jnp.zeros((k, lanes), F32))
    _, rank, vals = lax.fori_loop(0, k, body, init)
    return rank, vals


RANK_CODE_SCALE = 2.0 ** 123
RANK_CODE_LIMIT = -float(PEER_TOPK) * RANK_CODE_SCALE


def _topk_quick_pair(s1, s2, k):
    c1, c2, v1, v2 = s1, s2, [], []
    for a in range(k):
        code = -float(a + k) * RANK_CODE_SCALE
        m1 = jnp.max(c1, axis=0, keepdims=True)
        m2 = jnp.max(c2, axis=0, keepdims=True)
        c1 = jnp.where(c1 == m1, code, c1)
        c2 = jnp.where(c2 == m2, code, c2)
        v1.append(m1)
        v2.append(m2)
    v1 = jnp.concatenate(v1, axis=0)
    v2 = jnp.concatenate(v2, axis=0)

    def decode(c, s):
        marked = c <= RANK_CODE_LIMIT
        rank = jnp.where(marked, c * (-1.0 / RANK_CODE_SCALE) - float(k), float(k))
        n_marked = jnp.sum(jnp.where(marked, 1.0, 0.0), axis=0, keepdims=True)
        in_range = jnp.min(s, axis=0, keepdims=True) > RANK_CODE_LIMIT
        return rank, jnp.where(n_marked == float(k), jnp.where(in_range, 1.0, 0.0), 0.0)

    rank1, ok1 = decode(c1, s1)
    rank2, ok2 = decode(c2, s2)
    return rank1, v1, rank2, v2, ok1 * ok2


def _cand_row_lengths(k):
    return [k // (a + 1) for a in range(k)]


def _candidates(v1, v2, k):
    lengths = _cand_row_lengths(k)
    rows = [v1[a:a + 1] + v2[:n] for a, n in enumerate(lengths)]
    pad = -sum(lengths) % SUBLANES
    if pad:
        rows.append(jnp.full((pad, v1.shape[1]), -jnp.inf, F32))
    return jnp.concatenate(rows, axis=0)


def _choose_exact(cand, k):
    n_cand, lanes = cand.shape
    crow = lax.broadcasted_iota(jnp.int32, (n_cand, lanes), 0).astype(F32)

    def body(_, c):
        m = jnp.max(c, axis=0, keepdims=True)
        first = jnp.min(jnp.where(c == m, crow, float(n_cand)), axis=0, keepdims=True)
        return jnp.where(crow == first, -jnp.inf, c)

    return lax.fori_loop(0, k, body, cand)


def _choose_quick(cand, k):
    for _ in range(k):
        cand = jnp.where(cand == jnp.max(cand, axis=0, keepdims=True), -jnp.inf, cand)
    return cand


def _pack_rows(x):
    return pltpu.bitcast(x, jnp.uint32)


def _unpack_rows(x):
    return pltpu.bitcast(x, BF16)


def _route_outputs(s1, s2, rank1, v1, rank2, v2, cand, left, k):
    chosen = jnp.where(left == -jnp.inf, jnp.where(cand > -jnp.inf, 1.0, 0.0), 0.0)
    top = v1[0:1] + v2[0:1]
    z = jnp.sum(chosen * jnp.exp(cand - top), axis=0, keepdims=True)
    lengths = _cand_row_lengths(k)
    offsets = [sum(lengths[:a]) for a in range(k)]
    counts = [jnp.sum(chosen[offsets[a]:offsets[a] + lengths[a]], axis=0, keepdims=True) for a in range(k)]
    nonempty = sum(counts[a] if lengths[a] == 1 else jnp.minimum(counts[a], 1.0) for a in range(k))
    n1 = jnp.where(rank1 < nonempty, 1.0, 0.0)
    for a in range(k):
        if lengths[a] > 1:
            n1 = jnp.where(rank1 == float(a), counts[a], n1)
    e1 = jnp.exp(s1 - v1[0:1])
    r2 = _pack_rows(rank2.astype(BF16))
    e2 = _pack_rows((jnp.exp(s2 - v2[0:1]) / z).astype(BF16))
    return (n1, e1, r2, e2), jnp.sum(chosen, axis=0, keepdims=True)


def _route_head(hh, s1, s2, out_refs):
    k = PEER_TOPK
    tiles = [_topk_quick_pair(s1[:, lt * LANE_TILE:(lt + 1) * LANE_TILE],
                              s2[:, lt * LANE_TILE:(lt + 1) * LANE_TILE], k)
             for lt in range(s1.shape[1] // LANE_TILE)]
    rank1, v1, rank2, v2, ok = (jnp.concatenate(parts, axis=1) for parts in zip(*tiles))
    cand = _candidates(v1, v2, k)
    outs, n_chosen = _route_outputs(s1, s2, rank1, v1, rank2, v2, cand, _choose_quick(cand, k), k)
    for ref, val in zip(out_refs, outs):
        ref[hh] = val
    ok = ok * jnp.where(n_chosen == float(k), 1.0, 0.0)

    @pl.when(jnp.min(ok) < 1.0)
    def _():
        for lt in range(s1.shape[1] // LANE_TILE):
            lanes = slice(lt * LANE_TILE, (lt + 1) * LANE_TILE)

            @pl.when(jnp.min(ok[:, lanes]) < 1.0)
            def _():
                t1, t2 = s1[:, lanes], s2[:, lanes]
                rank1, v1 = _topk_exact(t1, k)
                rank2, v2 = _topk_exact(t2, k)
                cand = _candidates(v1, v2, k)
                outs, _ = _route_outputs(t1, t2, rank1, v1, rank2, v2, cand, _choose_exact(cand, k), k)
                for ref, val in zip(out_refs, outs):
                    ref[hh, :, lanes] = val


def _route_kernel(hnt_ref, wq_ref, k1_ref, k2_ref, n1_ref, e1_ref, r2_ref, e2_ref):
    kd = PEER_KEY_DIM
    q_t = jnp.dot(wq_ref[...], hnt_ref[...], preferred_element_type=F32).astype(BF16)
    scores = [(jnp.dot(k1_ref[hh], q_t[2 * hh * kd:(2 * hh + 1) * kd], preferred_element_type=F32),
               jnp.dot(k2_ref[hh], q_t[(2 * hh + 1) * kd:(2 * hh + 2) * kd], preferred_element_type=F32))
              for hh in range(k1_ref.shape[0])]
    for hh, (s1, s2) in enumerate(scores):
        _route_head(hh, s1, s2, (n1_ref, e1_ref, r2_ref, e2_ref))


def _route(hnt, wq_t, keys1, keys2):
    t = hnt.shape[1]
    tb = min(ROUTE_TOKENS, t)
    hs = ROUTE_HEADS
    out = jax.ShapeDtypeStruct((PEER_HEADS, PEER_N_KEYS, t), F32)
    out16 = jax.ShapeDtypeStruct((PEER_HEADS, PEER_N_KEYS // 2, t), jnp.uint32)
    pspec = pl.BlockSpec((hs, PEER_N_KEYS // 2, tb), lambda i, h: (h, 0, i))
    ospec = pl.BlockSpec((hs, PEER_N_KEYS, tb), lambda i, h: (h, 0, i))
    kspec = pl.BlockSpec((hs, PEER_N_KEYS, PEER_KEY_DIM), lambda i, h: (h, 0, 0))
    return pl.pallas_call(
        _route_kernel,
        grid=(t // tb, PEER_HEADS // hs),
        in_specs=[pl.BlockSpec((D_MODEL, tb), lambda i, h: (0, i)),
                  pl.BlockSpec((hs * 2 * PEER_KEY_DIM, D_MODEL), lambda i, h: (h, 0)),
                  kspec, kspec],
        out_specs=(ospec, ospec, pspec, pspec),
        out_shape=(out, out, out16, out16),
        compiler_params=_params(("parallel", "arbitrary")),
        name="route",
    )(hnt, wq_t, keys1, keys2)


def _peer_gates(n1_ref, e1_ref, r2_ref, e2_ref, firsts, act, d_ref, first0):
    nk = PEER_N_KEYS
    for lt in range(d_ref.shape[1] // LANE_TILE):
        lanes = slice(lt * LANE_TILE, (lt + 1) * LANE_TILE)
        ws = [None] * len(firsts)
        for h in range(PEER_HEADS):
            keys = slice(h * nk // 2, (h + 1) * nk // 2)
            r2 = _unpack_rows(r2_ref[keys, lanes])
            e2 = _unpack_rows(e2_ref[keys, lanes])
            for k, ii in enumerate(firsts):
                n1 = n1_ref[h, ii:ii + 1, lanes].astype(BF16)
                e1 = e1_ref[h, ii:ii + 1, lanes].astype(BF16)
                term = jnp.where(r2 < n1, e2, 0.0) * e1
                ws[k] = term if ws[k] is None else ws[k] + term
        for k, ii in enumerate(firsts):
            a = act[k * nk:(k + 1) * nk, lanes].astype(BF16)
            gelu = (0.5 * a) * (1.0 + lax.erf(a * (2.0 ** -0.5)))
            d_ref[(ii - first0) * nk // 2:(ii - first0 + 1) * nk // 2, lanes] = _pack_rows(ws[k] * gelu)


def _peer_kernel(hnt_ref, u_ref, vt_ref, n1_ref, e1_ref, r2_ref, e2_ref, x1_ref, g_ref,
                 o_ref, acc_ref, *d_refs):
    c = pl.program_id(1)
    nk = PEER_N_KEYS
    sub = PEER_SUB_CHUNK
    piece = sub // PEER_MXU_SPLIT
    n_sub = u_ref.shape[0] // sub

    @pl.when(c == 0)
    def _():
        acc_ref[...] = jnp.zeros_like(acc_ref)

    acts = {}

    def first(s):
        for p in range(PEER_MXU_SPLIT):
            rows = slice(s * sub + p * piece, s * sub + (p + 1) * piece)
            acts[s, p] = rows, jnp.dot(u_ref[rows, :], hnt_ref[...], preferred_element_type=F32)

    def gate_and_second(s):
        for p in range(PEER_MXU_SPLIT):
            rows, act = acts.pop((s, p))
            firsts = range(rows.start // nk, rows.stop // nk)
            _peer_gates(n1_ref, e1_ref, r2_ref, e2_ref, firsts, act, d_refs[s], s * sub // nk)
        d_t = _unpack_rows(d_refs[s][...])
        acc_ref[...] += jnp.dot(vt_ref[:, s * sub:(s + 1) * sub], d_t, preferred_element_type=F32)

    first(0)
    for s in range(n_sub):
        if s + 1 < n_sub:
            first(s + 1)
        gate_and_second(s)

    @pl.when(c == pl.num_programs(1) - 1)
    def _():
        y = x1_ref[...] + jnp.transpose(acc_ref[...])
        o_ref[...] = _rms(y, g_ref[...])


def _peer(hnt, u, v_t, n1, e1, r2, e2, x1, g):
    t = x1.shape[0]
    tb = min(PEER_TOKENS, t)
    ec = PEER_EXPERT_CHUNK
    packed_rows = PEER_HEADS * PEER_N_KEYS // 2
    dense = pl.BlockSpec((packed_rows, tb), lambda i, c: (0, i))
    r2 = r2.reshape(packed_rows, t)
    e2 = e2.reshape(packed_rows, t)
    rows = pl.BlockSpec((PEER_HEADS, ec // PEER_N_KEYS, tb), lambda i, c: (0, c, i))
    return pl.pallas_call(
        _peer_kernel,
        grid=(t // tb, PEER_N_EXPERTS // ec),
        in_specs=[pl.BlockSpec((D_MODEL, tb), lambda i, c: (0, i)),
                  pl.BlockSpec((ec, D_MODEL), lambda i, c: (c, 0)),
                  pl.BlockSpec((D_MODEL, ec), lambda i, c: (0, c)),
                  rows, rows, dense, dense,
                  pl.BlockSpec((tb, D_MODEL), lambda i, c: (i, 0)),
                  pl.BlockSpec((1, D_MODEL), lambda i, c: (0, 0))],
        out_specs=pl.BlockSpec((tb, D_MODEL), lambda i, c: (i, 0)),
        out_shape=jax.ShapeDtypeStruct((t, D_MODEL), F32),
        scratch_shapes=([pltpu.VMEM((D_MODEL, tb), F32)]
                        + [pltpu.VMEM((PEER_SUB_CHUNK // 2, tb), jnp.uint32)] * (ec // PEER_SUB_CHUNK)),
        compiler_params=_params(("parallel", "arbitrary")),
        name="peer",
    )(hnt, u, v_t, n1, e1, r2, e2, x1, g)


def kernel(x, norm_mix_g, w_in, attn_sinks, hgrn_lb_logits, hgrn_norm_g, w_attn_proj, w_hgrn_proj,
           w_out, norm_ffn_g, w_peer_q, peer_keys, peer_u, peer_v, norm_final_g):
    b, s, d = x.shape
    assert d == D_MODEL and norm_mix_g.shape[0] == 1
    t = b * s
    x2 = x.reshape(t, d)
    row = lambda a: a.reshape(1, -1).astype(F32)

    aq, ak, av, hq, hf, hi, hg, ga, gh = _inproj(x2, row(norm_mix_g[0]), w_in[0].astype(BF16))
    y_attn = _attention(aq, ak, av, attn_sinks[0].astype(F32), b, s)
    y_hgrn = _hgrn(hq, hf, hi, hg, hgrn_lb_logits.astype(F32), row(hgrn_norm_g[0]), b, s)
    x1, hnt = _merge(x2, y_attn, y_hgrn, ga, gh, w_attn_proj[0].astype(BF16),
                     w_hgrn_proj[0].astype(BF16), w_out[0].astype(BF16), row(norm_ffn_g[0]))
    wq_t = jnp.transpose(w_peer_q[0]).astype(BF16)
    keys = peer_keys[0].astype(BF16)
    n1, e1, r2, e2 = _route(hnt, wq_t, keys[0], keys[1])
    out = _peer(hnt, peer_u[0].astype(BF16), jnp.transpose(peer_v[0]).astype(BF16),
                n1, e1, r2, e2, x1, row(norm_final_g))
    return out.reshape(b, s, d)
```

```python
import jax
import jax.numpy as jnp
from jax import lax
from jax.experimental import pallas as pl
from jax.experimental.pallas import tpu as pltpu

F32 = jnp.float32
BF16 = jnp.bfloat16

D_MODEL = 1024
ATTN_HEADS = 8
ATTN_KV_HEADS = 2
ATTN_GROUP = ATTN_HEADS // ATTN_KV_HEADS
ATTN_HEAD_DIM = 64
ATTN_WIDTH = ATTN_HEADS * ATTN_HEAD_DIM
KV_WIDTH = ATTN_KV_HEADS * ATTN_HEAD_DIM
WINDOW = 128
ATTN_BLOCK = 128
HGRN_HEADS = 4
HGRN_DK = 128
HGRN_DV = 128
HGRN_WIDTH = HGRN_HEADS * HGRN_DK
HGRN_CHUNK = 32
PEER_HEADS = 8
PEER_N_KEYS = 128
PEER_N_EXPERTS = PEER_N_KEYS * PEER_N_KEYS
PEER_KEY_DIM = 128
PEER_TOPK = 16
EPS = 1e-6
MASK_VALUE = -1e30

SPLIT_SIZES = (ATTN_WIDTH, KV_WIDTH, KV_WIDTH, HGRN_WIDTH, HGRN_WIDTH,
               HGRN_WIDTH, HGRN_WIDTH, D_MODEL, D_MODEL)
IN_WIDTH = sum(SPLIT_SIZES)

VMEM_LIMIT_BYTES = 56 * 1024 * 1024

INPROJ_TOKENS = 512
HGRN_TOKENS = 256
HGRN_BATCH_ROWS = 2
MERGE_TOKENS = 512
ROUTE_TOKENS = 512
ROUTE_HEADS = 4
PEER_TOKENS = 512
PEER_EXPERT_CHUNK = 4096
PEER_SUB_CHUNK = 1024
LANE_TILE = 128
SUBLANES = 8
PEER_MXU_SPLIT = 4


def _params(semantics):
    return pltpu.CompilerParams(dimension_semantics=semantics,
                                vmem_limit_bytes=VMEM_LIMIT_BYTES)


def _nt_dot(a, b):
    return lax.dot_general(a, b, (((1,), (1,)), ((), ())), preferred_element_type=F32)


def _rms(x, g):
    ms = jnp.mean(x * x, axis=-1, keepdims=True)
    return x * lax.rsqrt(ms + EPS) * g


def _inproj_kernel(x_ref, g_ref, w_ref, *out_refs):
    h = _rms(x_ref[...], g_ref[...]).astype(BF16)
    off = 0
    for o_ref, width in zip(out_refs, SPLIT_SIZES):
        o_ref[...] = jnp.dot(h, w_ref[:, off:off + width],
                             preferred_element_type=F32).astype(o_ref.dtype)
        off += width


def _inproj(x2, g, w_in):
    t = x2.shape[0]
    tm = min(INPROJ_TOKENS, t)
    out_shape = tuple(jax.ShapeDtypeStruct((t, w), BF16) for w in SPLIT_SIZES)
    return pl.pallas_call(
        _inproj_kernel,
        grid=(t // tm,),
        in_specs=[pl.BlockSpec((tm, D_MODEL), lambda i: (i, 0)),
                  pl.BlockSpec((1, D_MODEL), lambda i: (0, 0)),
                  pl.BlockSpec((D_MODEL, IN_WIDTH), lambda i: (0, 0))],
        out_specs=tuple(pl.BlockSpec((tm, w), lambda i: (i, 0)) for w in SPLIT_SIZES),
        out_shape=out_shape,
        compiler_params=_params(("parallel",)),
        name="inproj",
    )(x2, g, w_in)


def _alibi_slopes():
    return [float(2.0 ** (-8.0 * h / ATTN_HEADS)) for h in range(1, ATTN_HEADS + 1)]


def _attn_kernel(sink_ref, q_ref, kp_ref, kc_ref, vp_ref, vc_ref, o_ref):
    n = pl.program_id(1)
    w = ATTN_BLOCK
    q = q_ref[0]
    k = jnp.concatenate([kp_ref[0], kc_ref[0]], axis=0)
    v = jnp.concatenate([vp_ref[0], vc_ref[0]], axis=0)
    qpos = lax.broadcasted_iota(jnp.int32, (w, 2 * w), 0) + w
    kpos = lax.broadcasted_iota(jnp.int32, (w, 2 * w), 1)
    dist = qpos - kpos
    first_ok = jnp.where(n > 0, 0, w)
    in_window = jnp.where(dist >= 0, jnp.where(dist < WINDOW, 1, 0), 0)
    valid = jnp.where(kpos >= first_ok, in_window, 0) == 1
    distf = dist.astype(F32)
    scale = ATTN_HEAD_DIM ** -0.5
    slopes = _alibi_slopes()
    outs = []
    scores = []
    for kh in range(ATTN_KV_HEADS):
        ksl = k[:, kh * ATTN_HEAD_DIM:(kh + 1) * ATTN_HEAD_DIM]
        qg = jnp.concatenate(
            [q[:, (kh * ATTN_GROUP + g) * ATTN_HEAD_DIM:(kh * ATTN_GROUP + g + 1) * ATTN_HEAD_DIM]
             for g in range(ATTN_GROUP)], axis=0)
        scores.append(_nt_dot(qg, ksl) * scale)
    for kh in range(ATTN_KV_HEADS):
        vsl = v[:, kh * ATTN_HEAD_DIM:(kh + 1) * ATTN_HEAD_DIM]
        s_all = scores[kh]
        ps, denoms = [], []
        for g in range(ATTN_GROUP):
            head = kh * ATTN_GROUP + g
            s = s_all[g * w:(g + 1) * w] - slopes[head] * distf
            s = jnp.where(valid, s, MASK_VALUE)
            sink = sink_ref[head]
            m = jnp.maximum(jnp.max(s, axis=-1, keepdims=True), sink)
            p = jnp.exp(s - m)
            denoms.append(jnp.sum(p, axis=-1, keepdims=True) + jnp.exp(sink - m))
            ps.append(p.astype(BF16))
        o_all = jnp.dot(jnp.concatenate(ps, axis=0), vsl, preferred_element_type=F32)
        for g in range(ATTN_GROUP):
            outs.append(o_all[g * w:(g + 1) * w] / denoms[g])
    o_ref[0] = jnp.concatenate(outs, axis=-1).astype(o_ref.dtype)


def _attention(aq, ak, av, sinks, b, s):
    nb = s // ATTN_BLOCK
    q3 = aq.reshape(b, s, ATTN_WIDTH)
    k3 = ak.reshape(b, s, KV_WIDTH)
    v3 = av.reshape(b, s, KV_WIDTH)
    cur = lambda i, j: (i, j, 0)
    prev = lambda i, j: (i, jnp.maximum(j - 1, 0), 0)
    out = pl.pallas_call(
        _attn_kernel,
        grid=(b, nb),
        in_specs=[pl.BlockSpec(memory_space=pltpu.SMEM),
                  pl.BlockSpec((1, ATTN_BLOCK, ATTN_WIDTH), cur),
                  pl.BlockSpec((1, ATTN_BLOCK, KV_WIDTH), prev),
                  pl.BlockSpec((1, ATTN_BLOCK, KV_WIDTH), cur),
                  pl.BlockSpec((1, ATTN_BLOCK, KV_WIDTH), prev),
                  pl.BlockSpec((1, ATTN_BLOCK, KV_WIDTH), cur)],
        out_specs=pl.BlockSpec((1, ATTN_BLOCK, ATTN_WIDTH), cur),
        out_shape=jax.ShapeDtypeStruct((b, s, ATTN_WIDTH), BF16),
        compiler_params=_params(("parallel", "parallel")),
        name="attn",
    )(sinks, q3, k3, k3, v3, v3)
    return out.reshape(b * s, ATTN_WIDTH)


def _split3(x):
    hi = x.astype(BF16)
    r1 = x - hi.astype(F32)
    mid = r1.astype(BF16)
    lo = (r1 - mid.astype(F32)).astype(BF16)
    return hi, mid, lo


def _hgrn_kernel(lb_ref, gn_ref, hq_ref, hf_ref, hi_ref, hg_ref, o_ref, st_ref, o_scr, ds_scr):
    n_batch, sb = hq_ref.shape[0], hq_ref.shape[1]
    c = HGRN_CHUNK

    @pl.when(pl.program_id(1) == 0)
    def _():
        st_ref[...] = jnp.zeros_like(st_ref)

    logits = lb_ref[...]
    e = jnp.exp(logits - jnp.max(logits, axis=0, keepdims=True))
    lb = e[0:1] / jnp.sum(e, axis=0, keepdims=True)

    r = lax.broadcasted_iota(jnp.int32, (sb, sb), 0)
    cc = lax.broadcasted_iota(jnp.int32, (sb, sb), 1)
    same = (r // c) == (cc // c)
    tri = jnp.where(same, jnp.where(cc <= r, 1.0, 0.0), 0.0).astype(BF16)
    blk = jnp.where(same, 1.0, 0.0).astype(BF16)
    tr = lax.broadcasted_iota(jnp.int32, (c, c), 0)
    tc = lax.broadcasted_iota(jnp.int32, (c, c), 1)
    causal = tc <= tr

    def prepare(bb):
        f = lb + (1.0 - lb) * jax.nn.sigmoid(hf_ref[bb].astype(F32))
        kk = 1.0 - f
        parts = _split3(jnp.log(f))
        bcum = sum(jnp.dot(tri, p, preferred_element_type=F32) for p in parts)
        blast = sum(jnp.dot(blk, p, preferred_element_type=F32) for p in parts)
        hq = hq_ref[bb].astype(F32)
        q_dec = (hq * jax.nn.sigmoid(hq) * jnp.exp(bcum)).astype(BF16)
        k_inv = (kk * jnp.exp(-bcum)).astype(BF16)
        k_end = (kk * jnp.exp(blast - bcum)).astype(BF16)
        vv = hi_ref[bb]
        v_t = jnp.transpose(vv.astype(F32)).astype(BF16)
        return q_dec, k_inv, k_end, jnp.exp(blast), vv, v_t

    rows_of_batch = [prepare(bb) for bb in range(n_batch)]
    n_chunks = sb // c
    units = [(ci, bb, h) for ci in range(n_chunks) for bb in range(n_batch) for h in range(HGRN_HEADS)]
    rows = lambda ci: slice(ci * c, (ci + 1) * c)
    lanes = lambda h: slice(h * HGRN_DK, (h + 1) * HGRN_DK)

    intra = {}
    for ci, bb, h in units:
        q_dec, k_inv = rows_of_batch[bb][0], rows_of_batch[bb][1]
        a = _nt_dot(q_dec[rows(ci), lanes(h)], k_inv[rows(ci), lanes(h)])
        intra[ci, bb, h] = jnp.where(causal, a, 0.0).astype(BF16)
    for ci, bb, h in units:
        k_end, v_t = rows_of_batch[bb][2], rows_of_batch[bb][5]
        ds_scr[ci, bb, h] = jnp.dot(v_t[lanes(h), rows(ci)], k_end[rows(ci), lanes(h)],
                                    preferred_element_type=F32)
    for ci, bb, h in units:
        vv = rows_of_batch[bb][4]
        o_scr[bb, rows(ci), lanes(h)] = jnp.dot(intra.pop((ci, bb, h)), vv[rows(ci), lanes(h)],
                                                preferred_element_type=F32)

    for ci in range(n_chunks):
        states = {}
        for bb in range(n_batch):
            q_dec = rows_of_batch[bb][0]
            for h in range(HGRN_HEADS):
                states[bb, h] = st = st_ref[bb, h]
                o_scr[bb, rows(ci), lanes(h)] += _nt_dot(q_dec[rows(ci), lanes(h)], st.astype(BF16))
        for bb in range(n_batch):
            decay = rows_of_batch[bb][3]
            for h in range(HGRN_HEADS):
                st_ref[bb, h] = states[bb, h] * decay[ci * c:ci * c + 1, lanes(h)] + ds_scr[ci, bb, h]

    for bb in range(n_batch):
        hg = hg_ref[bb].astype(F32)
        gate = hg * jax.nn.sigmoid(hg)
        ys = []
        for h in range(HGRN_HEADS):
            lanes = slice(h * HGRN_DV, (h + 1) * HGRN_DV)
            ys.append(_rms(o_scr[bb, :, lanes], gn_ref[...]) * gate[:, lanes])
        o_ref[bb] = jnp.concatenate(ys, axis=-1).astype(o_ref.dtype)


def _hgrn(hq, hf, hi, hg, lb_logits, gn, b, s):
    sb = min(HGRN_TOKENS, s)
    nb = HGRN_BATCH_ROWS if b % HGRN_BATCH_ROWS == 0 else 1
    spec = pl.BlockSpec((nb, sb, HGRN_WIDTH), lambda i, j: (i, j, 0))
    r3 = lambda a: a.reshape(b, s, HGRN_WIDTH)
    out = pl.pallas_call(
        _hgrn_kernel,
        grid=(b // nb, s // sb),
        in_specs=[pl.BlockSpec(lb_logits.shape, lambda i, j: (0, 0)),
                  pl.BlockSpec((1, HGRN_DV), lambda i, j: (0, 0)),
                  spec, spec, spec, spec],
        out_specs=spec,
        out_shape=jax.ShapeDtypeStruct((b, s, HGRN_WIDTH), BF16),
        scratch_shapes=[pltpu.VMEM((nb, HGRN_HEADS, HGRN_DV, HGRN_DK), F32),
                        pltpu.VMEM((nb, sb, HGRN_WIDTH), F32),
                        pltpu.VMEM((sb // HGRN_CHUNK, nb, HGRN_HEADS, HGRN_DV, HGRN_DK), F32)],
        compiler_params=_params(("parallel", "arbitrary")),
        name="hgrn",
    )(lb_logits, gn, r3(hq), r3(hf), r3(hi), r3(hg))
    return out.reshape(b * s, HGRN_WIDTH)


def _merge_kernel(x_ref, ya_ref, yh_ref, ga_ref, gh_ref, wa_ref, wh_ref, wo_ref, g_ref,
                  x1_ref, hnt_ref):
    pa = jnp.dot(ya_ref[...], wa_ref[...], preferred_element_type=F32)
    ph = jnp.dot(yh_ref[...], wh_ref[...], preferred_element_type=F32)
    merged = (jax.nn.sigmoid(ga_ref[...].astype(F32)) * pa
              + jax.nn.sigmoid(gh_ref[...].astype(F32)) * ph)
    x1 = x_ref[...] + jnp.dot(merged.astype(BF16), wo_ref[...], preferred_element_type=F32)
    x1_ref[...] = x1
    hnt_ref[...] = jnp.transpose(_rms(x1, g_ref[...])).astype(BF16)


def _merge(x2, ya, yh, ga, gh, wa, wh, wo, g):
    t = x2.shape[0]
    tm = min(MERGE_TOKENS, t)
    row = lambda w: pl.BlockSpec((tm, w), lambda i: (i, 0))
    full = lambda a: pl.BlockSpec(a.shape, lambda i: (0, 0))
    return pl.pallas_call(
        _merge_kernel,
        grid=(t // tm,),
        in_specs=[row(D_MODEL), row(ATTN_WIDTH), row(HGRN_WIDTH), row(D_MODEL), row(D_MODEL),
                  full(wa), full(wh), full(wo), full(g)],
        out_specs=(row(D_MODEL), pl.BlockSpec((D_MODEL, tm), lambda i: (0, i))),
        out_shape=(jax.ShapeDtypeStruct((t, D_MODEL), F32),
                   jax.ShapeDtypeStruct((D_MODEL, t), BF16)),
        compiler_params=_params(("parallel",)),
        name="merge",
    )(x2, ya, yh, ga, gh, wa, wh, wo, g)


def _as_f32(i):
    return jnp.asarray(i, jnp.int32).astype(F32)


def _topk_exact(s, k):
    n_rows, lanes = s.shape
    row = lax.broadcasted_iota(jnp.int32, (n_rows, lanes), 0).astype(F32)
    kid = lax.broadcasted_iota(jnp.int32, (k, lanes), 0).astype(F32)

    def body(a, carry):
        s, rank, vals = carry
        af = _as_f32(a)
        m = jnp.max(s, axis=0, keepdims=True)
        first = jnp.min(jnp.where(s == m, row, float(n_rows)), axis=0, keepdims=True)
        sel = row == first
        return jnp.where(sel, -jnp.inf, s), jnp.where(sel, af, rank), jnp.where(kid == af, m, vals)

    init = (s, jnp.full((n_rows, lanes), float(k), F32), jnp.zeros((k, lanes), F32))
    _, rank, vals = lax.fori_loop(0, k, body, init)
    return rank, vals


RANK_CODE_SCALE = 2.0 ** 123
RANK_CODE_LIMIT = -float(PEER_TOPK) * RANK_CODE_SCALE


def _topk_quick_pair(s1, s2, k):
    c1, c2, v1, v2 = s1, s2, [], []
    for a in range(k):
        code = -float(a + k) * RANK_CODE_SCALE
        m1 = jnp.max(c1, axis=0, keepdims=True)
        m2 = jnp.max(c2, axis=0, keepdims=True)
        c1 = jnp.where(c1 == m1, code, c1)
        c2 = jnp.where(c2 == m2, code, c2)
        v1.append(m1)
        v2.append(m2)
    v1 = jnp.concatenate(v1, axis=0)
    v2 = jnp.concatenate(v2, axis=0)

    def decode(c, s):
        marked = c <= RANK_CODE_LIMIT
        rank = jnp.where(marked, c * (-1.0 / RANK_CODE_SCALE) - float(k), float(k))
        n_marked = jnp.sum(jnp.where(marked, 1.0, 0.0), axis=0, keepdims=True)
        in_range = jnp.min(s, axis=0, keepdims=True) > RANK_CODE_LIMIT
        return rank, jnp.where(n_marked == float(k), jnp.where(in_range, 1.0, 0.0), 0.0)

    rank1, ok1 = decode(c1, s1)
    rank2, ok2 = decode(c2, s2)
    return rank1, v1, rank2, v2, ok1 * ok2


def _cand_row_lengths(k):
    return [k // (a + 1) for a in range(k)]


def _candidates(v1, v2, k):
    lengths = _cand_row_lengths(k)
    rows = [v1[a:a + 1] + v2[:n] for a, n in enumerate(lengths)]
    pad = -sum(lengths) % SUBLANES
    if pad:
        rows.append(jnp.full((pad, v1.shape[1]), -jnp.inf, F32))
    return jnp.concatenate(rows, axis=0)


def _choose_exact(cand, k):
    n_cand, lanes = cand.shape
    crow = lax.broadcasted_iota(jnp.int32, (n_cand, lanes), 0).astype(F32)

    def body(_, c):
        m = jnp.max(c, axis=0, keepdims=True)
        first = jnp.min(jnp.where(c == m, crow, float(n_cand)), axis=0, keepdims=True)
        return jnp.where(crow == first, -jnp.inf, c)

    return lax.fori_loop(0, k, body, cand)


def _choose_quick(cand, k):
    for _ in range(k):
        cand = jnp.where(cand == jnp.max(cand, axis=0, keepdims=True), -jnp.inf, cand)
    return cand


def _pack_rows(x):
    return pltpu.bitcast(x, jnp.uint32)


def _unpack_rows(x):
    return pltpu.bitcast(x, BF16)


def _route_outputs(s1, s2, rank1, v1, rank2, v2, cand, left, k):
    chosen = jnp.where(left == -jnp.inf, jnp.where(cand > -jnp.inf, 1.0, 0.0), 0.0)
    top = v1[0:1] + v2[0:1]
    z = jnp.sum(chosen * jnp.exp(cand - top), axis=0, keepdims=True)
    lengths = _cand_row_lengths(k)
    offsets = [sum(lengths[:a]) for a in range(k)]
    counts = [jnp.sum(chosen[offsets[a]:offsets[a] + lengths[a]], axis=0, keepdims=True) for a in range(k)]
    nonempty = sum(counts[a] if lengths[a] == 1 else jnp.minimum(counts[a], 1.0) for a in range(k))
    n1 = jnp.where(rank1 < nonempty, 1.0, 0.0)
    for a in range(k):
        if lengths[a] > 1:
            n1 = jnp.where(rank1 == float(a), counts[a], n1)
    e1 = jnp.exp(s1 - v1[0:1])
    r2 = _pack_rows(rank2.astype(BF16))
    e2 = _pack_rows((jnp.exp(s2 - v2[0:1]) / z).astype(BF16))
    return (n1, e1, r2, e2), jnp.sum(chosen, axis=0, keepdims=True)


def _route_head(hh, s1, s2, out_refs):
    k = PEER_TOPK
    tiles = [_topk_quick_pair(s1[:, lt * LANE_TILE:(lt + 1) * LANE_TILE],
                              s2[:, lt * LANE_TILE:(lt + 1) * LANE_TILE], k)
             for lt in range(s1.shape[1] // LANE_TILE)]
    rank1, v1, rank2, v2, ok = (jnp.concatenate(parts, axis=1) for parts in zip(*tiles))
    cand = _candidates(v1, v2, k)
    outs, n_chosen = _route_outputs(s1, s2, rank1, v1, rank2, v2, cand, _choose_quick(cand, k), k)
    for ref, val in zip(out_refs, outs):
        ref[hh] = val
    ok = ok * jnp.where(n_chosen == float(k), 1.0, 0.0)

    @pl.when(jnp.min(ok) < 1.0)
    def _():
        for lt in range(s1.shape[1] // LANE_TILE):
            lanes = slice(lt * LANE_TILE, (lt + 1) * LANE_TILE)

            @pl.when(jnp.min(ok[:, lanes]) < 1.0)
            def _():
                t1, t2 = s1[:, lanes], s2[:, lanes]
                rank1, v1 = _topk_exact(t1, k)
                rank2, v2 = _topk_exact(t2, k)
                cand = _candidates(v1, v2, k)
                outs, _ = _route_outputs(t1, t2, rank1, v1, rank2, v2, cand, _choose_exact(cand, k), k)
                for ref, val in zip(out_refs, outs):
                    ref[hh, :, lanes] = val


def _route_kernel(hnt_ref, wq_ref, k1_ref, k2_ref, n1_ref, e1_ref, r2_ref, e2_ref):
    kd = PEER_KEY_DIM
    q_t = jnp.dot(wq_ref[...], hnt_ref[...], preferred_element_type=F32).astype(BF16)
    scores = [(jnp.dot(k1_ref[hh], q_t[2 * hh * kd:(2 * hh + 1) * kd], preferred_element_type=F32),
               jnp.dot(k2_ref[hh], q_t[(2 * hh + 1) * kd:(2 * hh + 2) * kd], preferred_element_type=F32))
              for hh in range(k1_ref.shape[0])]
    for hh, (s1, s2) in enumerate(scores):
        _route_head(hh, s1, s2, (n1_ref, e1_ref, r2_ref, e2_ref))


def _route(hnt, wq_t, keys1, keys2):
    t = hnt.shape[1]
    tb = min(ROUTE_TOKENS, t)
    hs = ROUTE_HEADS
    out = jax.ShapeDtypeStruct((PEER_HEADS, PEER_N_KEYS, t), F32)
    out16 = jax.ShapeDtypeStruct((PEER_HEADS, PEER_N_KEYS // 2, t), jnp.uint32)
    pspec = pl.BlockSpec((hs, PEER_N_KEYS // 2, tb), lambda i, h: (h, 0, i))
    ospec = pl.BlockSpec((hs, PEER_N_KEYS, tb), lambda i, h: (h, 0, i))
    kspec = pl.BlockSpec((hs, PEER_N_KEYS, PEER_KEY_DIM), lambda i, h: (h, 0, 0))
    return pl.pallas_call(
        _route_kernel,
        grid=(t // tb, PEER_HEADS // hs),
        in_specs=[pl.BlockSpec((D_MODEL, tb), lambda i, h: (0, i)),
                  pl.BlockSpec((hs * 2 * PEER_KEY_DIM, D_MODEL), lambda i, h: (h, 0)),
                  kspec, kspec],
        out_specs=(ospec, ospec, pspec, pspec),
        out_shape=(out, out, out16, out16),
        compiler_params=_params(("parallel", "arbitrary")),
        name="route",
    )(hnt, wq_t, keys1, keys2)


def _peer_gates(n1_ref, e1_ref, r2_ref, e2_ref, firsts, act, d_ref, first0):
    nk = PEER_N_KEYS
    for lt in range(d_ref.shape[1] // LANE_TILE):
        lanes = slice(lt * LANE_TILE, (lt + 1) * LANE_TILE)
        ws = [None] * len(firsts)
        for h in range(PEER_HEADS):
            keys = slice(h * nk // 2, (h + 1) * nk // 2)
            r2 = _unpack_rows(r2_ref[keys, lanes])
            e2 = _unpack_rows(e2_ref[keys, lanes])
            for k, ii in enumerate(firsts):
                n1 = n1_ref[h, ii:ii + 1, lanes].astype(BF16)
                e1 = e1_ref[h, ii:ii + 1, lanes].astype(BF16)
                term = jnp.where(r2 < n1, e2, 0.0) * e1
                ws[k] = term if ws[k] is None else ws[k] + term
        for k, ii in enumerate(firsts):
            a = act[k * nk:(k + 1) * nk, lanes].astype(BF16)
            gelu = (0.5 * a) * (1.0 + lax.erf(a * (2.0 ** -0.5)))
            d_ref[(ii - first0) * nk // 2:(ii - first0 + 1) * nk // 2, lanes] = _pack_rows(ws[k] * gelu)


def _peer_kernel(hnt_ref, u_ref, vt_ref, n1_ref, e1_ref, r2_ref, e2_ref, x1_ref, g_ref,
                 o_ref, acc_ref, *d_refs):
    c = pl.program_id(1)
    nk = PEER_N_KEYS
    sub = PEER_SUB_CHUNK
    piece = sub // PEER_MXU_SPLIT
    n_sub = u_ref.shape[0] // sub

    @pl.when(c == 0)
    def _():
        acc_ref[...] = jnp.zeros_like(acc_ref)

    acts = {}

    def first(s):
        for p in range(PEER_MXU_SPLIT):
            rows = slice(s * sub + p * piece, s * sub + (p + 1) * piece)
            acts[s, p] = rows, jnp.dot(u_ref[rows, :], hnt_ref[...], preferred_element_type=F32)

    def gate_and_second(s):
        for p in range(PEER_MXU_SPLIT):
            rows, act = acts.pop((s, p))
            firsts = range(rows.start // nk, rows.stop // nk)
            _peer_gates(n1_ref, e1_ref, r2_ref, e2_ref, firsts, act, d_refs[s], s * sub // nk)
        d_t = _unpack_rows(d_refs[s][...])
        acc_ref[...] += jnp.dot(vt_ref[:, s * sub:(s + 1) * sub], d_t, preferred_element_type=F32)

    first(0)
    for s in range(n_sub):
        if s + 1 < n_sub:
            first(s + 1)
        gate_and_second(s)

    @pl.when(c == pl.num_programs(1) - 1)
    def _():
        y = x1_ref[...] + jnp.transpose(acc_ref[...])
        o_ref[...] = _rms(y, g_ref[...])


def _peer(hnt, u, v_t, n1, e1, r2, e2, x1, g):
    t = x1.shape[0]
    tb = min(PEER_TOKENS, t)
    ec = PEER_EXPERT_CHUNK
    packed_rows = PEER_HEADS * PEER_N_KEYS // 2
    dense = pl.BlockSpec((packed_rows, tb), lambda i, c: (0, i))
    r2 = r2.reshape(packed_rows, t)
    e2 = e2.reshape(packed_rows, t)
    rows = pl.BlockSpec((PEER_HEADS, ec // PEER_N_KEYS, tb), lambda i, c: (0, c, i))
    return pl.pallas_call(
        _peer_kernel,
        grid=(t // tb, PEER_N_EXPERTS // ec),
        in_specs=[pl.BlockSpec((D_MODEL, tb), lambda i, c: (0, i)),
                  pl.BlockSpec((ec, D_MODEL), lambda i, c: (c, 0)),
                  pl.BlockSpec((D_MODEL, ec), lambda i, c: (0, c)),
                  rows, rows, dense, dense,
                  pl.BlockSpec((tb, D_MODEL), lambda i, c: (i, 0), pipeline_mode=pl.Buffered(1)),
                  pl.BlockSpec((1, D_MODEL), lambda i, c: (0, 0))],
        out_specs=pl.BlockSpec((tb, D_MODEL), lambda i, c: (i, 0)),
        out_shape=jax.ShapeDtypeStruct((t, D_MODEL), F32),
        scratch_shapes=([pltpu.VMEM((D_MODEL, tb), F32)]
                        + [pltpu.VMEM((PEER_SUB_CHUNK // 2, tb), jnp.uint32)] * (ec // PEER_SUB_CHUNK)),
        compiler_params=_params(("parallel", "arbitrary")),
        name="peer",
    )(hnt, u, v_t, n1, e1, r2, e2, x1, g)


def kernel(x, norm_mix_g, w_in, attn_sinks, hgrn_lb_logits, hgrn_norm_g, w_attn_proj, w_hgrn_proj,
           w_out, norm_ffn_g, w_peer_q, peer_keys, peer_u, peer_v, norm_final_g):
    b, s, d = x.shape
    assert d == D_MODEL and norm_mix_g.shape[0] == 1
    t = b * s
    x2 = x.reshape(t, d)
    row = lambda a: a.reshape(1, -1).astype(F32)

    aq, ak, av, hq, hf, hi, hg, ga, gh = _inproj(x2, row(norm_mix_g[0]), w_in[0].astype(BF16))
    y_attn = _attention(aq, ak, av, attn_sinks[0].astype(F32), b, s)
    y_hgrn = _hgrn(hq, hf, hi, hg, hgrn_lb_logits.astype(F32), row(hgrn_norm_g[0]), b, s)
    x1, hnt = _merge(x2, y_attn, y_hgrn, ga, gh, w_attn_proj[0].astype(BF16),
                     w_hgrn_proj[0].astype(BF16), w_out[0].astype(BF16), row(norm_ffn_g[0]))
    wq_t = jnp.transpose(w_peer_q[0]).astype(BF16)
    keys = peer_keys[0].astype(BF16)
    n1, e1, r2, e2 = _route(hnt, wq_t, keys[0], keys[1])
    out = _peer(hnt, peer_u[0].astype(BF16), jnp.transpose(peer_v[0]).astype(BF16),
                n1, e1, r2, e2, x1, row(norm_final_g))
    return out.reshape(b, s, d)
```

```python
import jax
import jax.numpy as jnp
from jax import lax
from jax.experimental import pallas as pl
from jax.experimental.pallas import tpu as pltpu

F32 = jnp.float32
BF16 = jnp.bfloat16

D_MODEL = 1024
ATTN_HEADS = 8
ATTN_KV_HEADS = 2
ATTN_GROUP = ATTN_HEADS // ATTN_KV_HEADS
ATTN_HEAD_DIM = 64
ATTN_WIDTH = ATTN_HEADS * ATTN_HEAD_DIM
KV_WIDTH = ATTN_KV_HEADS * ATTN_HEAD_DIM
WINDOW = 128
ATTN_BLOCK = 128
HGRN_HEADS = 4
HGRN_DK = 128
HGRN_DV = 128
HGRN_WIDTH = HGRN_HEADS * HGRN_DK
HGRN_CHUNK = 32
PEER_HEADS = 8
PEER_N_KEYS = 128
PEER_N_EXPERTS = PEER_N_KEYS * PEER_N_KEYS
PEER_KEY_DIM = 128
PEER_TOPK = 16
EPS = 1e-6
MASK_VALUE = -1e30

SPLIT_SIZES = (ATTN_WIDTH, KV_WIDTH, KV_WIDTH, HGRN_WIDTH, HGRN_WIDTH,
               HGRN_WIDTH, HGRN_WIDTH, D_MODEL, D_MODEL)
IN_WIDTH = sum(SPLIT_SIZES)

VMEM_LIMIT_BYTES = 56 * 1024 * 1024

INPROJ_TOKENS = 512
HGRN_TOKENS = 256
HGRN_BATCH_ROWS = 2
MERGE_TOKENS = 512
ROUTE_TOKENS = 512
ROUTE_HEADS = 4
PEER_TOKENS = 512
PEER_EXPERT_CHUNK = 2048
PEER_SUB_CHUNK = 1024
LANE_TILE = 128
SUBLANES = 8
PEER_MXU_SPLIT = 4


def _params(semantics):
    return pltpu.CompilerParams(dimension_semantics=semantics,
                                vmem_limit_bytes=VMEM_LIMIT_BYTES)


def _nt_dot(a, b):
    return lax.dot_general(a, b, (((1,), (1,)), ((), ())), preferred_element_type=F32)


def _rms(x, g):
    ms = jnp.mean(x * x, axis=-1, keepdims=True)
    return x * lax.rsqrt(ms + EPS) * g


def _inproj_kernel(x_ref, g_ref, w_ref, *out_refs):
    h = _rms(x_ref[...], g_ref[...]).astype(BF16)
    off = 0
    for o_ref, width in zip(out_refs, SPLIT_SIZES):
        o_ref[...] = jnp.dot(h, w_ref[:, off:off + width],
                             preferred_element_type=F32).astype(o_ref.dtype)
        off += width


def _inproj(x2, g, w_in):
    t = x2.shape[0]
    tm = min(INPROJ_TOKENS, t)
    out_shape = tuple(jax.ShapeDtypeStruct((t, w), BF16) for w in SPLIT_SIZES)
    return pl.pallas_call(
        _inproj_kernel,
        grid=(t // tm,),
        in_specs=[pl.BlockSpec((tm, D_MODEL), lambda i: (i, 0)),
                  pl.BlockSpec((1, D_MODEL), lambda i: (0, 0)),
                  pl.BlockSpec((D_MODEL, IN_WIDTH), lambda i: (0, 0))],
        out_specs=tuple(pl.BlockSpec((tm, w), lambda i: (i, 0)) for w in SPLIT_SIZES),
        out_shape=out_shape,
        compiler_params=_params(("parallel",)),
        name="inproj",
    )(x2, g, w_in)


def _alibi_slopes():
    return [float(2.0 ** (-8.0 * h / ATTN_HEADS)) for h in range(1, ATTN_HEADS + 1)]


def _attn_kernel(sink_ref, q_ref, kp_ref, kc_ref, vp_ref, vc_ref, o_ref):
    n = pl.program_id(1)
    w = ATTN_BLOCK
    q = q_ref[0]
    k = jnp.concatenate([kp_ref[0], kc_ref[0]], axis=0)
    v = jnp.concatenate([vp_ref[0], vc_ref[0]], axis=0)
    qpos = lax.broadcasted_iota(jnp.int32, (w, 2 * w), 0) + w
    kpos = lax.broadcasted_iota(jnp.int32, (w, 2 * w), 1)
    dist = qpos - kpos
    first_ok = jnp.where(n > 0, 0, w)
    in_window = jnp.where(dist >= 0, jnp.where(dist < WINDOW, 1, 0), 0)
    valid = jnp.where(kpos >= first_ok, in_window, 0) == 1
    distf = dist.astype(F32)
    scale = ATTN_HEAD_DIM ** -0.5
    slopes = _alibi_slopes()
    outs = []
    scores = []
    for kh in range(ATTN_KV_HEADS):
        ksl = k[:, kh * ATTN_HEAD_DIM:(kh + 1) * ATTN_HEAD_DIM]
        qg = jnp.concatenate(
            [q[:, (kh * ATTN_GROUP + g) * ATTN_HEAD_DIM:(kh * ATTN_GROUP + g + 1) * ATTN_HEAD_DIM]
             for g in range(ATTN_GROUP)], axis=0)
        scores.append(_nt_dot(qg, ksl) * scale)
    for kh in range(ATTN_KV_HEADS):
        vsl = v[:, kh * ATTN_HEAD_DIM:(kh + 1) * ATTN_HEAD_DIM]
        s_all = scores[kh]
        ps, denoms = [], []
        for g in range(ATTN_GROUP):
            head = kh * ATTN_GROUP + g
            s = s_all[g * w:(g + 1) * w] - slopes[head] * distf
            s = jnp.where(valid, s, MASK_VALUE)
            sink = sink_ref[head]
            m = jnp.maximum(jnp.max(s, axis=-1, keepdims=True), sink)
            p = jnp.exp(s - m)
            denoms.append(jnp.sum(p, axis=-1, keepdims=True) + jnp.exp(sink - m))
            ps.append(p.astype(BF16))
        o_all = jnp.dot(jnp.concatenate(ps, axis=0), vsl, preferred_element_type=F32)
        for g in range(ATTN_GROUP):
            outs.append(o_all[g * w:(g + 1) * w] / denoms[g])
    o_ref[0] = jnp.concatenate(outs, axis=-1).astype(o_ref.dtype)


def _attention(aq, ak, av, sinks, b, s):
    nb = s // ATTN_BLOCK
    q3 = aq.reshape(b, s, ATTN_WIDTH)
    k3 = ak.reshape(b, s, KV_WIDTH)
    v3 = av.reshape(b, s, KV_WIDTH)
    cur = lambda i, j: (i, j, 0)
    prev = lambda i, j: (i, jnp.maximum(j - 1, 0), 0)
    out = pl.pallas_call(
        _attn_kernel,
        grid=(b, nb),
        in_specs=[pl.BlockSpec(memory_space=pltpu.SMEM),
                  pl.BlockSpec((1, ATTN_BLOCK, ATTN_WIDTH), cur),
                  pl.BlockSpec((1, ATTN_BLOCK, KV_WIDTH), prev),
                  pl.BlockSpec((1, ATTN_BLOCK, KV_WIDTH), cur),
                  pl.BlockSpec((1, ATTN_BLOCK, KV_WIDTH), prev),
                  pl.BlockSpec((1, ATTN_BLOCK, KV_WIDTH), cur)],
        out_specs=pl.BlockSpec((1, ATTN_BLOCK, ATTN_WIDTH), cur),
        out_shape=jax.ShapeDtypeStruct((b, s, ATTN_WIDTH), BF16),
        compiler_params=_params(("parallel", "parallel")),
        name="attn",
    )(sinks, q3, k3, k3, v3, v3)
    return out.reshape(b * s, ATTN_WIDTH)


def _split3(x):
    hi = x.astype(BF16)
    r1 = x - hi.astype(F32)
    mid = r1.astype(BF16)
    lo = (r1 - mid.astype(F32)).astype(BF16)
    return hi, mid, lo


def _hgrn_kernel(lb_ref, gn_ref, hq_ref, hf_ref, hi_ref, hg_ref, o_ref, st_ref, o_scr, ds_scr):
    n_batch, sb = hq_ref.shape[0], hq_ref.shape[1]
    c = HGRN_CHUNK

    @pl.when(pl.program_id(1) == 0)
    def _():
        st_ref[...] = jnp.zeros_like(st_ref)

    logits = lb_ref[...]
    e = jnp.exp(logits - jnp.max(logits, axis=0, keepdims=True))
    lb = e[0:1] / jnp.sum(e, axis=0, keepdims=True)

    r = lax.broadcasted_iota(jnp.int32, (sb, sb), 0)
    cc = lax.broadcasted_iota(jnp.int32, (sb, sb), 1)
    same = (r // c) == (cc // c)
    tri = jnp.where(same, jnp.where(cc <= r, 1.0, 0.0), 0.0).astype(BF16)
    blk = jnp.where(same, 1.0, 0.0).astype(BF16)
    tr = lax.broadcasted_iota(jnp.int32, (c, c), 0)
    tc = lax.broadcasted_iota(jnp.int32, (c, c), 1)
    causal = tc <= tr

    def prepare(bb):
        f = lb + (1.0 - lb) * jax.nn.sigmoid(hf_ref[bb].astype(F32))
        kk = 1.0 - f
        parts = _split3(jnp.log(f))
        bcum = sum(jnp.dot(tri, p, preferred_element_type=F32) for p in parts)
        blast = sum(jnp.dot(blk, p, preferred_element_type=F32) for p in parts)
        hq = hq_ref[bb].astype(F32)
        q_dec = (hq * jax.nn.sigmoid(hq) * jnp.exp(bcum)).astype(BF16)
        k_inv = (kk * jnp.exp(-bcum)).astype(BF16)
        k_end = (kk * jnp.exp(blast - bcum)).astype(BF16)
        vv = hi_ref[bb]
        v_t = jnp.transpose(vv.astype(F32)).astype(BF16)
        return q_dec, k_inv, k_end, jnp.exp(blast), vv, v_t

    rows_of_batch = [prepare(bb) for bb in range(n_batch)]
    n_chunks = sb // c
    units = [(ci, bb, h) for ci in range(n_chunks) for bb in range(n_batch) for h in range(HGRN_HEADS)]
    rows = lambda ci: slice(ci * c, (ci + 1) * c)
    lanes = lambda h: slice(h * HGRN_DK, (h + 1) * HGRN_DK)

    intra = {}
    for ci, bb, h in units:
        q_dec, k_inv = rows_of_batch[bb][0], rows_of_batch[bb][1]
        a = _nt_dot(q_dec[rows(ci), lanes(h)], k_inv[rows(ci), lanes(h)])
        intra[ci, bb, h] = jnp.where(causal, a, 0.0).astype(BF16)
    for ci, bb, h in units:
        k_end, v_t = rows_of_batch[bb][2], rows_of_batch[bb][5]
        ds_scr[ci, bb, h] = jnp.dot(v_t[lanes(h), rows(ci)], k_end[rows(ci), lanes(h)],
                                    preferred_element_type=F32)
    for ci, bb, h in units:
        vv = rows_of_batch[bb][4]
        o_scr[bb, rows(ci), lanes(h)] = jnp.dot(intra.pop((ci, bb, h)), vv[rows(ci), lanes(h)],
                                                preferred_element_type=F32)

    for ci in range(n_chunks):
        states = {}
        for bb in range(n_batch):
            q_dec = rows_of_batch[bb][0]
            for h in range(HGRN_HEADS):
                states[bb, h] = st = st_ref[bb, h]
                o_scr[bb, rows(ci), lanes(h)] += _nt_dot(q_dec[rows(ci), lanes(h)], st.astype(BF16))
        for bb in range(n_batch):
            decay = rows_of_batch[bb][3]
            for h in range(HGRN_HEADS):
                st_ref[bb, h] = states[bb, h] * decay[ci * c:ci * c + 1, lanes(h)] + ds_scr[ci, bb, h]

    for bb in range(n_batch):
        hg = hg_ref[bb].astype(F32)
        gate = hg * jax.nn.sigmoid(hg)
        ys = []
        for h in range(HGRN_HEADS):
            lanes = slice(h * HGRN_DV, (h + 1) * HGRN_DV)
            ys.append(_rms(o_scr[bb, :, lanes], gn_ref[...]) * gate[:, lanes])
        o_ref[bb] = jnp.concatenate(ys, axis=-1).astype(o_ref.dtype)


def _hgrn(hq, hf, hi, hg, lb_logits, gn, b, s):
    sb = min(HGRN_TOKENS, s)
    nb = HGRN_BATCH_ROWS if b % HGRN_BATCH_ROWS == 0 else 1
    spec = pl.BlockSpec((nb, sb, HGRN_WIDTH), lambda i, j: (i, j, 0))
    r3 = lambda a: a.reshape(b, s, HGRN_WIDTH)
    out = pl.pallas_call(
        _hgrn_kernel,
        grid=(b // nb, s // sb),
        in_specs=[pl.BlockSpec(lb_logits.shape, lambda i, j: (0, 0)),
                  pl.BlockSpec((1, HGRN_DV), lambda i, j: (0, 0)),
                  spec, spec, spec, spec],
        out_specs=spec,
        out_shape=jax.ShapeDtypeStruct((b, s, HGRN_WIDTH), BF16),
        scratch_shapes=[pltpu.VMEM((nb, HGRN_HEADS, HGRN_DV, HGRN_DK), F32),
                        pltpu.VMEM((nb, sb, HGRN_WIDTH), F32),
                        pltpu.VMEM((sb // HGRN_CHUNK, nb, HGRN_HEADS, HGRN_DV, HGRN_DK), F32)],
        compiler_params=_params(("parallel", "arbitrary")),
        name="hgrn",
    )(lb_logits, gn, r3(hq), r3(hf), r3(hi), r3(hg))
    return out.reshape(b * s, HGRN_WIDTH)


def _merge_kernel(x_ref, ya_ref, yh_ref, ga_ref, gh_ref, wa_ref, wh_ref, wo_ref, g_ref,
                  x1_ref, hnt_ref):
    pa = jnp.dot(ya_ref[...], wa_ref[...], preferred_element_type=F32)
    ph = jnp.dot(yh_ref[...], wh_ref[...], preferred_element_type=F32)
    merged = (jax.nn.sigmoid(ga_ref[...].astype(F32)) * pa
              + jax.nn.sigmoid(gh_ref[...].astype(F32)) * ph)
    x1 = x_ref[...] + jnp.dot(merged.astype(BF16), wo_ref[...], preferred_element_type=F32)
    x1_ref[...] = x1
    hnt_ref[...] = _pack_rows(jnp.transpose(_rms(x1, g_ref[...])).astype(BF16))


def _merge(x2, ya, yh, ga, gh, wa, wh, wo, g):
    t = x2.shape[0]
    tm = min(MERGE_TOKENS, t)
    row = lambda w: pl.BlockSpec((tm, w), lambda i: (i, 0))
    full = lambda a: pl.BlockSpec(a.shape, lambda i: (0, 0))
    return pl.pallas_call(
        _merge_kernel,
        grid=(t // tm,),
        in_specs=[row(D_MODEL), row(ATTN_WIDTH), row(HGRN_WIDTH), row(D_MODEL), row(D_MODEL),
                  full(wa), full(wh), full(wo), full(g)],
        out_specs=(row(D_MODEL), pl.BlockSpec((D_MODEL // 2, tm), lambda i: (0, i))),
        out_shape=(jax.ShapeDtypeStruct((t, D_MODEL), F32),
                   jax.ShapeDtypeStruct((D_MODEL // 2, t), jnp.uint32)),
        compiler_params=_params(("parallel",)),
        name="merge",
    )(x2, ya, yh, ga, gh, wa, wh, wo, g)


def _as_f32(i):
    return jnp.asarray(i, jnp.int32).astype(F32)


def _topk_exact(s, k):
    n_rows, lanes = s.shape
    row = lax.broadcasted_iota(jnp.int32, (n_rows, lanes), 0).astype(F32)
    kid = lax.broadcasted_iota(jnp.int32, (k, lanes), 0).astype(F32)

    def body(a, carry):
        s, rank, vals = carry
        af = _as_f32(a)
        m = jnp.max(s, axis=0, keepdims=True)
        first = jnp.min(jnp.where(s == m, row, float(n_rows)), axis=0, keepdims=True)
        sel = row == first
        return jnp.where(sel, -jnp.inf, s), jnp.where(sel, af, rank), jnp.where(kid == af, m, vals)

    init = (s, jnp.full((n_rows, lanes), float(k), F32), jnp.zeros((k, lanes), F32))
    _, rank, vals = lax.fori_loop(0, k, body, init)
    return rank, vals


RANK_CODE_SCALE = 2.0 ** 123
RANK_CODE_LIMIT = -float(PEER_TOPK) * RANK_CODE_SCALE


def _topk_quick_pair(s1, s2, k):
    c1, c2, v1, v2 = s1, s2, [], []
    for a in range(k):
        code = -float(a + k) * RANK_CODE_SCALE
        m1 = jnp.max(c1, axis=0, keepdims=True)
        m2 = jnp.max(c2, axis=0, keepdims=True)
        c1 = jnp.where(c1 == m1, code, c1)
        c2 = jnp.where(c2 == m2, code, c2)
        v1.append(m1)
        v2.append(m2)
    v1 = jnp.concatenate(v1, axis=0)
    v2 = jnp.concatenate(v2, axis=0)

    def decode(c, s):
        marked = c <= RANK_CODE_LIMIT
        rank = jnp.where(marked, c * (-1.0 / RANK_CODE_SCALE) - float(k), float(k))
        n_marked = jnp.sum(jnp.where(marked, 1.0, 0.0), axis=0, keepdims=True)
        in_range = jnp.min(s, axis=0, keepdims=True) > RANK_CODE_LIMIT
        return rank, jnp.where(n_marked == float(k), jnp.where(in_range, 1.0, 0.0), 0.0)

    rank1, ok1 = decode(c1, s1)
    rank2, ok2 = decode(c2, s2)
    return rank1, v1, rank2, v2, ok1 * ok2


def _cand_row_lengths(k):
    return [k // (a + 1) for a in range(k)]


def _candidates(v1, v2, k):
    lengths = _cand_row_lengths(k)
    rows = [v1[a:a + 1] + v2[:n] for a, n in enumerate(lengths)]
    pad = -sum(lengths) % SUBLANES
    if pad:
        rows.append(jnp.full((pad, v1.shape[1]), -jnp.inf, F32))
    return jnp.concatenate(rows, axis=0)


def _choose_exact(cand, k):
    n_cand, lanes = cand.shape
    crow = lax.broadcasted_iota(jnp.int32, (n_cand, lanes), 0).astype(F32)

    def body(_, c):
        m = jnp.max(c, axis=0, keepdims=True)
        first = jnp.min(jnp.where(c == m, crow, float(n_cand)), axis=0, keepdims=True)
        return jnp.where(crow == first, -jnp.inf, c)

    return lax.fori_loop(0, k, body, cand)


def _choose_quick(cand, k):
    for _ in range(k):
        cand = jnp.where(cand == jnp.max(cand, axis=0, keepdims=True), -jnp.inf, cand)
    return cand


def _pack_rows(x):
    return pltpu.bitcast(x, jnp.uint32)


def _unpack_rows(x):
    return pltpu.bitcast(x, BF16)


def _pack_row_pairs(x):
    bits = lax.bitcast_convert_type(x, jnp.uint16).astype(jnp.uint32)
    return bits[0::2] | (bits[1::2] << 16)


def _pack_row_pairs_of_transpose(x):
    r, c = x.shape
    return jnp.transpose(lax.bitcast_convert_type(x.reshape(r, c // 2, 2), jnp.uint32))


def _route_outputs(s1, s2, rank1, v1, rank2, v2, cand, left, k):
    chosen = jnp.where(left == -jnp.inf, jnp.where(cand > -jnp.inf, 1.0, 0.0), 0.0)
    top = v1[0:1] + v2[0:1]
    z = jnp.sum(chosen * jnp.exp(cand - top), axis=0, keepdims=True)
    lengths = _cand_row_lengths(k)
    offsets = [sum(lengths[:a]) for a in range(k)]
    counts = [jnp.sum(chosen[offsets[a]:offsets[a] + lengths[a]], axis=0, keepdims=True) for a in range(k)]
    nonempty = sum(counts[a] if lengths[a] == 1 else jnp.minimum(counts[a], 1.0) for a in range(k))
    n1 = jnp.where(rank1 < nonempty, 1.0, 0.0)
    for a in range(k):
        if lengths[a] > 1:
            n1 = jnp.where(rank1 == float(a), counts[a], n1)
    e1 = jnp.exp(s1 - v1[0:1])
    r2 = _pack_rows(rank2.astype(BF16))
    e2 = _pack_rows((jnp.exp(s2 - v2[0:1]) / z).astype(BF16))
    return (n1, e1, r2, e2), jnp.sum(chosen, axis=0, keepdims=True)


def _route_head(hh, s1, s2, out_refs):
    k = PEER_TOPK
    tiles = [_topk_quick_pair(s1[:, lt * LANE_TILE:(lt + 1) * LANE_TILE],
                              s2[:, lt * LANE_TILE:(lt + 1) * LANE_TILE], k)
             for lt in range(s1.shape[1] // LANE_TILE)]
    rank1, v1, rank2, v2, ok = (jnp.concatenate(parts, axis=1) for parts in zip(*tiles))
    cand = _candidates(v1, v2, k)
    outs, n_chosen = _route_outputs(s1, s2, rank1, v1, rank2, v2, cand, _choose_quick(cand, k), k)
    for ref, val in zip(out_refs, outs):
        ref[hh] = val
    ok = ok * jnp.where(n_chosen == float(k), 1.0, 0.0)

    @pl.when(jnp.min(ok) < 1.0)
    def _():
        for lt in range(s1.shape[1] // LANE_TILE):
            lanes = slice(lt * LANE_TILE, (lt + 1) * LANE_TILE)

            @pl.when(jnp.min(ok[:, lanes]) < 1.0)
            def _():
                t1, t2 = s1[:, lanes], s2[:, lanes]
                rank1, v1 = _topk_exact(t1, k)
                rank2, v2 = _topk_exact(t2, k)
                cand = _candidates(v1, v2, k)
                outs, _ = _route_outputs(t1, t2, rank1, v1, rank2, v2, cand, _choose_exact(cand, k), k)
                for ref, val in zip(out_refs, outs):
                    ref[hh, :, lanes] = val


def _route_kernel(hnt_ref, wq_ref, k1_ref, k2_ref, n1_ref, e1_ref, r2_ref, e2_ref):
    kd = PEER_KEY_DIM
    q_t = jnp.dot(wq_ref[...], _unpack_rows(hnt_ref[...]), preferred_element_type=F32).astype(BF16)
    scores = [(jnp.dot(k1_ref[hh], q_t[2 * hh * kd:(2 * hh + 1) * kd], preferred_element_type=F32),
               jnp.dot(k2_ref[hh], q_t[(2 * hh + 1) * kd:(2 * hh + 2) * kd], preferred_element_type=F32))
              for hh in range(k1_ref.shape[0])]
    for hh, (s1, s2) in enumerate(scores):
        _route_head(hh, s1, s2, (n1_ref, e1_ref, r2_ref, e2_ref))


def _route(hnt, wq_t, keys1, keys2):
    t = hnt.shape[1]
    tb = min(ROUTE_TOKENS, t)
    hs = ROUTE_HEADS
    out = jax.ShapeDtypeStruct((PEER_HEADS, PEER_N_KEYS, t), F32)
    out16 = jax.ShapeDtypeStruct((PEER_HEADS, PEER_N_KEYS // 2, t), jnp.uint32)
    pspec = pl.BlockSpec((hs, PEER_N_KEYS // 2, tb), lambda i, h: (h, 0, i))
    ospec = pl.BlockSpec((hs, PEER_N_KEYS, tb), lambda i, h: (h, 0, i))
    kspec = pl.BlockSpec((hs, PEER_N_KEYS, PEER_KEY_DIM), lambda i, h: (h, 0, 0))
    return pl.pallas_call(
        _route_kernel,
        grid=(t // tb, PEER_HEADS // hs),
        in_specs=[pl.BlockSpec((D_MODEL // 2, tb), lambda i, h: (0, i)),
                  pl.BlockSpec((hs * 2 * PEER_KEY_DIM, D_MODEL), lambda i, h: (h, 0)),
                  kspec, kspec],
        out_specs=(ospec, ospec, pspec, pspec),
        out_shape=(out, out, out16, out16),
        compiler_params=_params(("parallel", "arbitrary")),
        name="route",
    )(hnt, wq_t, keys1, keys2)


def _peer_gates(n1_ref, e1_ref, r2_ref, e2_ref, firsts, act, d_ref, first0):
    nk = PEER_N_KEYS
    for lt in range(d_ref.shape[1] // LANE_TILE):
        lanes = slice(lt * LANE_TILE, (lt + 1) * LANE_TILE)
        ws = [None] * len(firsts)
        for h in range(PEER_HEADS):
            keys = slice(h * nk // 2, (h + 1) * nk // 2)
            r2 = _unpack_rows(r2_ref[keys, lanes])
            e2 = _unpack_rows(e2_ref[keys, lanes])
            for k, ii in enumerate(firsts):
                n1 = n1_ref[h, ii:ii + 1, lanes].astype(BF16)
                e1 = e1_ref[h, ii:ii + 1, lanes].astype(BF16)
                term = jnp.where(r2 < n1, e2, 0.0) * e1
                ws[k] = term if ws[k] is None else ws[k] + term
        for k, ii in enumerate(firsts):
            a = act[k * nk:(k + 1) * nk, lanes].astype(BF16)
            gelu = (0.5 * a) * (1.0 + lax.erf(a * (2.0 ** -0.5)))
            d_ref[(ii - first0) * nk // 2:(ii - first0 + 1) * nk // 2, lanes] = _pack_rows(ws[k] * gelu)


def _peer_kernel(hnt_ref, u_ref, vt_ref, n1_ref, e1_ref, r2_ref, e2_ref, x1_ref, g_ref,
                 o_ref, acc_ref, *d_refs):
    c = pl.program_id(1)
    nk = PEER_N_KEYS
    sub = PEER_SUB_CHUNK
    piece = sub // PEER_MXU_SPLIT
    n_sub = 2 * u_ref.shape[0] // sub

    @pl.when(c == 0)
    def _():
        acc_ref[...] = jnp.zeros_like(acc_ref)

    acts = {}
    hnt = _unpack_rows(hnt_ref[...])

    def first(s):
        for p in range(PEER_MXU_SPLIT):
            rows = slice(s * sub + p * piece, s * sub + (p + 1) * piece)
            u_rows = _unpack_rows(u_ref[rows.start // 2:rows.stop // 2, :])
            acts[s, p] = rows, jnp.dot(u_rows, hnt, preferred_element_type=F32)

    def gate_and_second(s):
        for p in range(PEER_MXU_SPLIT):
            rows, act = acts.pop((s, p))
            firsts = range(rows.start // nk, rows.stop // nk)
            _peer_gates(n1_ref, e1_ref, r2_ref, e2_ref, firsts, act, d_refs[s], s * sub // nk)
        d_t = _unpack_rows(d_refs[s][...])
        v_t = _unpack_rows(vt_ref[:, s * sub:(s + 1) * sub])
        acc_ref[...] += jnp.dot(v_t, d_t, preferred_element_type=F32)

    first(0)
    for s in range(n_sub):
        if s + 1 < n_sub:
            first(s + 1)
        gate_and_second(s)

    @pl.when(c == pl.num_programs(1) - 1)
    def _():
        y = x1_ref[...] + jnp.transpose(acc_ref[...])
        o_ref[...] = _rms(y, g_ref[...])


def _peer(hnt, u, v_t, n1, e1, r2, e2, x1, g):
    t = x1.shape[0]
    tb = min(PEER_TOKENS, t)
    ec = PEER_EXPERT_CHUNK
    packed_rows = PEER_HEADS * PEER_N_KEYS // 2
    dense = pl.BlockSpec((packed_rows, tb), lambda i, c: (0, i))
    r2 = r2.reshape(packed_rows, t)
    e2 = e2.reshape(packed_rows, t)
    rows = pl.BlockSpec((PEER_HEADS, ec // PEER_N_KEYS, tb), lambda i, c: (0, c, i))
    return pl.pallas_call(
        _peer_kernel,
        grid=(t // tb, PEER_N_EXPERTS // ec),
        in_specs=[pl.BlockSpec((D_MODEL // 2, tb), lambda i, c: (0, i)),
                  pl.BlockSpec((ec // 2, D_MODEL), lambda i, c: (c, 0)),
                  pl.BlockSpec((D_MODEL // 2, ec), lambda i, c: (0, c)),
                  rows, rows, dense, dense,
                  pl.BlockSpec((tb, D_MODEL), lambda i, c: (i, 0)),
                  pl.BlockSpec((1, D_MODEL), lambda i, c: (0, 0))],
        out_specs=pl.BlockSpec((tb, D_MODEL), lambda i, c: (i, 0)),
        out_shape=jax.ShapeDtypeStruct((t, D_MODEL), F32),
        scratch_shapes=([pltpu.VMEM((D_MODEL, tb), F32)]
                        + [pltpu.VMEM((PEER_SUB_CHUNK // 2, tb), jnp.uint32)] * (ec // PEER_SUB_CHUNK)),
        compiler_params=_params(("parallel", "arbitrary")),
        name="peer",
    )(hnt, u, v_t, n1, e1, r2, e2, x1, g)


def kernel(x, norm_mix_g, w_in, attn_sinks, hgrn_lb_logits, hgrn_norm_g, w_attn_proj, w_hgrn_proj,
           w_out, norm_ffn_g, w_peer_q, peer_keys, peer_u, peer_v, norm_final_g):
    b, s, d = x.shape
    assert d == D_MODEL and norm_mix_g.shape[0] == 1
    t = b * s
    x2 = x.reshape(t, d)
    row = lambda a: a.reshape(1, -1).astype(F32)

    aq, ak, av, hq, hf, hi, hg, ga, gh = _inproj(x2, row(norm_mix_g[0]), w_in[0].astype(BF16))
    y_attn = _attention(aq, ak, av, attn_sinks[0].astype(F32), b, s)
    y_hgrn = _hgrn(hq, hf, hi, hg, hgrn_lb_logits.astype(F32), row(hgrn_norm_g[0]), b, s)
    x1, hnt = _merge(x2, y_attn, y_hgrn, ga, gh, w_attn_proj[0].astype(BF16),
                     w_hgrn_proj[0].astype(BF16), w_out[0].astype(BF16), row(norm_ffn_g[0]))
    wq_t = jnp.transpose(w_peer_q[0]).astype(BF16)
    keys = peer_keys[0].astype(BF16)
    n1, e1, r2, e2 = _route(hnt, wq_t, keys[0], keys[1])
    out = _peer(hnt, _pack_row_pairs(peer_u[0].astype(BF16)),
                _pack_row_pairs_of_transpose(peer_v[0].astype(BF16)),
                n1, e1, r2, e2, x1, row(norm_final_g))
    return out.reshape(b, s, d)
```
